```python
import math
import jax, jax.numpy as jnp
from jax import lax
import numpy as np

D_MODEL = 1024
BATCH = 8
SEQ = 2048
DEPTH = 2
DEC_BATCH = 32
DEC_SEQ = 8
PAST_LEN = 8192
PAGE_SIZE = 128

D_MIX = D_MODEL
D_A = D_MIX // 2
H_A = 4
DV_A = D_A // H_A
DQK_A = DV_A // 2
D_B = D_MIX - D_A
H_B = 4
DK_B = D_B // H_B
DV_B = D_B // H_B
CONV_W = 4
CHUNK = 64
Q_BLOCK = 128
D_FF = 2816
N_EXPERTS = 8
TOP_K = 2
D_FF_E = 2816
ALPHA = (2 * DEPTH) ** 0.25
BETA_INIT = (8 * DEPTH) ** -0.25
LN_EPS = 1e-5
N_DENSE = (DEPTH + 1) // 2
N_MOE = DEPTH // 2

A_Q = 2 * H_A * DQK_A
A_K = 2 * H_A * DQK_A
A_V = H_A * DV_A
B_QKV = 3 * D_B
B_BETA = H_B
B_DECAY = H_B
B_GATE = D_B
D_IN = A_Q + A_K + A_V + B_QKV + B_BETA + B_DECAY + B_GATE
SPLIT_IDX = (A_Q, A_Q + A_K, A_Q + A_K + A_V, A_Q + A_K + A_V + B_QKV,
             A_Q + A_K + A_V + B_QKV + B_BETA, A_Q + A_K + A_V + B_QKV + B_BETA + B_DECAY)

kernel_name = 'hymba_diffattn_gdn_deepnorm_adaln_step'


def _layernorm(x):
    xf = x.astype(jnp.float32)
    mu = jnp.mean(xf, axis=-1, keepdims=True)
    var = jnp.mean(jnp.square(xf - mu), axis=-1, keepdims=True)
    return ((xf - mu) * lax.rsqrt(var + LN_EPS)).astype(x.dtype)


def _rmsnorm(x, g):
    xf = x.astype(jnp.float32)
    y = xf * lax.rsqrt(jnp.mean(xf * xf, axis=-1, keepdims=True) + 1e-6)
    return (y * g.astype(jnp.float32)).astype(x.dtype)


def _l2norm(x):
    xf = x.astype(jnp.float32)
    return xf * lax.rsqrt(jnp.sum(xf * xf, axis=-1, keepdims=True) + 1e-6)


def _alibi_slopes():
    return jnp.asarray([2.0 ** (-8.0 * (h + 1) / H_A) for h in range(H_A)], dtype=jnp.float32)


def _diff_attn_block(q, k, v, q_pos, k_pos, lam, slopes):
    s = jnp.einsum('bqhmd,bkhmd->bhmqk', q, k).astype(jnp.float32) * (DQK_A ** -0.5)
    dist = (q_pos[:, None] - k_pos[None, :]).astype(jnp.float32)
    s = s - slopes[None, :, None, None, None] * dist
    s = jnp.where(dist >= 0, s, -jnp.inf)
    p = jax.nn.softmax(s, axis=-1)
    a = p[:, :, 0] - lam * p[:, :, 1]
    return jnp.einsum('bhqk,bkhd->bqhd', a.astype(v.dtype), v)


def _diff_attention(q, k_all, v_all, q_pos, lam_p, subln_g, layer):
    B, L = q.shape[0], q.shape[1]
    lam_init = 0.8 - 0.6 * math.exp(-0.3 * layer)
    lp = lam_p.astype(jnp.float32)
    lam = jnp.exp(jnp.sum(lp[0] * lp[1])) - jnp.exp(jnp.sum(lp[2] * lp[3])) + lam_init
    k_all = k_all.reshape(B, k_all.shape[1], H_A, 2, DQK_A)
    k_pos = jnp.arange(k_all.shape[1], dtype=jnp.int32)
    slopes = _alibi_slopes()
    blk = Q_BLOCK if L % Q_BLOCK == 0 else L
    nblk = L // blk
    qb = q.reshape(B, nblk, blk, H_A, 2, DQK_A).swapaxes(0, 1)
    pb = q_pos.reshape(nblk, blk)
    o = lax.map(lambda a: _diff_attn_block(a[0], k_all, v_all, a[1], k_pos, lam, slopes), (qb, pb))
    o = o.swapaxes(0, 1).reshape(B, L, H_A, DV_A)
    o = _rmsnorm(o, subln_g) * (1.0 - lam_init)
    return o.reshape(B, L, D_A)


def _short_conv(buf, x, w):
    L = x.shape[1]
    xx = jnp.concatenate([buf.astype(x.dtype), x], axis=1)
    y = xx[:, 0:L] * w[0]
    for i in range(1, CONV_W):
        y = y + xx[:, i:i + L] * w[i]
    return jax.nn.silu(y), xx[:, L:]


def _to_chunks(t, n, c):
    t = t.reshape((t.shape[0], n, c) + t.shape[2:])
    return jnp.moveaxis(t, 3, 1)


def _chunked_gated_delta(q, k, v, beta, g, s0):
    B, L = q.shape[0], q.shape[1]
    C = min(CHUNK, L)
    n = -(-L // C)
    pad = n * C - L
    if pad:
        pw = lambda t: jnp.pad(t, [(0, 0), (0, pad)] + [(0, 0)] * (t.ndim - 2))
        q, k, v, beta, g = pw(q), pw(k), pw(v), pw(beta), pw(g)
    q, k, v = _to_chunks(q, n, C), _to_chunks(k, n, C), _to_chunks(v, n, C)
    beta, g = _to_chunks(beta, n, C), _to_chunks(g, n, C)
    G = jnp.cumsum(g, axis=-1)
    diff = G[..., :, None] - G[..., None, :]
    tril = jnp.tril(jnp.ones((C, C), dtype=bool))
    strict = jnp.tril(jnp.ones((C, C), dtype=bool), -1)
    decay = jnp.exp(jnp.where(tril, diff, -jnp.inf))
    kb = k * beta[..., None]
    lmat = jnp.where(strict, jnp.einsum('bhnid,bhnjd->bhnij', kb, k) * decay, 0.0)
    lhs = lmat + jnp.eye(C, dtype=jnp.float32)
    rhs = jnp.concatenate([v * beta[..., None], kb * jnp.exp(G)[..., None]], axis=-1)
    sol = lax.linalg.triangular_solve(lhs, rhs, left_side=True, lower=True, unit_diagonal=True)
    u, w = sol[..., :DV_B], sol[..., DV_B:]
    qk = jnp.einsum('bhnid,bhnjd->bhnij', q, k) * decay
    q_dec = q * jnp.exp(G)[..., None]
    k_dec = k * jnp.exp(G[..., -1:] - G)[..., None]
    g_tot = jnp.exp(G[..., -1])

    def step(S, xs):
        u_c, w_c, qk_c, qd_c, kd_c, gt_c = xs
        v_new = u_c - jnp.einsum('bhck,bhkv->bhcv', w_c, S)
        o = jnp.einsum('bhck,bhkv->bhcv', qd_c, S) + jnp.einsum('bhij,bhjv->bhiv', qk_c, v_new)
        S = S * gt_c[..., None, None] + jnp.einsum('bhck,bhcv->bhkv', kd_c, v_new)
        return S, o

    xs = tuple(jnp.moveaxis(t, 2, 0) for t in (u, w, qk, q_dec, k_dec, g_tot))
    s_fin, o = lax.scan(step, s0, xs)
    o = jnp.transpose(o, (1, 0, 3, 2, 4)).reshape(B, n * C, H_B, DV_B)[:, :L]
    return o, s_fin


def _gated_deltanet(qkv, b_raw, a_raw, z, conv_buf, s0, conv_w_l, a_log_l, dt_bias_l, norm_g):
    B, L = qkv.shape[0], qkv.shape[1]
    y, new_buf = _short_conv(conv_buf, qkv, conv_w_l)
    qc, kc, vc = jnp.split(y, 3, axis=-1)
    q = _l2norm(qc.reshape(B, L, H_B, DK_B)) * (DK_B ** -0.5)
    k = _l2norm(kc.reshape(B, L, H_B, DK_B))
    v = vc.reshape(B, L, H_B, DV_B).astype(jnp.float32)
    beta = jax.nn.sigmoid(b_raw.astype(jnp.float32))
    g = -jnp.exp(a_log_l.astype(jnp.float32)) * jax.nn.softplus(a_raw.astype(jnp.float32) + dt_bias_l.astype(jnp.float32))
    o, s_new = _chunked_gated_delta(q, k, v, beta, g, s0.astype(jnp.float32))
    o = _rmsnorm(o, norm_g) * jax.nn.silu(z.reshape(B, L, H_B, DV_B).astype(jnp.float32))
    return o.reshape(B, L, D_B).astype(qkv.dtype), new_buf, s_new.astype(s0.dtype)


def _swiglu(x, w_up, w_down):
    gt, up = jnp.split(x @ w_up, 2, axis=-1)
    return (jax.nn.silu(gt) * up) @ w_down


def _moe_swiglu(x, w_r, b_r, w_up, w_down):
    logits = (x @ w_r + b_r).astype(jnp.float32)
    top_v, top_i = lax.top_k(logits, TOP_K)
    gates = jax.nn.softmax(top_v, axis=-1)
    gate_e = jnp.sum(jax.nn.one_hot(top_i, N_EXPERTS, dtype=jnp.float32) * gates[..., None], axis=-2).astype(x.dtype)
    y = jnp.zeros_like(x)
    for e in range(N_EXPERTS):
        y = y + gate_e[..., e:e + 1] * _swiglu(x, w_up[e], w_down[e])
    return y


def _trunk(x, c, pos0, conv_bufs, delta_states, past_k, past_v, weights):
    (w_in, w_out, lam_params, subln_a, conv_w, a_log, dt_bias, norm_b, ada_w, ada_b,
     ln1_g, ln1_b, ln2_g, ln2_b, dense_w_up, dense_w_down,
     moe_router, moe_router_b, moe_w_up, moe_w_down) = weights
    B, L = x.shape[0], x.shape[1]
    q_pos = pos0 + jnp.arange(L, dtype=jnp.int32)
    new_k, new_v, new_conv, new_delta = [], [], [], []
    for l in range(DEPTH):
        mod = jax.nn.silu(c) @ ada_w[l] + ada_b[l]
        sh1, sc1, g1, sh2, sc2, g2 = [m[:, None, :] for m in jnp.split(mod, 6, axis=-1)]
        h = _layernorm(x) * (1.0 + sc1) + sh1
        proj = h @ w_in[l]
        qa, ka, va, qkv_b, b_raw, a_raw, z = jnp.split(proj, SPLIT_IDX, axis=-1)
        k_heads = ka.reshape(B, L, H_A, 2 * DQK_A)
        v_heads = va.reshape(B, L, H_A, DV_A)
        new_k.append(k_heads)
        new_v.append(v_heads)
        if past_k is None:
            k_all, v_all = k_heads, v_heads
        else:
            k_all = jnp.concatenate([past_k[l].astype(k_heads.dtype), k_heads], axis=1)
            v_all = jnp.concatenate([past_v[l].astype(v_heads.dtype), v_heads], axis=1)
        o_a = _diff_attention(qa.reshape(B, L, H_A, 2, DQK_A), k_all, v_all, q_pos, lam_params[l], subln_a[l], l)
        o_b, buf_l, s_l = _gated_deltanet(qkv_b, b_raw, a_raw, z, conv_bufs[l], delta_states[l],
                                          conv_w[l], a_log[l], dt_bias[l], norm_b[l])
        new_conv.append(buf_l)
        new_delta.append(s_l)
        mix = jnp.concatenate([o_a, o_b], axis=-1) @ w_out[l]
        x = _layernorm(ALPHA * x + g1 * mix) * ln1_g[l] + ln1_b[l]
        h2 = _layernorm(x) * (1.0 + sc2) + sh2
        if l % 2 == 0:
            f = _swiglu(h2, dense_w_up[l // 2], dense_w_down[l // 2])
        else:
            j = l // 2
            f = _moe_swiglu(h2, moe_router[j], moe_router_b[j], moe_w_up[j], moe_w_down[j])
        x = _layernorm(ALPHA * x + g2 * f) * ln2_g[l] + ln2_b[l]
    return x, jnp.stack(new_k), jnp.stack(new_v), jnp.stack(new_conv), jnp.stack(new_delta)


def setup_inputs(seed: int = 0) -> dict:
    key = jax.random.key(seed)
    ks = iter(jax.random.split(key, 40))
    f32 = jnp.float32

    def nrm(shape, s):
        return jax.random.normal(next(ks), shape, f32) * s

    n_pages = PAST_LEN // PAGE_SIZE
    n_used = DEC_BATCH * n_pages
    n_pool = (n_used * 5) // 4
    x_prompt = nrm((BATCH, SEQ, D_MODEL), 1.0)
    x_sample = nrm((DEC_BATCH, DEC_SEQ, D_MODEL), 1.0)
    cache_k = nrm((DEPTH, n_pool, PAGE_SIZE, H_A, 2 * DQK_A), 1.0)
    cache_v = nrm((DEPTH, n_pool, PAGE_SIZE, H_A, DV_A), 0.5)
    state_conv = nrm((DEPTH, DEC_BATCH, CONV_W - 1, 3 * D_B), 0.5)
    state_delta = nrm((DEPTH, DEC_BATCH, H_B, DK_B, DV_B), 0.05)
    page_table = jax.random.permutation(next(ks), n_pool)[:n_used].reshape(DEC_BATCH, n_pages).astype(jnp.int32)
    c_prompt = nrm((BATCH, D_MODEL), 1.0)
    c_sample = nrm((DEC_BATCH, D_MODEL), 1.0)
    col_scale = jnp.concatenate([jnp.ones((A_Q + A_K,), f32), jnp.full((A_V,), BETA_INIT, f32),
                                 jnp.ones((2 * D_B,), f32), jnp.full((D_B,), BETA_INIT, f32),
                                 jnp.ones((B_BETA + B_DECAY + B_GATE,), f32)])
    w_in = nrm((DEPTH, D_MODEL, D_IN), D_MODEL ** -0.5) * col_scale
    w_out = nrm((DEPTH, D_MIX, D_MODEL), BETA_INIT * D_MIX ** -0.5)
    lam_params = nrm((DEPTH, 4, DQK_A), 0.1)
    subln_a = 1.0 + nrm((DEPTH, DV_A), 0.02)
    conv_w = nrm((DEPTH, CONV_W, 3 * D_B), CONV_W ** -0.5)
    a_log = jnp.log(jax.random.uniform(next(ks), (DEPTH, H_B), f32, minval=1.0, maxval=16.0))
    dt = jnp.exp(jax.random.uniform(next(ks), (DEPTH, H_B), f32, minval=math.log(1e-3), maxval=math.log(1e-1)))
    dt_bias = dt + jnp.log(-jnp.expm1(-dt))
    norm_b = 1.0 + nrm((DEPTH, DV_B), 0.02)
    ada_w = nrm((DEPTH, D_MODEL, 6 * D_MODEL), 0.5 * D_MODEL ** -0.5)
    ada_b = nrm((DEPTH, 6 * D_MODEL), 0.01)
    ln1_g = 1.0 + nrm((DEPTH, D_MODEL), 0.02)
    ln1_b = nrm((DEPTH, D_MODEL), 0.02)
    ln2_g = 1.0 + nrm((DEPTH, D_MODEL), 0.02)
    ln2_b = nrm((DEPTH, D_MODEL), 0.02)
    dense_w_up = nrm((N_DENSE, D_MODEL, 2 * D_FF), D_MODEL ** -0.5)
    dense_w_down = nrm((N_DENSE, D_FF, D_MODEL), BETA_INIT * D_FF ** -0.5)
    moe_router = nrm((N_MOE, D_MODEL, N_EXPERTS), D_MODEL ** -0.5)
    moe_router_b = nrm((N_MOE, N_EXPERTS), 0.01)
    moe_w_up = nrm((N_MOE, N_EXPERTS, D_MODEL, 2 * D_FF_E), D_MODEL ** -0.5)
    moe_w_down = nrm((N_MOE, N_EXPERTS, D_FF_E, D_MODEL), BETA_INIT * D_FF_E ** -0.5)
    return {'x_prompt': x_prompt, 'x_sample': x_sample, 'cache_k': cache_k, 'cache_v': cache_v,
            'state_conv': state_conv, 'state_delta': state_delta, 'page_table': page_table,
            'c_prompt': c_prompt, 'c_sample': c_sample, 'w_in': w_in, 'w_out': w_out,
            'lam_params': lam_params, 'subln_a': subln_a, 'conv_w': conv_w, 'a_log': a_log,
            'dt_bias': dt_bias, 'norm_b': norm_b, 'ada_w': ada_w, 'ada_b': ada_b,
            'ln1_g': ln1_g, 'ln1_b': ln1_b, 'ln2_g': ln2_g, 'ln2_b': ln2_b,
            'dense_w_up': dense_w_up, 'dense_w_down': dense_w_down,
            'moe_router': moe_router, 'moe_router_b': moe_router_b,
            'moe_w_up': moe_w_up, 'moe_w_down': moe_w_down}


def reference(x_prompt, x_sample, cache_k, cache_v, state_conv, state_delta, page_table,
              c_prompt, c_sample, w_in, w_out, lam_params, subln_a, conv_w, a_log, dt_bias,
              norm_b, ada_w, ada_b, ln1_g, ln1_b, ln2_g, ln2_b, dense_w_up, dense_w_down,
              moe_router, moe_router_b, moe_w_up, moe_w_down):
    weights = (w_in, w_out, lam_params, subln_a, conv_w, a_log, dt_bias, norm_b, ada_w, ada_b,
               ln1_g, ln1_b, ln2_g, ln2_b, dense_w_up, dense_w_down,
               moe_router, moe_router_b, moe_w_up, moe_w_down)
    nbp = x_prompt.shape[0]
    conv0 = jnp.zeros((DEPTH, nbp, CONV_W - 1, 3 * D_B), x_prompt.dtype)
    delta0 = jnp.zeros((DEPTH, nbp, H_B, DK_B, DV_B), x_prompt.dtype)
    y_prompt, k_prompt, v_prompt, conv_prompt, delta_prompt = _trunk(
        x_prompt, c_prompt, 0, conv0, delta0, None, None, weights)
    nbs, n_pages = page_table.shape
    past_len = n_pages * cache_k.shape[2]
    past_k = [cache_k[l][page_table].reshape(nbs, past_len, H_A, 2 * DQK_A) for l in range(DEPTH)]
    past_v = [cache_v[l][page_table].reshape(nbs, past_len, H_A, DV_A) for l in range(DEPTH)]
    y_sample, k_sample, v_sample, conv_sample, delta_sample = _trunk(
        x_sample, c_sample, past_len, state_conv, state_delta, past_k, past_v, weights)
    return (y_prompt, y_sample, k_prompt, v_prompt, conv_prompt, delta_prompt,
            k_sample, v_sample, conv_sample, delta_sample)
```

```python
import functools
import math

import jax
import jax.numpy as jnp
from jax import lax
from jax.experimental import pallas as pl
from jax.experimental.pallas import tpu as pltpu

F32 = jnp.float32
BF16 = jnp.bfloat16

D_MODEL = 1024
H_A = 4
DV_A = 128
DQK_A = 64
D_A = H_A * DV_A
H_B = 4
DK_B = 128
DV_B = 128
D_B = H_B * DV_B
CONV_W = 4
CHUNK = 64
D_FF = 2816
N_EXPERTS = 8
LN_EPS = 1e-5
PAGE = 128
LANES = 128
SUBLANES = 8
VMEM_LIMIT = 56 * 1024 * 1024
NEG_INF = float("-inf")
HIGHEST = lax.Precision.HIGHEST

IN_SEGS = (("q", 0, 512), ("k", 512, 1024), ("v", 1024, 1536), ("qkvb", 1536, 3072),
           ("z", 3072, 3584), ("ba", 3584, 3712))
D_IN_PAD = 3712


def _cparams(sem):
    return pltpu.CompilerParams(dimension_semantics=sem, vmem_limit_bytes=VMEM_LIMIT)


def _ln(x):
    mu = jnp.mean(x, axis=-1, keepdims=True)
    xc = x - mu
    var = jnp.mean(xc * xc, axis=-1, keepdims=True)
    return xc * lax.rsqrt(var + LN_EPS)


def _silu(x):
    return x * jax.nn.sigmoid(x)


def _row_tile(nb, seq, target):
    if seq >= target:
        assert seq % target == 0
        return 1, target
    tb = max(1, min(nb, target // seq))
    while nb % tb:
        tb -= 1
    return tb, seq


def _ada_kernel(c_ref, w_ref, b_ref, o_ref):
    a = _silu(c_ref[...]).astype(BF16)
    o_ref[0, 0] = jnp.dot(a, w_ref[0].astype(BF16), preferred_element_type=F32) + b_ref[0, 0]


def _ada_mod(c_all, ada_w, ada_b):
    depth = ada_w.shape[0]
    nb = c_all.shape[0]
    return pl.pallas_call(
        _ada_kernel,
        grid=(depth, 6),
        in_specs=[pl.BlockSpec((nb, D_MODEL), lambda l, s: (0, 0)),
                  pl.BlockSpec((1, D_MODEL, D_MODEL), lambda l, s: (l, 0, s)),
                  pl.BlockSpec((1, 1, 1, D_MODEL), lambda l, s: (l, s, 0, 0))],
        out_specs=pl.BlockSpec((1, 1, nb, D_MODEL), lambda l, s: (l, s, 0, 0)),
        out_shape=jax.ShapeDtypeStruct((depth, 6, nb, D_MODEL), F32),
        compiler_params=_cparams(("arbitrary", "arbitrary")),
        name="ada_mod",
    )(c_all, ada_w, ada_b.reshape(depth, 6, 1, D_MODEL))


def _inproj_kernel(x_ref, sh_ref, sc_ref, w_ref, q_ref, k_ref, v_ref, qkvb_ref, z_ref, ba_ref, *, tb, tl):
    h = _ln(x_ref[...]) * (1.0 + sc_ref[...]) + sh_ref[...]
    hb = h.reshape(tb * tl, D_MODEL).astype(BF16)
    outs = dict(q=q_ref, k=k_ref, v=v_ref, qkvb=qkvb_ref, z=z_ref, ba=ba_ref)
    for name, lo, hi in IN_SEGS:
        r = jnp.dot(hb, w_ref[:, lo:hi], preferred_element_type=F32)
        outs[name][...] = r.reshape(tb, tl, hi - lo).astype(outs[name].dtype)


def _in_proj(x, sh, sc, w_perm, tm):
    nb, seq, _ = x.shape
    tb, tl = _row_tile(nb, seq, tm)
    grid = (nb // tb, seq // tl)
    row = lambda w: pl.BlockSpec((tb, tl, w), lambda i, j: (i, j, 0))
    mod = pl.BlockSpec((tb, 1, D_MODEL), lambda i, j: (i, 0, 0))
    widths = [hi - lo for _, lo, hi in IN_SEGS]
    dtypes = [BF16, F32, F32, F32, F32, F32]
    return pl.pallas_call(
        functools.partial(_inproj_kernel, tb=tb, tl=tl),
        grid=grid,
        in_specs=[row(D_MODEL), mod, mod, pl.BlockSpec((D_MODEL, D_IN_PAD), lambda i, j: (0, 0))],
        out_specs=[row(w) for w in widths],
        out_shape=[jax.ShapeDtypeStruct((nb, seq, w), dt) for w, dt in zip(widths, dtypes)],
        compiler_params=_cparams(("arbitrary", "arbitrary")),
        name="in_proj",
    )(x, sh, sc, w_perm)


def _lam_value(lam_ref, lam_init):
    lp = lam_ref[...]
    a = jnp.sum(lp[0:1] * lp[1:2], axis=-1, keepdims=True)
    b = jnp.sum(lp[2:3] * lp[3:4], axis=-1, keepdims=True)
    return jnp.exp(a) - jnp.exp(b) + lam_init


def _sub_ln(o, g, lam_init):
    y = o * lax.rsqrt(jnp.mean(o * o, axis=-1, keepdims=True) + 1e-6)
    return (y * g) * (1.0 - lam_init)


def _attn_kernel(q_ref, k_ref, v_ref, sl_ref, lam_ref, g_ref, o_ref,
                 kb_ref, vb_ref, m_ref, l_ref, acc_ref, *, tq, lam_init):
    qi = pl.program_id(2)

    @pl.when(qi == 0)
    def _():
        kb_ref[...] = k_ref[0].astype(BF16)
        vb_ref[...] = v_ref[0].astype(BF16)

    slope = sl_ref[0][:, 0:1]
    qs = (q_ref[0].astype(F32) * (DQK_A ** -0.5)).astype(BF16)
    lane = lax.broadcasted_iota(jnp.int32, (tq, 2 * DQK_A), 1)
    zero = jnp.zeros_like(qs)
    q2 = jnp.concatenate([jnp.where(lane < DQK_A, qs, zero), jnp.where(lane >= DQK_A, qs, zero)], axis=0)
    r = lax.broadcasted_iota(jnp.int32, (2 * tq, tq), 0)
    c = lax.broadcasted_iota(jnp.int32, (2 * tq, tq), 1)
    rc = jnp.where(r >= tq, r - tq, r) - c
    alibi = slope * rc.astype(F32)

    m_ref[...] = jnp.full_like(m_ref, NEG_INF)
    l_ref[...] = jnp.zeros_like(l_ref)
    acc_ref[...] = jnp.zeros_like(acc_ref)

    def step(j, masked):
        start = pl.multiple_of(j * tq, tq)
        kj = kb_ref[pl.ds(start, tq), :]
        vj = vb_ref[pl.ds(start, tq), :]
        s = lax.dot_general(q2, kj, (((1,), (1,)), ((), ())), preferred_element_type=F32)
        s = s - alibi - slope * ((qi - j) * tq).astype(F32)
        if masked:
            s = jnp.where(rc >= 0, s, NEG_INF)
        m_prev = m_ref[...]
        m_new = jnp.maximum(m_prev, jnp.max(s, axis=-1, keepdims=True))
        a = jnp.exp(m_prev - m_new)
        p = jnp.exp(s - m_new)
        l_ref[...] = a * l_ref[...] + jnp.sum(p, axis=-1, keepdims=True)
        acc_ref[...] = a * acc_ref[...] + jnp.dot(p.astype(BF16), vj, preferred_element_type=F32)
        m_ref[...] = m_new

    def body(j, carry):
        step(j, False)
        return carry

    lax.fori_loop(0, qi, body, 0)
    step(qi, True)

    o = acc_ref[...] / l_ref[...]
    od = o[:tq] - _lam_value(lam_ref, lam_init) * o[tq:]
    o_ref[0] = _sub_ln(od, g_ref[...], lam_init).astype(o_ref.dtype)


def _attn_prompt(q, k, v, slopes, lam_p, subln_g, lam_init, tq):
    nb, seq, _ = q.shape
    tq = min(tq, seq)
    assert seq % tq == 0
    grid = (nb, H_A, seq // tq)
    return pl.pallas_call(
        functools.partial(_attn_kernel, tq=tq, lam_init=lam_init),
        grid=grid,
        in_specs=[pl.BlockSpec((1, tq, DV_A), lambda b, h, i: (b, i, h)),
                  pl.BlockSpec((1, seq, DV_A), lambda b, h, i: (b, 0, h)),
                  pl.BlockSpec((1, seq, DV_A), lambda b, h, i: (b, 0, h)),
                  pl.BlockSpec((1, 1, LANES), lambda b, h, i: (h, 0, 0)),
                  pl.BlockSpec((4, DQK_A), lambda b, h, i: (0, 0)),
                  pl.BlockSpec((1, DV_A), lambda b, h, i: (0, 0))],
        out_specs=pl.BlockSpec((1, tq, DV_A), lambda b, h, i: (b, i, h)),
        out_shape=jax.ShapeDtypeStruct((nb, seq, D_A), BF16),
        scratch_shapes=[pltpu.VMEM((seq, DV_A), BF16), pltpu.VMEM((seq, DV_A), BF16),
                        pltpu.VMEM((2 * tq, 1), F32), pltpu.VMEM((2 * tq, 1), F32),
                        pltpu.VMEM((2 * tq, DV_A), F32)],
        compiler_params=_cparams(("arbitrary", "arbitrary", "arbitrary")),
        name="attn_prompt",
    )(q, k, v, slopes, lam_p, subln_g.reshape(1, DV_A))


PAGES_PER_STEP = 8
QROWS = 2 * H_A * 8


def _decode_kernel(pt_ref, q_ref, kn_ref, vn_ref, lam_ref, g_ref, *rest, nq, past_len, lam_init, ngroups):
    kp_refs = rest[:PAGES_PER_STEP]
    vp_refs = rest[PAGES_PER_STEP:2 * PAGES_PER_STEP]
    o_ref = rest[2 * PAGES_PER_STEP]
    qbd_ref, m_ref, l_ref, acc_ref, pad_k_ref, pad_v_ref = rest[2 * PAGES_PER_STEP + 1:]
    g = pl.program_id(1)
    rows = 2 * H_A * nq

    row = lax.broadcasted_iota(jnp.int32, (rows, 1), 0)
    head = row // (2 * nq)
    slope = jnp.exp2(-8.0 * (head + 1).astype(F32) / H_A)
    qpos = past_len + row % nq

    @pl.when(g == 0)
    def _():
        qs = (q_ref[0].astype(F32) * (DQK_A ** -0.5)).astype(BF16)
        lane = lax.broadcasted_iota(jnp.int32, (nq, D_A), 1)
        zero = jnp.zeros_like(qs)
        for hm in range(2 * H_A):
            keep = (lane >= hm * DQK_A) & (lane < (hm + 1) * DQK_A)
            qbd_ref[hm * nq:(hm + 1) * nq, :] = jnp.where(keep, qs, zero)
        m_ref[...] = jnp.full_like(m_ref, NEG_INF)
        l_ref[...] = jnp.zeros_like(l_ref)
        acc_ref[...] = jnp.zeros_like(acc_ref)

    qbd = qbd_ref[...]

    def update(kb, vb, kpos, valid):
        s = lax.dot_general(qbd, kb, (((1,), (1,)), ((), ())), preferred_element_type=F32)
        s = s - slope * (qpos - kpos).astype(F32)
        if valid is not None:
            s = jnp.where(valid, s, NEG_INF)
        m_prev = m_ref[...]
        m_new = jnp.maximum(m_prev, jnp.max(s, axis=-1, keepdims=True))
        a = jnp.exp(m_prev - m_new)
        p = jnp.exp(s - m_new)
        l_ref[...] = a * l_ref[...] + jnp.sum(p, axis=-1, keepdims=True)
        acc_ref[...] = a * acc_ref[...] + jnp.dot(p.astype(BF16), vb, preferred_element_type=F32)
        m_ref[...] = m_new

    col = lax.broadcasted_iota(jnp.int32, (1, PAGE), 1)
    for i in range(PAGES_PER_STEP):
        kpos = (g * PAGES_PER_STEP + i) * PAGE + col
        update(kp_refs[i][0, 0].astype(BF16), vp_refs[i][0, 0].astype(BF16), kpos, None)

    @pl.when(g == ngroups - 1)
    def _():
        pad_k_ref[...] = jnp.zeros_like(pad_k_ref)
        pad_v_ref[...] = jnp.zeros_like(pad_v_ref)
        pad_k_ref[0:nq, :] = kn_ref[0]
        pad_v_ref[0:nq, :] = vn_ref[0]
        kpos = past_len + col
        valid = (kpos <= qpos) & (col < nq)
        update(pad_k_ref[...].astype(BF16), pad_v_ref[...].astype(BF16), kpos, valid)
        o = acc_ref[...] / l_ref[...]
        lam = _lam_value(lam_ref, lam_init)
        for h in range(H_A):
            o1 = o[(2 * h) * nq:(2 * h + 1) * nq, h * DV_A:(h + 1) * DV_A]
            o2 = o[(2 * h + 1) * nq:(2 * h + 2) * nq, h * DV_A:(h + 1) * DV_A]
            y = _sub_ln(o1 - lam * o2, g_ref[...], lam_init)
            o_ref[0, :, h * DV_A:(h + 1) * DV_A] = y.astype(o_ref.dtype)


def _attn_decode(q, k_new, v_new, cache_k, cache_v, layer, page_table, lam_p, subln_g, lam_init):
    nb, nq, _ = q.shape
    n_pages = page_table.shape[1]
    assert n_pages % PAGES_PER_STEP == 0
    ngroups = n_pages // PAGES_PER_STEP
    past_len = n_pages * PAGE
    rows = 2 * H_A * nq

    def page_spec(i):
        return pl.BlockSpec((1, 1, PAGE, D_A),
                            lambda b, g, pt: (layer, pt[b, g * PAGES_PER_STEP + i], 0, 0))

    per_b = lambda w: pl.BlockSpec((1, nq, w), lambda b, g, pt: (b, 0, 0))
    grid_spec = pltpu.PrefetchScalarGridSpec(
        num_scalar_prefetch=1,
        grid=(nb, ngroups),
        in_specs=[per_b(D_A), per_b(D_A), per_b(D_A),
                  pl.BlockSpec((4, DQK_A), lambda b, g, pt: (0, 0)),
                  pl.BlockSpec((1, DV_A), lambda b, g, pt: (0, 0))]
                 + [page_spec(i) for i in range(PAGES_PER_STEP)]
                 + [page_spec(i) for i in range(PAGES_PER_STEP)],
        out_specs=per_b(D_A),
        scratch_shapes=[pltpu.VMEM((rows, D_A), BF16), pltpu.VMEM((rows, 1), F32), pltpu.VMEM((rows, 1), F32),
                        pltpu.VMEM((rows, D_A), F32), pltpu.VMEM((PAGE, D_A), F32), pltpu.VMEM((PAGE, D_A), F32)],
    )
    return pl.pallas_call(
        functools.partial(_decode_kernel, nq=nq, past_len=past_len, lam_init=lam_init, ngroups=ngroups),
        grid_spec=grid_spec,
        out_shape=jax.ShapeDtypeStruct((nb, nq, D_A), BF16),
        compiler_params=_cparams(("arbitrary", "arbitrary")),
        name="attn_decode",
    )(page_table, q, k_new, v_new, lam_p, subln_g.reshape(1, DV_A),
      *([cache_k] * PAGES_PER_STEP), *([cache_v] * PAGES_PER_STEP))


def _mm(a, b):
    return jnp.dot(a, b, preferred_element_type=F32, precision=HIGHEST)


def _mm_nt(a, b):
    return lax.dot_general(a, b, (((1,), (1,)), ((), ())), preferred_element_type=F32, precision=HIGHEST)


def _mm_tn(a, b):
    return lax.dot_general(a, b, (((0,), (0,)), ((), ())), preferred_element_type=F32, precision=HIGHEST)


def _unit_lower_inverse(lmat, csz):
    r = lax.broadcasted_iota(jnp.int32, (csz, csz), 0)
    c = lax.broadcasted_iota(jnp.int32, (csz, csz), 1)
    eye = (r == c).astype(F32)
    same_blk = (r // SUBLANES) == (c // SUBLANES)
    ld = jnp.where(same_blk, lmat, 0.0)
    dinv = eye - ld
    pw = ld
    for _ in range(2):
        pw = _mm(pw, pw)
        dinv = _mm(dinv, eye + pw)
    nblk = csz // SUBLANES
    if nblk == 1:
        return dinv
    n = _mm(dinv, lmat - ld)
    out = eye - n
    pw = n
    k = 2
    while k < nblk:
        pw = _mm(pw, pw)
        out = _mm(out, eye + pw)
        k *= 2
    return _mm(out, dinv)


def _gdn_kernel(xq_ref, xk_ref, xv_ref, bq_ref, bk_ref, bv_ref, wq_ref, wk_ref, wv_ref,
                ba_ref, z_ref, s0_ref, hp_ref, ng_ref, o_ref, sn_ref,
                xx_ref, qn_ref, kn_ref, vn_ref, beta_ref, g_ref, zz_ref, oo_ref, *, seq, seq_pad, csz):
    h = pl.program_id(1)

    def conv(x_ref, b_ref, w_ref):
        xx_ref[0:SUBLANES, :] = jnp.zeros((SUBLANES, LANES), F32)
        xx_ref[SUBLANES - (CONV_W - 1):SUBLANES, :] = b_ref[0]
        xx_ref[SUBLANES:SUBLANES + seq, :] = x_ref[0]
        w = w_ref[...]
        y = w[CONV_W - 1:CONV_W] * xx_ref[SUBLANES:SUBLANES + seq, :]
        for i in range(CONV_W - 1):
            off = SUBLANES - (CONV_W - 1) + i
            y = y + w[i:i + 1] * xx_ref[off:off + seq, :]
        return _silu(y)

    def l2n(t):
        return t * lax.rsqrt(jnp.sum(t * t, axis=-1, keepdims=True) + 1e-6)

    if seq_pad > seq:
        for ref in (qn_ref, kn_ref, vn_ref, beta_ref, g_ref, zz_ref):
            ref[...] = jnp.zeros_like(ref)
    qn_ref[0:seq, :] = l2n(conv(xq_ref, bq_ref, wq_ref)) * (DK_B ** -0.5)
    kn_ref[0:seq, :] = l2n(conv(xk_ref, bk_ref, wk_ref))
    vn_ref[0:seq, :] = conv(xv_ref, bv_ref, wv_ref)
    zz_ref[0:seq, :] = z_ref[0]

    ba = ba_ref[0]
    lane = lax.broadcasted_iota(jnp.int32, ba.shape, 1)
    b_raw = jnp.sum(jnp.where(lane == h, ba, 0.0), axis=-1, keepdims=True)
    a_raw = jnp.sum(jnp.where(lane == h + H_B, ba, 0.0), axis=-1, keepdims=True)
    a_log = hp_ref[0, 0:1, :]
    dt_bias = hp_ref[0, 1:2, :]
    xsp = a_raw + dt_bias
    softplus = jnp.maximum(xsp, 0.0) + jnp.log1p(jnp.exp(-jnp.abs(xsp)))
    beta_ref[0:seq, :] = jnp.broadcast_to(jax.nn.sigmoid(b_raw), (seq, LANES))
    g_ref[0:seq, :] = -jnp.exp(a_log) * softplus

    r = lax.broadcasted_iota(jnp.int32, (csz, csz), 0)
    c = lax.broadcasted_iota(jnp.int32, (csz, csz), 1)
    tril = r >= c
    strict = r > c
    diag = r == c
    tril_f = tril.astype(F32)
    ng = ng_ref[...]

    def chunk(ci, state):
        start = pl.multiple_of(ci * csz, csz)
        sl = pl.ds(start, csz)
        qc, kc, vc = qn_ref[sl, :], kn_ref[sl, :], vn_ref[sl, :]
        beta = beta_ref[sl, :]
        gcum = _mm(tril_f, g_ref[sl, :])
        gi = gcum[:, 0:csz]
        gj = jnp.sum(jnp.where(diag, gi, 0.0), axis=0, keepdims=True)
        decay = jnp.where(tril, jnp.exp(jnp.where(tril, gi - gj, 0.0)), 0.0)
        eg = jnp.exp(gcum)
        g_last = gcum[csz - 1:csz, :]
        kb = kc * beta
        lmat = jnp.where(strict, _mm_nt(kb, kc) * decay, 0.0)
        ainv = _unit_lower_inverse(lmat, csz)
        u = _mm(ainv, vc * beta)
        w = _mm(ainv, kb * eg)
        qk = _mm_nt(qc, kc) * decay
        v_new = u - _mm(w, state)
        o = _mm(qc * eg, state) + _mm(qk, v_new)
        k_dec = kc * jnp.exp(g_last - gcum)
        state = state * jnp.exp(g_last) + _mm_tn(k_dec, v_new)
        y = o * lax.rsqrt(jnp.mean(o * o, axis=-1, keepdims=True) + 1e-6) * ng
        oo_ref[sl, :] = y * _silu(zz_ref[sl, :])
        return state

    state = lax.fori_loop(0, seq_pad // csz, chunk, s0_ref[0, 0])
    sn_ref[0, 0] = state
    o_ref[0] = oo_ref[0:seq, :].astype(o_ref.dtype)


def _gated_deltanet(qkvb, ba, z, conv_buf, s0, conv_w, head_params, norm_g):
    nb, seq, _ = qkvb.shape
    csz = CHUNK
    seq_pad = -(-seq // csz) * csz
    assert seq_pad == seq or seq < csz
    col = lambda off: pl.BlockSpec((1, seq, LANES), lambda b, h: (b, 0, off + h))
    buf = lambda off: pl.BlockSpec((1, CONV_W - 1, LANES), lambda b, h: (b, 0, off + h))
    cw = lambda off: pl.BlockSpec((CONV_W, LANES), lambda b, h: (0, off + h))
    st = pl.BlockSpec((1, 1, DK_B, DV_B), lambda b, h: (b, h, 0, 0))
    scr = lambda n: pltpu.VMEM((n, LANES), F32)
    return pl.pallas_call(
        functools.partial(_gdn_kernel, seq=seq, seq_pad=seq_pad, csz=csz),
        grid=(nb, H_B),
        in_specs=[col(0), col(H_B), col(2 * H_B), buf(0), buf(H_B), buf(2 * H_B),
                  cw(0), cw(H_B), cw(2 * H_B),
                  pl.BlockSpec((1, seq, LANES), lambda b, h: (b, 0, 0)),
                  col(0), st,
                  pl.BlockSpec((1, 2, LANES), lambda b, h: (h, 0, 0)),
                  pl.BlockSpec((1, DV_B), lambda b, h: (0, 0))],
        out_specs=[col(0), st],
        out_shape=[jax.ShapeDtypeStruct((nb, seq, D_B), BF16),
                   jax.ShapeDtypeStruct((nb, H_B, DK_B, DV_B), F32)],
        scratch_shapes=[scr(seq + SUBLANES)] + [scr(seq_pad)] * 7,
        compiler_params=_cparams(("arbitrary", "arbitrary")),
        name="gated_deltanet",
    )(qkvb, qkvb, qkvb, conv_buf, conv_buf, conv_buf, conv_w, conv_w, conv_w,
      ba, z, s0, head_params, norm_g.reshape(1, DV_B))


def _outproj_kernel(oa_ref, ob_ref, x_ref, w_ref, g1_ref, lg_ref, lb_ref, sh_ref, sc_ref, *rest, tb, tl, alpha, router):
    if router:
        wr_ref, br_ref, x1_ref, h2_ref, gate_ref = rest
    else:
        x1_ref, h2_ref = rest
    rows = tb * tl
    oa = oa_ref[...].reshape(rows, D_A)
    ob = ob_ref[...].reshape(rows, D_B)
    mix = (jnp.dot(oa, w_ref[0:D_A, :], preferred_element_type=F32)
           + jnp.dot(ob, w_ref[D_A:D_A + D_B, :], preferred_element_type=F32))
    y = alpha * x_ref[...] + g1_ref[...] * mix.reshape(tb, tl, D_MODEL)
    x1 = _ln(y) * lg_ref[...] + lb_ref[...]
    x1_ref[...] = x1
    h2 = (_ln(x1) * (1.0 + sc_ref[...]) + sh_ref[...]).astype(BF16)
    h2_ref[...] = h2
    if router:
        logits = jnp.dot(h2.reshape(rows, D_MODEL), wr_ref[...], preferred_element_type=F32) + br_ref[...]
        lane = lax.broadcasted_iota(jnp.int32, (rows, LANES), 1)
        logits = jnp.where(lane < N_EXPERTS, logits, NEG_INF)
        v1 = jnp.max(logits, axis=-1, keepdims=True)
        i1 = jnp.min(jnp.where(logits == v1, lane, LANES), axis=-1, keepdims=True)
        rest_l = jnp.where(lane == i1, NEG_INF, logits)
        v2 = jnp.max(rest_l, axis=-1, keepdims=True)
        i2 = jnp.min(jnp.where(rest_l == v2, lane, LANES), axis=-1, keepdims=True)
        e2 = jnp.exp(v2 - v1)
        gate1 = 1.0 / (1.0 + e2)
        gate2 = e2 / (1.0 + e2)
        gates = jnp.where(lane == i1, gate1, 0.0) + jnp.where(lane == i2, gate2, 0.0)
        gate_ref[...] = gates.reshape(tb, tl, LANES)


def _out_proj(oa, ob, x, w_out_b, g1, ln_g, ln_b, sh2, sc2, alpha, tm, router_w=None, router_b=None):
    nb, seq, _ = x.shape
    tb, tl = _row_tile(nb, seq, tm)
    grid = (nb // tb, seq // tl)
    row = lambda w: pl.BlockSpec((tb, tl, w), lambda i, j: (i, j, 0))
    mod = pl.BlockSpec((tb, 1, D_MODEL), lambda i, j: (i, 0, 0))
    vec = pl.BlockSpec((1, 1, D_MODEL), lambda i, j: (0, 0, 0))
    router = router_w is not None
    in_specs = [row(D_A), row(D_B), row(D_MODEL), pl.BlockSpec((D_MODEL, D_MODEL), lambda i, j: (0, 0)),
                mod, vec, vec, mod, mod]
    args = [oa, ob, x, w_out_b, g1, ln_g.reshape(1, 1, D_MODEL), ln_b.reshape(1, 1, D_MODEL), sh2, sc2]
    out_specs = [row(D_MODEL), row(D_MODEL)]
    out_shape = [jax.ShapeDtypeStruct((nb, seq, D_MODEL), F32), jax.ShapeDtypeStruct((nb, seq, D_MODEL), BF16)]
    if router:
        in_specs += [pl.BlockSpec((D_MODEL, LANES), lambda i, j: (0, 0)), pl.BlockSpec((1, LANES), lambda i, j: (0, 0))]
        args += [router_w, router_b]
        out_specs.append(row(LANES))
        out_shape.append(jax.ShapeDtypeStruct((nb, seq, LANES), F32))
    return pl.pallas_call(
        functools.partial(_outproj_kernel, tb=tb, tl=tl, alpha=alpha, router=router),
        grid=grid, in_specs=in_specs, out_specs=out_specs, out_shape=out_shape,
        compiler_params=_cparams(("arbitrary", "arbitrary")),
        name="out_proj",
    )(*args)


def _ffn_kernel(h_ref, x_ref, wg_ref, wu_ref, wd_ref, g2_ref, lg_ref, lb_ref, *rest, tb, tl, alpha, gated):
    if gated:
        gate_ref, o_ref, acc_ref = rest
    else:
        o_ref, acc_ref = rest
    e = pl.program_id(2)
    f = pl.program_id(3)
    rows = tb * tl

    @pl.when((e == 0) & (f == 0))
    def _():
        acc_ref[...] = jnp.zeros_like(acc_ref)

    hb = h_ref[...].reshape(rows, D_MODEL)
    gt = jnp.dot(hb, wg_ref[0], preferred_element_type=F32)
    up = jnp.dot(hb, wu_ref[0], preferred_element_type=F32)
    a = (_silu(gt) * up).astype(BF16)
    part = jnp.dot(a, wd_ref[0], preferred_element_type=F32)
    if gated:
        gates = gate_ref[...].reshape(rows, LANES)
        lane = lax.broadcasted_iota(jnp.int32, (rows, LANES), 1)
        part = part * jnp.sum(jnp.where(lane == e, gates, 0.0), axis=-1, keepdims=True)
    acc_ref[...] += part

    @pl.when((e == pl.num_programs(2) - 1) & (f == pl.num_programs(3) - 1))
    def _():
        y = alpha * x_ref[...] + g2_ref[...] * acc_ref[...].reshape(tb, tl, D_MODEL)
        o_ref[...] = _ln(y) * lg_ref[...] + lb_ref[...]


def _ffn(h2, x1, w_up_b, w_down_b, g2, ln_g, ln_b, alpha, tm, tf, gates=None):
    nb, seq, _ = x1.shape
    ne = w_up_b.shape[0]
    tb, tl = _row_tile(nb, seq, tm)
    nf = D_FF // tf
    grid = (nb // tb, seq // tl, ne, nf)
    row = lambda w: pl.BlockSpec((tb, tl, w), lambda i, j, e, f: (i, j, 0))
    mod = pl.BlockSpec((tb, 1, D_MODEL), lambda i, j, e, f: (i, 0, 0))
    vec = pl.BlockSpec((1, 1, D_MODEL), lambda i, j, e, f: (0, 0, 0))
    in_specs = [row(D_MODEL), row(D_MODEL),
                pl.BlockSpec((1, D_MODEL, tf), lambda i, j, e, f: (e, 0, f)),
                pl.BlockSpec((1, D_MODEL, tf), lambda i, j, e, f: (e, 0, nf + f)),
                pl.BlockSpec((1, tf, D_MODEL), lambda i, j, e, f: (e, f, 0)),
                mod, vec, vec]
    args = [h2, x1, w_up_b, w_up_b, w_down_b, g2, ln_g.reshape(1, 1, D_MODEL), ln_b.reshape(1, 1, D_MODEL)]
    gated = gates is not None
    if gated:
        in_specs.append(row(LANES))
        args.append(gates)
    return pl.pallas_call(
        functools.partial(_ffn_kernel, tb=tb, tl=tl, alpha=alpha, gated=gated),
        grid=grid, in_specs=in_specs, out_specs=row(D_MODEL),
        out_shape=jax.ShapeDtypeStruct((nb, seq, D_MODEL), F32),
        scratch_shapes=[pltpu.VMEM((tb * tl, D_MODEL), F32)],
        compiler_params=_cparams(("arbitrary", "arbitrary", "arbitrary", "arbitrary")),
        name="ffn",
    )(*args)


def _trunk(x, mod, conv_bufs, delta_states, past, weights, tm):
    depth = weights["w_in"].shape[0]
    alpha = (2 * depth) ** 0.25
    nb, seq, _ = x.shape
    new_k, new_v, new_conv, new_delta = [], [], [], []
    slopes = jnp.broadcast_to(
        jnp.asarray([2.0 ** (-8.0 * (h + 1) / H_A) for h in range(H_A)], F32)[:, None, None], (H_A, 1, LANES))
    for l in range(depth):
        sh1, sc1, g1, sh2, sc2, g2 = [mod[l, s] for s in range(6)]
        lam_init = 0.8 - 0.6 * math.exp(-0.3 * l)
        q, k, v, qkvb, z, ba = _in_proj(x, sh1, sc1, weights["w_in"][l], tm)
        new_k.append(k.reshape(nb, seq, H_A, 2 * DQK_A))
        new_v.append(v.reshape(nb, seq, H_A, DV_A))
        if past is None:
            o_a = _attn_prompt(q, k, v, slopes, weights["lam_params"][l], weights["subln_a"][l], lam_init, 256)
        else:
            cache_k, cache_v, page_table = past
            o_a = _attn_decode(q, k, v, cache_k, cache_v, l, page_table,
                               weights["lam_params"][l], weights["subln_a"][l], lam_init)
        o_b, s_new = _gated_deltanet(qkvb, ba, z, conv_bufs[l], delta_states[l], weights["conv_w"][l],
                                     weights["head_params"][l], weights["norm_b"][l])
        new_conv.append(qkvb[:, seq - (CONV_W - 1):, :] if seq >= CONV_W - 1 else
                        jnp.concatenate([conv_bufs[l], qkvb], axis=1)[:, -(CONV_W - 1):, :])
        new_delta.append(s_new)
        if l % 2 == 0:
            x1, h2 = _out_proj(o_a, o_b, x, weights["w_out"][l], g1, weights["ln1_g"][l], weights["ln1_b"][l],
                               sh2, sc2, alpha, tm)
            j = l // 2
            x = _ffn(h2, x1, weights["dense_w_up"][j:j + 1], weights["dense_w_down"][j:j + 1], g2,
                     weights["ln2_g"][l], weights["ln2_b"][l], alpha, tm, D_FF // 2)
        else:
            j = l // 2
            x1, h2, gates = _out_proj(o_a, o_b, x, weights["w_out"][l], g1, weights["ln1_g"][l],
                                      weights["ln1_b"][l], sh2, sc2, alpha, tm,
                                      weights["router_w"][j], weights["router_b"][j])
            x = _ffn(h2, x1, weights["moe_w_up"][j], weights["moe_w_down"][j], g2,
                     weights["ln2_g"][l], weights["ln2_b"][l], alpha, tm, D_FF // 2, gates)
    return x, jnp.stack(new_k), jnp.stack(new_v), jnp.stack(new_conv), jnp.stack(new_delta)


def kernel(x_prompt, x_sample, cache_k, cache_v, state_conv, state_delta, page_table, c_prompt, c_sample,
           w_in, w_out, lam_params, subln_a, conv_w, a_log, dt_bias, norm_b, ada_w, ada_b,
           ln1_g, ln1_b, ln2_g, ln2_b, dense_w_up, dense_w_down, moe_router, moe_router_b, moe_w_up, moe_w_down):
    depth = w_in.shape[0]
    nbp, nbs = x_prompt.shape[0], x_sample.shape[0]
    a_q, a_k, a_v, b_qkv = 2 * H_A * DQK_A, 2 * H_A * DQK_A, H_A * DV_A, 3 * D_B
    n0 = a_q + a_k + a_v + b_qkv
    w_perm = jnp.concatenate(
        [w_in[:, :, :n0], w_in[:, :, n0 + 2 * H_B:], w_in[:, :, n0:n0 + 2 * H_B],
         jnp.zeros((depth, D_MODEL, LANES - 2 * H_B), w_in.dtype)], axis=-1).astype(BF16)
    n_moe = moe_router.shape[0]
    router_w = jnp.concatenate([moe_router, jnp.zeros((n_moe, D_MODEL, LANES - N_EXPERTS), moe_router.dtype)],
                               axis=-1).astype(BF16)
    router_b = jnp.concatenate([moe_router_b, jnp.zeros((n_moe, LANES - N_EXPERTS), moe_router_b.dtype)],
                               axis=-1).reshape(n_moe, 1, LANES)
    head_params = jnp.broadcast_to(jnp.stack([a_log, dt_bias], axis=-1)[..., None], (depth, H_B, 2, LANES))
    weights = dict(w_in=w_perm, w_out=w_out.astype(BF16), lam_params=lam_params, subln_a=subln_a, conv_w=conv_w,
                   head_params=head_params, norm_b=norm_b, ln1_g=ln1_g, ln1_b=ln1_b, ln2_g=ln2_g, ln2_b=ln2_b,
                   dense_w_up=dense_w_up.astype(BF16), dense_w_down=dense_w_down.astype(BF16),
                   router_w=router_w, router_b=router_b,
                   moe_w_up=moe_w_up.astype(BF16), moe_w_down=moe_w_down.astype(BF16))

    mod = _ada_mod(jnp.concatenate([c_prompt, c_sample], axis=0), ada_w, ada_b)
    mod = mod[:, :, :, None, :]
    conv0 = jnp.zeros((depth, nbp, CONV_W - 1, 3 * D_B), x_prompt.dtype)
    delta0 = jnp.zeros((depth, nbp, H_B, DK_B, DV_B), x_prompt.dtype)
    y_p, k_p, v_p, conv_p, delta_p = _trunk(x_prompt, mod[:, :, :nbp], conv0, delta0, None, weights, 512)

    n_pool = cache_k.shape[1]
    ck = cache_k.reshape(depth, n_pool, PAGE, D_A)
    cv = cache_v.reshape(depth, n_pool, PAGE, D_A)
    y_s, k_s, v_s, conv_s, delta_s = _trunk(x_sample, mod[:, :, nbp:], state_conv, state_delta,
                                            (ck, cv, page_table), weights, 512)
    return (y_p, y_s, k_p, v_p, conv_p, delta_p, k_s, v_s, conv_s, delta_s)
```

```python
import functools
import math

import jax
import jax.numpy as jnp
from jax import lax
from jax.experimental import pallas as pl
from jax.experimental.pallas import tpu as pltpu

F32 = jnp.float32
BF16 = jnp.bfloat16

D_MODEL = 1024
H_A = 4
DV_A = 128
DQK_A = 64
D_A = H_A * DV_A
H_B = 4
DK_B = 128
DV_B = 128
D_B = H_B * DV_B
CONV_W = 4
CHUNK = 64
D_FF = 2816
N_EXPERTS = 8
LN_EPS = 1e-5
PAGE = 128
LANES = 128
SUBLANES = 8
VMEM_LIMIT = 56 * 1024 * 1024
NEG_INF = float("-inf")
HIGHEST = lax.Precision.HIGHEST

IN_SEGS = (("q", 0, 512), ("k", 512, 1024), ("v", 1024, 1536), ("qkvb", 1536, 3072),
           ("z", 3072, 3584), ("ba", 3584, 3712))
D_IN_PAD = 3712


def _cparams(sem):
    return pltpu.CompilerParams(dimension_semantics=sem, vmem_limit_bytes=VMEM_LIMIT)


def _ln(x):
    mu = jnp.mean(x, axis=-1, keepdims=True)
    xc = x - mu
    var = jnp.mean(xc * xc, axis=-1, keepdims=True)
    return xc * lax.rsqrt(var + LN_EPS)


def _silu(x):
    return x * jax.nn.sigmoid(x)


def _row_tile(nb, seq, target):
    if seq >= target:
        assert seq % target == 0
        return 1, target
    tb = max(1, min(nb, target // seq))
    while nb % tb:
        tb -= 1
    return tb, seq


def _ada_kernel(c_ref, w_ref, b_ref, o_ref):
    a = _silu(c_ref[...]).astype(BF16)
    o_ref[0, 0] = jnp.dot(a, w_ref[0].astype(BF16), preferred_element_type=F32) + b_ref[0, 0]


def _ada_mod(c_all, ada_w, ada_b):
    depth = ada_w.shape[0]
    nb = c_all.shape[0]
    return pl.pallas_call(
        _ada_kernel,
        grid=(depth, 6),
        in_specs=[pl.BlockSpec((nb, D_MODEL), lambda l, s: (0, 0)),
                  pl.BlockSpec((1, D_MODEL, D_MODEL), lambda l, s: (l, 0, s)),
                  pl.BlockSpec((1, 1, 1, D_MODEL), lambda l, s: (l, s, 0, 0))],
        out_specs=pl.BlockSpec((1, 1, nb, D_MODEL), lambda l, s: (l, s, 0, 0)),
        out_shape=jax.ShapeDtypeStruct((depth, 6, nb, D_MODEL), F32),
        compiler_params=_cparams(("arbitrary", "arbitrary")),
        name="ada_mod",
    )(c_all, ada_w, ada_b.reshape(depth, 6, 1, D_MODEL))


def _inproj_kernel(x_ref, sh_ref, sc_ref, w_ref, q_ref, k_ref, v_ref, qkvb_ref, z_ref, ba_ref, *, tb, tl):
    h = _ln(x_ref[...]) * (1.0 + sc_ref[...]) + sh_ref[...]
    hb = h.reshape(tb * tl, D_MODEL).astype(BF16)
    outs = dict(q=q_ref, k=k_ref, v=v_ref, qkvb=qkvb_ref, z=z_ref, ba=ba_ref)
    for name, lo, hi in IN_SEGS:
        r = jnp.dot(hb, w_ref[:, lo:hi], preferred_element_type=F32)
        outs[name][...] = r.reshape(tb, tl, hi - lo).astype(outs[name].dtype)


def _in_proj(x, sh, sc, w_perm, tm):
    nb, seq, _ = x.shape
    tb, tl = _row_tile(nb, seq, tm)
    grid = (nb // tb, seq // tl)
    row = lambda w: pl.BlockSpec((tb, tl, w), lambda i, j: (i, j, 0))
    mod = pl.BlockSpec((tb, 1, D_MODEL), lambda i, j: (i, 0, 0))
    widths = [hi - lo for _, lo, hi in IN_SEGS]
    dtypes = [BF16, F32, F32, F32, F32, F32]
    return pl.pallas_call(
        functools.partial(_inproj_kernel, tb=tb, tl=tl),
        grid=grid,
        in_specs=[row(D_MODEL), mod, mod, pl.BlockSpec((D_MODEL, D_IN_PAD), lambda i, j: (0, 0))],
        out_specs=[row(w) for w in widths],
        out_shape=[jax.ShapeDtypeStruct((nb, seq, w), dt) for w, dt in zip(widths, dtypes)],
        compiler_params=_cparams(("arbitrary", "arbitrary")),
        name="in_proj",
    )(x, sh, sc, w_perm)


def _lam_value(lam_ref, lam_init):
    lp = lam_ref[...]
    a = jnp.sum(lp[0:1] * lp[1:2], axis=-1, keepdims=True)
    b = jnp.sum(lp[2:3] * lp[3:4], axis=-1, keepdims=True)
    return jnp.exp(a) - jnp.exp(b) + lam_init


def _sub_ln(o, g, lam_init):
    y = o * lax.rsqrt(jnp.mean(o * o, axis=-1, keepdims=True) + 1e-6)
    return (y * g) * (1.0 - lam_init)


ATT_ROWS = 128


def _attn_kernel(q_ref, k_ref, v_ref, sl_ref, lam_ref, g_ref, o_ref, kb_ref, vb_ref, q2_ref, *stats, tq, lam_init):
    ngrp = 2 * tq // ATT_ROWS
    m_refs, l_refs, acc_refs = stats[0:ngrp], stats[ngrp:2 * ngrp], stats[2 * ngrp:3 * ngrp]
    qi = pl.program_id(2)
    nrep = tq // LANES

    @pl.when(qi == 0)
    def _():
        kb_ref[...] = k_ref[0].astype(BF16)
        vb_ref[...] = v_ref[0].astype(BF16)

    slope = jnp.concatenate([sl_ref[0]] * nrep, axis=1)
    qs = (q_ref[0].astype(F32) * (DQK_A ** -0.5)).astype(BF16)
    lane = lax.broadcasted_iota(jnp.int32, (tq, 2 * DQK_A), 1)
    zero = jnp.zeros_like(qs)
    q2_ref[0:tq, :] = jnp.where(lane < DQK_A, qs, zero)
    q2_ref[tq:2 * tq, :] = jnp.where(lane >= DQK_A, qs, zero)
    col = lax.broadcasted_iota(jnp.int32, (1, tq), 1)
    for g in range(ngrp):
        m_refs[g][...] = jnp.full_like(m_refs[g], NEG_INF)
        l_refs[g][...] = jnp.zeros_like(l_refs[g])
        acc_refs[g][...] = jnp.zeros_like(acc_refs[g])

    def step(j, masked):
        start = pl.multiple_of(j * tq, tq)
        kj = kb_ref[pl.ds(start, tq), :]
        vj = vb_ref[pl.ds(start, tq), :]
        bias = slope * (col + (j - qi) * tq).astype(F32)
        scores = [lax.dot_general(q2_ref[g * ATT_ROWS:(g + 1) * ATT_ROWS, :], kj, (((1,), (1,)), ((), ())),
                                  preferred_element_type=F32) for g in range(ngrp)]
        for g in range(ngrp):
            s = scores[g] + bias
            if masked:
                rq = (g * ATT_ROWS) % tq + lax.broadcasted_iota(jnp.int32, (ATT_ROWS, tq), 0)
                s = jnp.where(rq >= col, s, NEG_INF)
            m_prev = m_refs[g][...]
            m_new = jnp.maximum(m_prev, jnp.max(s, axis=-1, keepdims=True))
            a = jnp.exp(m_prev - m_new)
            p = jnp.exp(s - jnp.concatenate([m_new] * nrep, axis=1))
            l_refs[g][...] = a * l_refs[g][...] + jnp.sum(p, axis=-1, keepdims=True)
            acc_refs[g][...] = a * acc_refs[g][...] + jnp.dot(p.astype(BF16), vj, preferred_element_type=F32)
            m_refs[g][...] = m_new

    def body(j, carry):
        step(j, False)
        return carry

    lax.fori_loop(0, qi, body, 0)
    step(qi, True)

    lam = _lam_value(lam_ref, lam_init)
    half = ngrp // 2
    for g in range(half):
        o1 = acc_refs[g][...] / l_refs[g][...]
        o2 = acc_refs[half + g][...] / l_refs[half + g][...]
        y = _sub_ln(o1 - lam * o2, g_ref[...], lam_init)
        o_ref[0, g * ATT_ROWS:(g + 1) * ATT_ROWS, :] = y.astype(o_ref.dtype)


def _attn_prompt(q, k, v, slopes, lam_p, subln_g, lam_init, tq):
    nb, seq, _ = q.shape
    tq = min(tq, seq)
    assert seq % tq == 0 and tq % LANES == 0 and tq % ATT_ROWS == 0 and DV_A == LANES
    grid = (nb, H_A, seq // tq)
    return pl.pallas_call(
        functools.partial(_attn_kernel, tq=tq, lam_init=lam_init),
        grid=grid,
        in_specs=[pl.BlockSpec((1, tq, DV_A), lambda b, h, i: (b, i, h)),
                  pl.BlockSpec((1, seq, DV_A), lambda b, h, i: (b, 0, h)),
                  pl.BlockSpec((1, seq, DV_A), lambda b, h, i: (b, 0, h)),
                  pl.BlockSpec((1, 1, LANES), lambda b, h, i: (h, 0, 0)),
                  pl.BlockSpec((4, DQK_A), lambda b, h, i: (0, 0)),
                  pl.BlockSpec((1, DV_A), lambda b, h, i: (0, 0))],
        out_specs=pl.BlockSpec((1, tq, DV_A), lambda b, h, i: (b, i, h)),
        out_shape=jax.ShapeDtypeStruct((nb, seq, D_A), BF16),
        scratch_shapes=[pltpu.VMEM((seq, DV_A), BF16), pltpu.VMEM((seq, DV_A), BF16),
                        pltpu.VMEM((2 * tq, DV_A), BF16)]
                       + [pltpu.VMEM((ATT_ROWS, LANES), F32)] * (3 * (2 * tq // ATT_ROWS)),
        compiler_params=_cparams(("arbitrary", "arbitrary", "arbitrary")),
        name="attn_prompt",
    )(q, k, v, slopes, lam_p, subln_g.reshape(1, DV_A))


PAGES_PER_STEP = 8
PAGE_ROWS = PAGE * H_A
HEAD_SHIFT = 2


def _dg_nt(a, b):
    return lax.dot_general(a, b, (((1,), (1,)), ((), ())), preferred_element_type=F32)


def _decode_kernel(pt_ref, q_ref, kn_ref, vn_ref, lam_ref, g_ref, *rest, nq, past_len, lam_init, ngroups):
    kp_refs = rest[:PAGES_PER_STEP]
    vp_refs = rest[PAGES_PER_STEP:2 * PAGES_PER_STEP]
    o_ref = rest[2 * PAGES_PER_STEP]
    qr_ref, m_ref, l_ref, acc_ref, pad_k_ref, pad_v_ref = rest[2 * PAGES_PER_STEP + 1:]
    g = pl.program_id(1)
    rows = 2 * H_A * nq

    row = lax.broadcasted_iota(jnp.int32, (rows, 1), 0)
    hm = jnp.zeros_like(row)
    for t in range(1, 2 * H_A):
        hm = hm + (row >= t * nq).astype(jnp.int32)
    qpos = past_len + (row - nq * hm)
    rhead = jnp.zeros_like(row)
    for h in range(1, H_A):
        rhead = rhead + (row >= 2 * h * nq).astype(jnp.int32)
    slope = jnp.zeros((rows, 1), F32)
    for h in range(H_A):
        slope = jnp.where(rhead == h, 2.0 ** (-8.0 * (h + 1) / H_A), slope)

    @pl.when(g == 0)
    def _():
        lane = lax.broadcasted_iota(jnp.int32, (nq, 2 * DQK_A), 1)
        for h in range(H_A):
            qh = (q_ref[0, :, h * DV_A:(h + 1) * DV_A].astype(F32) * (DQK_A ** -0.5)).astype(BF16)
            zero = jnp.zeros_like(qh)
            qr_ref[(2 * h) * nq:(2 * h + 1) * nq, :] = jnp.where(lane < DQK_A, qh, zero)
            qr_ref[(2 * h + 1) * nq:(2 * h + 2) * nq, :] = jnp.where(lane >= DQK_A, qh, zero)
        m_ref[...] = jnp.full_like(m_ref, NEG_INF)
        l_ref[...] = jnp.zeros_like(l_ref)
        acc_ref[...] = jnp.zeros_like(acc_ref)

    qr = qr_ref[...]

    def update(kbs, vbs, base_pos, causal):
        n = len(kbs)
        ncol = n * PAGE_ROWS
        s = jnp.concatenate([_dg_nt(qr, kb) for kb in kbs], axis=1)
        col = lax.broadcasted_iota(jnp.int32, (1, ncol), 1)
        ctok = col >> HEAD_SHIFT
        kpos = base_pos + ctok
        valid = (col - (ctok << HEAD_SHIFT)) == rhead
        if causal:
            valid = valid & (kpos <= qpos) & (ctok < nq)
        s = jnp.where(valid, s - slope * (qpos - kpos).astype(F32), NEG_INF)
        m_prev = m_ref[...]
        m_new = jnp.maximum(m_prev, jnp.max(s, axis=-1, keepdims=True))
        a = jnp.exp(m_prev - m_new)
        p = jnp.exp(s - jnp.concatenate([m_new] * (ncol // LANES), axis=1)).astype(BF16)
        l_ref[...] = a * l_ref[...] + jnp.sum(p.astype(F32), axis=-1, keepdims=True)
        pv = jnp.dot(p[:, 0:PAGE_ROWS], vbs[0], preferred_element_type=F32)
        for i in range(1, n):
            pv = pv + jnp.dot(p[:, i * PAGE_ROWS:(i + 1) * PAGE_ROWS], vbs[i], preferred_element_type=F32)
        acc_ref[...] = a * acc_ref[...] + pv
        m_ref[...] = m_new

    update([r[...].astype(BF16) for r in kp_refs], [r[...].astype(BF16) for r in vp_refs],
           g * (PAGES_PER_STEP * PAGE), False)

    @pl.when(g == ngroups - 1)
    def _():
        pad_k_ref[...] = jnp.zeros_like(pad_k_ref)
        pad_v_ref[...] = jnp.zeros_like(pad_v_ref)
        pad_k_ref[0:nq * H_A, :] = kn_ref[0]
        pad_v_ref[0:nq * H_A, :] = vn_ref[0]
        update([pad_k_ref[...].astype(BF16)], [pad_v_ref[...].astype(BF16)], past_len, True)
        o = acc_ref[...] / l_ref[...]
        lam = _lam_value(lam_ref, lam_init)
        for h in range(H_A):
            o1 = o[(2 * h) * nq:(2 * h + 1) * nq, :]
            o2 = o[(2 * h + 1) * nq:(2 * h + 2) * nq, :]
            y = _sub_ln(o1 - lam * o2, g_ref[...], lam_init)
            o_ref[0, :, h * DV_A:(h + 1) * DV_A] = y.astype(o_ref.dtype)


def _attn_decode(q, k_new, v_new, cache_k, cache_v, layer, pool, page_table, lam_p, subln_g, lam_init):
    nb, nq, _ = q.shape
    n_pages = page_table.shape[1]
    assert n_pages % PAGES_PER_STEP == 0 and nq * H_A <= PAGE_ROWS and H_A == 1 << HEAD_SHIFT
    ngroups = n_pages // PAGES_PER_STEP
    past_len = n_pages * PAGE
    rows = 2 * H_A * nq

    def page_spec(i):
        return pl.BlockSpec((PAGE_ROWS, DV_A),
                            lambda b, g, pt: (layer * pool + pt[b, g * PAGES_PER_STEP + i], 0))

    per_b = lambda r, w: pl.BlockSpec((1, r, w), lambda b, g, pt: (b, 0, 0))
    grid_spec = pltpu.PrefetchScalarGridSpec(
        num_scalar_prefetch=1,
        grid=(nb, ngroups),
        in_specs=[per_b(nq, D_A), per_b(nq * H_A, DV_A), per_b(nq * H_A, DV_A),
                  pl.BlockSpec((4, DQK_A), lambda b, g, pt: (0, 0)),
                  pl.BlockSpec((1, DV_A), lambda b, g, pt: (0, 0))]
                 + [page_spec(i) for i in range(PAGES_PER_STEP)]
                 + [page_spec(i) for i in range(PAGES_PER_STEP)],
        out_specs=per_b(nq, D_A),
        scratch_shapes=[pltpu.VMEM((rows, DV_A), BF16), pltpu.VMEM((rows, LANES), F32), pltpu.VMEM((rows, LANES), F32),
                        pltpu.VMEM((rows, DV_A), F32), pltpu.VMEM((PAGE_ROWS, DV_A), F32),
                        pltpu.VMEM((PAGE_ROWS, DV_A), F32)],
    )
    return pl.pallas_call(
        functools.partial(_decode_kernel, nq=nq, past_len=past_len, lam_init=lam_init, ngroups=ngroups),
        grid_spec=grid_spec,
        out_shape=jax.ShapeDtypeStruct((nb, nq, D_A), BF16),
        compiler_params=_cparams(("arbitrary", "arbitrary")),
        name="attn_decode",
    )(page_table, q, k_new.reshape(nb, nq * H_A, DV_A), v_new.reshape(nb, nq * H_A, DV_A), lam_p,
      subln_g.reshape(1, DV_A), *([cache_k] * PAGES_PER_STEP), *([cache_v] * PAGES_PER_STEP))


def _mm(a, b):
    return jnp.dot(a, b, preferred_element_type=F32, precision=HIGHEST)


def _mm_nt(a, b):
    return lax.dot_general(a, b, (((1,), (1,)), ((), ())), preferred_element_type=F32, precision=HIGHEST)


def _mm_tn(a, b):
    return lax.dot_general(a, b, (((0,), (0,)), ((), ())), preferred_element_type=F32, precision=HIGHEST)


def _unit_lower_inverse(lmat, csz):
    r = lax.broadcasted_iota(jnp.int32, (csz, csz), 0)
    c = lax.broadcasted_iota(jnp.int32, (csz, csz), 1)
    eye = (r == c).astype(F32)
    same_blk = (r // SUBLANES) == (c // SUBLANES)
    ld = jnp.where(same_blk, lmat, 0.0)
    dinv = eye - ld
    pw = ld
    for _ in range(2):
        pw = _mm(pw, pw)
        dinv = _mm(dinv, eye + pw)
    nblk = csz // SUBLANES
    if nblk == 1:
        return dinv
    n = _mm(dinv, lmat - ld)
    out = eye - n
    pw = n
    k = 2
    while k < nblk:
        pw = _mm(pw, pw)
        out = _mm(out, eye + pw)
        k *= 2
    return _mm(out, dinv)


def _gdn_kernel(xq_ref, xk_ref, xv_ref, bq_ref, bk_ref, bv_ref, wq_ref, wk_ref, wv_ref,
                ba_ref, z_ref, s0_ref, hp_ref, ng_ref, o_ref, sn_ref,
                xx_ref, qn_ref, kn_ref, vn_ref, beta_ref, g_ref, zz_ref, oo_ref, *, seq, seq_pad, csz):
    h = pl.program_id(1)

    def conv(x_ref, b_ref, w_ref):
        xx_ref[0:SUBLANES, :] = jnp.zeros((SUBLANES, LANES), F32)
        xx_ref[SUBLANES - (CONV_W - 1):SUBLANES, :] = b_ref[0]
        xx_ref[SUBLANES:SUBLANES + seq, :] = x_ref[0]
        w = w_ref[...]
        y = w[CONV_W - 1:CONV_W] * xx_ref[SUBLANES:SUBLANES + seq, :]
        for i in range(CONV_W - 1):
            off = SUBLANES - (CONV_W - 1) + i
            y = y + w[i:i + 1] * xx_ref[off:off + seq, :]
        return _silu(y)

    def l2n(t):
        return t * lax.rsqrt(jnp.sum(t * t, axis=-1, keepdims=True) + 1e-6)

    if seq_pad > seq:
        for ref in (qn_ref, kn_ref, vn_ref, beta_ref, g_ref, zz_ref):
            ref[...] = jnp.zeros_like(ref)
    qn_ref[0:seq, :] = l2n(conv(xq_ref, bq_ref, wq_ref)) * (DK_B ** -0.5)
    kn_ref[0:seq, :] = l2n(conv(xk_ref, bk_ref, wk_ref))
    vn_ref[0:seq, :] = conv(xv_ref, bv_ref, wv_ref)
    zz_ref[0:seq, :] = z_ref[0]

    ba = ba_ref[0]
    lane = lax.broadcasted_iota(jnp.int32, ba.shape, 1)
    b_raw = jnp.sum(jnp.where(lane == h, ba, 0.0), axis=-1, keepdims=True)
    a_raw = jnp.sum(jnp.where(lane == h + H_B, ba, 0.0), axis=-1, keepdims=True)
    a_log = hp_ref[0, 0:1, :]
    dt_bias = hp_ref[0, 1:2, :]
    xsp = a_raw + dt_bias
    softplus = jnp.maximum(xsp, 0.0) + jnp.log1p(jnp.exp(-jnp.abs(xsp)))
    beta_ref[0:seq, :] = jnp.broadcast_to(jax.nn.sigmoid(b_raw), (seq, LANES))
    g_ref[0:seq, :] = -jnp.exp(a_log) * softplus

    r = lax.broadcasted_iota(jnp.int32, (csz, csz), 0)
    c = lax.broadcasted_iota(jnp.int32, (csz, csz), 1)
    tril = r >= c
    strict = r > c
    diag = r == c
    tril_f = tril.astype(F32)
    ng = ng_ref[...]

    def chunk(ci, state):
        start = pl.multiple_of(ci * csz, csz)
        sl = pl.ds(start, csz)
        qc, kc, vc = qn_ref[sl, :], kn_ref[sl, :], vn_ref[sl, :]
        beta = beta_ref[sl, :]
        gcum = _mm(tril_f, g_ref[sl, :])
        gi = gcum[:, 0:csz]
        gj = jnp.sum(jnp.where(diag, gi, 0.0), axis=0, keepdims=True)
        decay = jnp.where(tril, jnp.exp(jnp.where(tril, gi - gj, 0.0)), 0.0)
        eg = jnp.exp(gcum)
        g_last = gcum[csz - 1:csz, :]
        kb = kc * beta
        lmat = jnp.where(strict, _mm_nt(kb, kc) * decay, 0.0)
        ainv = _unit_lower_inverse(lmat, csz)
        u = _mm(ainv, vc * beta)
        w = _mm(ainv, kb * eg)
        qk = _mm_nt(qc, kc) * decay
        v_new = u - _mm(w, state)
        o = _mm(qc * eg, state) + _mm(qk, v_new)
        k_dec = kc * jnp.exp(g_last - gcum)
        state = state * jnp.exp(g_last) + _mm_tn(k_dec, v_new)
        y = o * lax.rsqrt(jnp.mean(o * o, axis=-1, keepdims=True) + 1e-6) * ng
        oo_ref[sl, :] = y * _silu(zz_ref[sl, :])
        return state

    state = lax.fori_loop(0, seq_pad // csz, chunk, s0_ref[0, 0])
    sn_ref[0, 0] = state
    o_ref[0] = oo_ref[0:seq, :].astype(o_ref.dtype)


def _gated_deltanet(qkvb, ba, z, conv_buf, s0, conv_w, head_params, norm_g):
    nb, seq, _ = qkvb.shape
    csz = CHUNK
    seq_pad = -(-seq // csz) * csz
    assert seq_pad == seq or seq < csz
    col = lambda off: pl.BlockSpec((1, seq, LANES), lambda b, h: (b, 0, off + h))
    buf = lambda off: pl.BlockSpec((1, CONV_W - 1, LANES), lambda b, h: (b, 0, off + h))
    cw = lambda off: pl.BlockSpec((CONV_W, LANES), lambda b, h: (0, off + h))
    st = pl.BlockSpec((1, 1, DK_B, DV_B), lambda b, h: (b, h, 0, 0))
    scr = lambda n: pltpu.VMEM((n, LANES), F32)
    return pl.pallas_call(
        functools.partial(_gdn_kernel, seq=seq, seq_pad=seq_pad, csz=csz),
        grid=(nb, H_B),
        in_specs=[col(0), col(H_B), col(2 * H_B), buf(0), buf(H_B), buf(2 * H_B),
                  cw(0), cw(H_B), cw(2 * H_B),
                  pl.BlockSpec((1, seq, LANES), lambda b, h: (b, 0, 0)),
                  col(0), st,
                  pl.BlockSpec((1, 2, LANES), lambda b, h: (h, 0, 0)),
                  pl.BlockSpec((1, DV_B), lambda b, h: (0, 0))],
        out_specs=[col(0), st],
        out_shape=[jax.ShapeDtypeStruct((nb, seq, D_B), BF16),
                   jax.ShapeDtypeStruct((nb, H_B, DK_B, DV_B), F32)],
        scratch_shapes=[scr(seq + SUBLANES)] + [scr(seq_pad)] * 7,
        compiler_params=_cparams(("arbitrary", "arbitrary")),
        name="gated_deltanet",
    )(qkvb, qkvb, qkvb, conv_buf, conv_buf, conv_buf, conv_w, conv_w, conv_w,
      ba, z, s0, head_params, norm_g.reshape(1, DV_B))


GDN_TILE = 512
GDN_SUPER = 256


def _split(a):
    hi = a.astype(BF16)
    return hi, (a - hi.astype(F32)).astype(BF16)


def _dg(a, b, dims):
    return lax.dot_general(a, b, (dims, ((), ())), preferred_element_type=F32)


_NN = ((1,), (0,))
_NT = ((1,), (1,))
_TN = ((0,), (0,))


def _mm3s(asp, bsp, dims=_NN):
    (ah, al), (bh, bl) = asp, bsp
    return _dg(ah, bh, dims) + (_dg(ah, bl, dims) + _dg(al, bh, dims))


def _mm3(a, b, dims=_NN):
    return _mm3s(_split(a), _split(b), dims)


def _mm_exact_lhs(t, b):
    b1 = b.astype(BF16)
    r1 = b - b1.astype(F32)
    b2 = r1.astype(BF16)
    b3 = (r1 - b2.astype(F32)).astype(BF16)
    return _dg(t, b1, _NN) + (_dg(t, b2, _NN) + _dg(t, b3, _NN))


def _unit_lower_inverse3(lmat, size, csz):
    r = lax.broadcasted_iota(jnp.int32, (size, size), 0)
    c = lax.broadcasted_iota(jnp.int32, (size, size), 1)
    eye = (r == c).astype(F32)
    same_blk = (r // SUBLANES) == (c // SUBLANES)
    ld = jnp.where(same_blk, lmat, 0.0)
    dinv = eye - ld
    pw = ld
    for _ in range(2):
        sp = _split(pw)
        pw = _mm3s(sp, sp)
        dinv = _mm3(dinv, eye + pw)
    dsp = _split(dinv)
    n = _mm3s(dsp, _split(lmat - ld))
    out = eye - n
    pw = n
    k = 2
    while k < csz // SUBLANES:
        sp = _split(pw)
        pw = _mm3s(sp, sp)
        out = _mm3(out, eye + pw)
        k *= 2
    return _mm3s(_split(out), dsp)


def _gdn_long_kernel(x_ref, cb_ref, cw_ref, ba_ref, z_ref, s0_ref, hp_ref, ng_ref, o_ref, sn_ref,
                     xx_ref, y_ref, st_ref, u_ref, w_ref, qd_ref, kd_ref, gt_ref, qk_ref, *, tl):
    t = pl.program_id(1)
    width = 3 * D_B
    hist = SUBLANES

    @pl.when(t == 0)
    def _():
        xx_ref[0:hist, :] = jnp.zeros((hist, width), F32)
        xx_ref[hist - (CONV_W - 1):hist, :] = cb_ref[0]
        st_ref[...] = s0_ref[0]

    @pl.when(t > 0)
    def _():
        xx_ref[0:hist, :] = xx_ref[tl:tl + hist, :]

    xx_ref[hist:hist + tl, :] = x_ref[0]

    cw = cw_ref[...]
    blk = 256
    for r0 in range(0, tl, blk):
        y = cw[CONV_W - 1:CONV_W] * xx_ref[hist + r0:hist + r0 + blk, :]
        for i in range(CONV_W - 1):
            off = hist - (CONV_W - 1) + i + r0
            y = y + cw[i:i + 1] * xx_ref[off:off + blk, :]
        y_ref[r0:r0 + blk, :] = _silu(y)

    sup = GDN_SUPER
    r = lax.broadcasted_iota(jnp.int32, (sup, sup), 0)
    c = lax.broadcasted_iota(jnp.int32, (sup, sup), 1)
    same_chunk = (r // CHUNK) == (c // CHUNK)
    tril = same_chunk & (r >= c)
    strict = same_chunk & (r > c)
    diag = r == c
    sum_mat = jnp.concatenate([tril.astype(BF16), same_chunk.astype(BF16)], axis=0)

    def l2n(v):
        return v * lax.rsqrt(jnp.sum(v * v, axis=-1, keepdims=True) + 1e-6)

    for sc in range(tl // sup):
        rs = slice(sc * sup, (sc + 1) * sup)
        ba = ba_ref[0, rs, :]
        for h in range(H_B):
            hc = slice(h * DK_B, (h + 1) * DK_B)
            q = l2n(y_ref[rs, h * DK_B:(h + 1) * DK_B]) * (DK_B ** -0.5)
            k = l2n(y_ref[rs, D_B + h * DK_B:D_B + (h + 1) * DK_B])
            v = y_ref[rs, 2 * D_B + h * DV_B:2 * D_B + (h + 1) * DV_B]
            beta = jax.nn.sigmoid(ba[:, h:h + 1])
            xsp = ba[:, H_B + h:H_B + h + 1] + hp_ref[h, 1:2, :]
            gl = -jnp.exp(hp_ref[h, 0:1, :]) * (jnp.maximum(xsp, 0.0) + jnp.log1p(jnp.exp(-jnp.abs(xsp))))
            sums = _mm_exact_lhs(sum_mat, gl)
            gcum, gtot = sums[0:sup], sums[sup:2 * sup]
            gi = jnp.concatenate([gcum] * (sup // LANES), axis=1)
            gj = jnp.sum(jnp.where(diag, gi, 0.0), axis=0, keepdims=True)
            decay = jnp.where(tril, jnp.exp(jnp.where(tril, gi - gj, 0.0)), 0.0)
            eg = jnp.exp(gcum)
            kb = k * beta
            kbf = k.astype(BF16)
            lmat = jnp.where(strict, _dg(kb.astype(BF16), kbf, _NT) * decay, 0.0)
            ainv = _unit_lower_inverse3(lmat, sup, CHUNK)
            uw = _mm3(ainv, jnp.concatenate([v * beta, kb * eg], axis=1))
            u_ref[rs, hc] = uw[:, 0:DV_B]
            w_ref[rs, hc] = uw[:, DV_B:DV_B + DK_B]
            qk = _dg(q.astype(BF16), kbf, _NT) * decay
            qkc = qk[:, 0:CHUNK]
            for i in range(1, sup // CHUNK):
                qkc = qkc + qk[:, i * CHUNK:(i + 1) * CHUNK]
            qk_ref[h, rs, :] = qkc
            qd_ref[rs, hc] = q * eg
            kd_ref[rs, hc] = k * jnp.exp(gtot - gcum)
            gt_ref[rs, hc] = jnp.exp(gtot)

    ng = ng_ref[...]

    def chunk(ci, carry):
        start = pl.multiple_of(ci * CHUNK, CHUNK)
        rows = pl.ds(start, CHUNK)
        for h in range(H_B):
            hc = slice(h * DK_B, (h + 1) * DK_B)
            ssp = _split(st_ref[h])
            v_new = u_ref[rows, hc] - _mm3s(_split(w_ref[rows, hc]), ssp)
            vsp = _split(v_new)
            o = _mm3s(_split(qd_ref[rows, hc]), ssp) + _dg(qk_ref[h, rows, :].astype(BF16), vsp[0], _NN)
            st_ref[h] = st_ref[h] * gt_ref[pl.ds(start, 1), hc] + _mm3s(_split(kd_ref[rows, hc]), vsp, _TN)
            y = o * lax.rsqrt(jnp.mean(o * o, axis=-1, keepdims=True) + 1e-6) * ng
            o_ref[0, rows, hc] = (y * _silu(z_ref[0, rows, hc])).astype(o_ref.dtype)
        return carry

    lax.fori_loop(0, tl // CHUNK, chunk, 0)
    sn_ref[0] = st_ref[...]


def _gated_deltanet_long(qkvb, ba, z, conv_buf, s0, conv_w, head_params, norm_g):
    nb, seq, width = qkvb.shape
    tl = GDN_TILE
    assert seq % tl == 0 and tl % GDN_SUPER == 0
    row = lambda w: pl.BlockSpec((1, tl, w), lambda b, t: (b, t, 0))
    st = pl.BlockSpec((1, H_B, DK_B, DV_B), lambda b, t: (b, 0, 0, 0))
    scr = lambda n, w: pltpu.VMEM((n, w), F32)
    return pl.pallas_call(
        functools.partial(_gdn_long_kernel, tl=tl),
        grid=(nb, seq // tl),
        in_specs=[row(width),
                  pl.BlockSpec((1, CONV_W - 1, width), lambda b, t: (b, 0, 0)),
                  pl.BlockSpec((CONV_W, width), lambda b, t: (0, 0)),
                  row(LANES), row(D_B), st,
                  pl.BlockSpec((H_B, 2, LANES), lambda b, t: (0, 0, 0)),
                  pl.BlockSpec((1, DV_B), lambda b, t: (0, 0))],
        out_specs=[row(D_B), st],
        out_shape=[jax.ShapeDtypeStruct((nb, seq, D_B), BF16),
                   jax.ShapeDtypeStruct((nb, H_B, DK_B, DV_B), F32)],
        scratch_shapes=[scr(tl + SUBLANES, width), scr(tl, width), pltpu.VMEM((H_B, DK_B, DV_B), F32),
                        scr(tl, D_B), scr(tl, D_B), scr(tl, D_B), scr(tl, D_B), scr(tl, D_B),
                        pltpu.VMEM((H_B, tl, CHUNK), F32)],
        compiler_params=_cparams(("arbitrary", "arbitrary")),
        name="gated_deltanet_long",
    )(qkvb, conv_buf, conv_w, ba, z, s0, head_params, norm_g.reshape(1, DV_B))


def _outproj_kernel(oa_ref, ob_ref, x_ref, w_ref, g1_ref, lg_ref, lb_ref, sh_ref, sc_ref, *rest, tb, tl, alpha, router):
    if router:
        wr_ref, br_ref, x1_ref, h2_ref, gate_ref = rest
    else:
        x1_ref, h2_ref = rest
    rows = tb * tl
    oa = oa_ref[...].reshape(rows, D_A)
    ob = ob_ref[...].reshape(rows, D_B)
    mix = (jnp.dot(oa, w_ref[0:D_A, :], preferred_element_type=F32)
           + jnp.dot(ob, w_ref[D_A:D_A + D_B, :], preferred_element_type=F32))
    y = alpha * x_ref[...] + g1_ref[...] * mix.reshape(tb, tl, D_MODEL)
    x1 = _ln(y) * lg_ref[...] + lb_ref[...]
    x1_ref[...] = x1
    h2 = (_ln(x1) * (1.0 + sc_ref[...]) + sh_ref[...]).astype(BF16)
    h2_ref[...] = h2
    if router:
        logits = jnp.dot(h2.reshape(rows, D_MODEL), wr_ref[...], preferred_element_type=F32) + br_ref[...]
        lane = lax.broadcasted_iota(jnp.int32, (rows, LANES), 1)
        logits = jnp.where(lane < N_EXPERTS, logits, NEG_INF)
        v1 = jnp.max(logits, axis=-1, keepdims=True)
        i1 = jnp.min(jnp.where(logits == v1, lane, LANES), axis=-1, keepdims=True)
        rest_l = jnp.where(lane == i1, NEG_INF, logits)
        v2 = jnp.max(rest_l, axis=-1, keepdims=True)
        i2 = jnp.min(jnp.where(rest_l == v2, lane, LANES), axis=-1, keepdims=True)
        e2 = jnp.exp(v2 - v1)
        gate1 = 1.0 / (1.0 + e2)
        gate2 = e2 / (1.0 + e2)
        gates = jnp.where(lane == i1, gate1, 0.0) + jnp.where(lane == i2, gate2, 0.0)
        gate_ref[...] = gates.reshape(tb, tl, LANES)


def _out_proj(oa, ob, x, w_out_b, g1, ln_g, ln_b, sh2, sc2, alpha, tm, router_w=None, router_b=None):
    nb, seq, _ = x.shape
    tb, tl = _row_tile(nb, seq, tm)
    grid = (nb // tb, seq // tl)
    row = lambda w: pl.BlockSpec((tb, tl, w), lambda i, j: (i, j, 0))
    mod = pl.BlockSpec((tb, 1, D_MODEL), lambda i, j: (i, 0, 0))
    vec = pl.BlockSpec((1, 1, D_MODEL), lambda i, j: (0, 0, 0))
    router = router_w is not None
    in_specs = [row(D_A), row(D_B), row(D_MODEL), pl.BlockSpec((D_MODEL, D_MODEL), lambda i, j: (0, 0)),
                mod, vec, vec, mod, mod]
    args = [oa, ob, x, w_out_b, g1, ln_g.reshape(1, 1, D_MODEL), ln_b.reshape(1, 1, D_MODEL), sh2, sc2]
    out_specs = [row(D_MODEL), row(D_MODEL)]
    out_shape = [jax.ShapeDtypeStruct((nb, seq, D_MODEL), F32), jax.ShapeDtypeStruct((nb, seq, D_MODEL), BF16)]
    if router:
        in_specs += [pl.BlockSpec((D_MODEL, LANES), lambda i, j: (0, 0)), pl.BlockSpec((1, LANES), lambda i, j: (0, 0))]
        args += [router_w, router_b]
        out_specs.append(row(LANES))
        out_shape.append(jax.ShapeDtypeStruct((nb, seq, LANES), F32))
    return pl.pallas_call(
        functools.partial(_outproj_kernel, tb=tb, tl=tl, alpha=alpha, router=router),
        grid=grid, in_specs=in_specs, out_specs=out_specs, out_shape=out_shape,
        compiler_params=_cparams(("arbitrary", "arbitrary")),
        name="out_proj",
    )(*args)


def _ffn_kernel(h_ref, x_ref, wg_ref, wu_ref, wd_ref, g2_ref, lg_ref, lb_ref, *rest, tb, tl, alpha, gated):
    if gated:
        gate_ref, o_ref, acc_ref = rest
    else:
        o_ref, acc_ref = rest
    e = pl.program_id(2)
    f = pl.program_id(3)
    rows = tb * tl

    @pl.when((e == 0) & (f == 0))
    def _():
        acc_ref[...] = jnp.zeros_like(acc_ref)

    hb = h_ref[...].reshape(rows, D_MODEL)
    gt = jnp.dot(hb, wg_ref[0], preferred_element_type=F32)
    up = jnp.dot(hb, wu_ref[0], preferred_element_type=F32)
    a = (_silu(gt) * up).astype(BF16)
    part = jnp.dot(a, wd_ref[0], preferred_element_type=F32)
    if gated:
        gates = gate_ref[...].reshape(rows, LANES)
        lane = lax.broadcasted_iota(jnp.int32, (rows, LANES), 1)
        part = part * jnp.sum(jnp.where(lane == e, gates, 0.0), axis=-1, keepdims=True)
    acc_ref[...] += part

    @pl.when((e == pl.num_programs(2) - 1) & (f == pl.num_programs(3) - 1))
    def _():
        y = alpha * x_ref[...] + g2_ref[...] * acc_ref[...].reshape(tb, tl, D_MODEL)
        o_ref[...] = _ln(y) * lg_ref[...] + lb_ref[...]


def _ffn(h2, x1, w_up_b, w_down_b, g2, ln_g, ln_b, alpha, tm, tf, gates=None):
    nb, seq, _ = x1.shape
    ne = w_up_b.shape[0]
    tb, tl = _row_tile(nb, seq, tm)
    nf = D_FF // tf
    grid = (nb // tb, seq // tl, ne, nf)
    row = lambda w: pl.BlockSpec((tb, tl, w), lambda i, j, e, f: (i, j, 0))
    mod = pl.BlockSpec((tb, 1, D_MODEL), lambda i, j, e, f: (i, 0, 0))
    vec = pl.BlockSpec((1, 1, D_MODEL), lambda i, j, e, f: (0, 0, 0))
    in_specs = [row(D_MODEL), row(D_MODEL),
                pl.BlockSpec((1, D_MODEL, tf), lambda i, j, e, f: (e, 0, f)),
                pl.BlockSpec((1, D_MODEL, tf), lambda i, j, e, f: (e, 0, nf + f)),
                pl.BlockSpec((1, tf, D_MODEL), lambda i, j, e, f: (e, f, 0)),
                mod, vec, vec]
    args = [h2, x1, w_up_b, w_up_b, w_down_b, g2, ln_g.reshape(1, 1, D_MODEL), ln_b.reshape(1, 1, D_MODEL)]
    gated = gates is not None
    if gated:
        in_specs.append(row(LANES))
        args.append(gates)
    return pl.pallas_call(
        functools.partial(_ffn_kernel, tb=tb, tl=tl, alpha=alpha, gated=gated),
        grid=grid, in_specs=in_specs, out_specs=row(D_MODEL),
        out_shape=jax.ShapeDtypeStruct((nb, seq, D_MODEL), F32),
        scratch_shapes=[pltpu.VMEM((tb * tl, D_MODEL), F32)],
        compiler_params=_cparams(("arbitrary", "arbitrary", "arbitrary", "arbitrary")),
        name="ffn",
    )(*args)


def _trunk(x, mod, conv_bufs, delta_states, past, weights, tm):
    depth = weights["w_in"].shape[0]
    alpha = (2 * depth) ** 0.25
    nb, seq, _ = x.shape
    new_k, new_v, new_conv, new_delta = [], [], [], []
    slopes = jnp.broadcast_to(
        jnp.asarray([2.0 ** (-8.0 * (h + 1) / H_A) for h in range(H_A)], F32)[:, None, None], (H_A, 1, LANES))
    for l in range(depth):
        sh1, sc1, g1, sh2, sc2, g2 = [mod[l, s] for s in range(6)]
        lam_init = 0.8 - 0.6 * math.exp(-0.3 * l)
        q, k, v, qkvb, z, ba = _in_proj(x, sh1, sc1, weights["w_in"][l], tm)
        new_k.append(k.reshape(nb, seq, H_A, 2 * DQK_A))
        new_v.append(v.reshape(nb, seq, H_A, DV_A))
        if past is None:
            o_a = _attn_prompt(q, k, v, slopes, weights["lam_params"][l], weights["subln_a"][l], lam_init, 256)
        else:
            cache_k, cache_v, pool, page_table = past
            o_a = _attn_decode(q, k, v, cache_k, cache_v, l, pool, page_table,
                               weights["lam_params"][l], weights["subln_a"][l], lam_init)
        gdn = _gated_deltanet_long if seq % GDN_TILE == 0 else _gated_deltanet
        o_b, s_new = gdn(qkvb, ba, z, conv_bufs[l], delta_states[l], weights["conv_w"][l],
                         weights["head_params"][l], weights["norm_b"][l])
        new_conv.append(qkvb[:, seq - (CONV_W - 1):, :] if seq >= CONV_W - 1 else
                        jnp.concatenate([conv_bufs[l], qkvb], axis=1)[:, -(CONV_W - 1):, :])
        new_delta.append(s_new)
        if l % 2 == 0:
            x1, h2 = _out_proj(o_a, o_b, x, weights["w_out"][l], g1, weights["ln1_g"][l], weights["ln1_b"][l],
                               sh2, sc2, alpha, tm)
            j = l // 2
            x = _ffn(h2, x1, weights["dense_w_up"][j:j + 1], weights["dense_w_down"][j:j + 1], g2,
                     weights["ln2_g"][l], weights["ln2_b"][l], alpha, tm, D_FF // 2)
        else:
            j = l // 2
            x1, h2, gates = _out_proj(o_a, o_b, x, weights["w_out"][l], g1, weights["ln1_g"][l],
                                      weights["ln1_b"][l], sh2, sc2, alpha, tm,
                                      weights["router_w"][j], weights["router_b"][j])
            x = _ffn(h2, x1, weights["moe_w_up"][j], weights["moe_w_down"][j], g2,
                     weights["ln2_g"][l], weights["ln2_b"][l], alpha, tm, D_FF // 2, gates)
    return x, jnp.stack(new_k), jnp.stack(new_v), jnp.stack(new_conv), jnp.stack(new_delta)


def kernel(x_prompt, x_sample, cache_k, cache_v, state_conv, state_delta, page_table, c_prompt, c_sample,
           w_in, w_out, lam_params, subln_a, conv_w, a_log, dt_bias, norm_b, ada_w, ada_b,
           ln1_g, ln1_b, ln2_g, ln2_b, dense_w_up, dense_w_down, moe_router, moe_router_b, moe_w_up, moe_w_down):
    depth = w_in.shape[0]
    nbp, nbs = x_prompt.shape[0], x_sample.shape[0]
    a_q, a_k, a_v, b_qkv = 2 * H_A * DQK_A, 2 * H_A * DQK_A, H_A * DV_A, 3 * D_B
    n0 = a_q + a_k + a_v + b_qkv
    w_perm = jnp.concatenate(
        [w_in[:, :, :n0], w_in[:, :, n0 + 2 * H_B:], w_in[:, :, n0:n0 + 2 * H_B],
         jnp.zeros((depth, D_MODEL, LANES - 2 * H_B), w_in.dtype)], axis=-1).astype(BF16)
    n_moe = moe_router.shape[0]
    router_w = jnp.concatenate([moe_router, jnp.zeros((n_moe, D_MODEL, LANES - N_EXPERTS), moe_router.dtype)],
                               axis=-1).astype(BF16)
    router_b = jnp.concatenate([moe_router_b, jnp.zeros((n_moe, LANES - N_EXPERTS), moe_router_b.dtype)],
                               axis=-1).reshape(n_moe, 1, LANES)
    head_params = jnp.broadcast_to(jnp.stack([a_log, dt_bias], axis=-1)[..., None], (depth, H_B, 2, LANES))
    weights = dict(w_in=w_perm, w_out=w_out.astype(BF16), lam_params=lam_params, subln_a=subln_a, conv_w=conv_w,
                   head_params=head_params, norm_b=norm_b, ln1_g=ln1_g, ln1_b=ln1_b, ln2_g=ln2_g, ln2_b=ln2_b,
                   dense_w_up=dense_w_up.astype(BF16), dense_w_down=dense_w_down.astype(BF16),
                   router_w=router_w, router_b=router_b,
                   moe_w_up=moe_w_up.astype(BF16), moe_w_down=moe_w_down.astype(BF16))

    mod = _ada_mod(jnp.concatenate([c_prompt, c_sample], axis=0), ada_w, ada_b)
    mod = mod[:, :, :, None, :]
    conv0 = jnp.zeros((depth, nbp, CONV_W - 1, 3 * D_B), x_prompt.dtype)
    delta0 = jnp.zeros((depth, nbp, H_B, DK_B, DV_B), x_prompt.dtype)
    y_p, k_p, v_p, conv_p, delta_p = _trunk(x_prompt, mod[:, :, :nbp], conv0, delta0, None, weights, 512)

    n_pool = cache_k.shape[1]
    ck = cache_k.reshape(depth * n_pool * PAGE * H_A, 2 * DQK_A)
    cv = cache_v.reshape(depth * n_pool * PAGE * H_A, DV_A)
    y_s, k_s, v_s, conv_s, delta_s = _trunk(x_sample, mod[:, :, nbp:], state_conv, state_delta,
                                            (ck, cv, n_pool, page_table), weights, 512)
    return (y_p, y_s, k_p, v_p, conv_p, delta_p, k_s, v_s, conv_s, delta_s)
```

```python
import functools
import math

import jax
import jax.numpy as jnp
from jax import lax
from jax.experimental import pallas as pl
from jax.experimental.pallas import tpu as pltpu

F32 = jnp.float32
BF16 = jnp.bfloat16

D_MODEL = 1024
H_A = 4
DV_A = 128
DQK_A = 64
D_A = H_A * DV_A
H_B = 4
DK_B = 128
DV_B = 128
D_B = H_B * DV_B
CONV_W = 4
CHUNK = 64
D_FF = 2816
N_EXPERTS = 8
LN_EPS = 1e-5
PAGE = 128
LANES = 128
SUBLANES = 8
VMEM_LIMIT = 56 * 1024 * 1024
NEG_INF = float("-inf")

IN_SEGS = (("q", 0, 512), ("k", 512, 1024), ("v", 1024, 1536), ("qkvb", 1536, 3072),
           ("z", 3072, 3584), ("ba", 3584, 3712))
D_IN_PAD = 3712


def _cparams(sem):
    return pltpu.CompilerParams(dimension_semantics=sem, vmem_limit_bytes=VMEM_LIMIT)


def _ln(x):
    mu = jnp.mean(x, axis=-1, keepdims=True)
    xc = x - mu
    var = jnp.mean(xc * xc, axis=-1, keepdims=True)
    return xc * lax.rsqrt(var + LN_EPS)


def _silu(x):
    return x * jax.nn.sigmoid(x)


def _row_tile(nb, seq, target):
    if seq >= target:
        assert seq % target == 0
        return 1, target
    tb = max(1, min(nb, target // seq))
    while nb % tb:
        tb -= 1
    return tb, seq


def _ada_kernel(c_ref, w_ref, b_ref, o_ref):
    a = _silu(c_ref[...]).astype(BF16)
    o_ref[0, 0] = jnp.dot(a, w_ref[0].astype(BF16), preferred_element_type=F32) + b_ref[0, 0]


def _ada_mod(c_all, ada_w, ada_b):
    depth = ada_w.shape[0]
    nb = c_all.shape[0]
    return pl.pallas_call(
        _ada_kernel,
        grid=(depth, 6),
        in_specs=[pl.BlockSpec((nb, D_MODEL), lambda l, s: (0, 0)),
                  pl.BlockSpec((1, D_MODEL, D_MODEL), lambda l, s: (l, 0, s)),
                  pl.BlockSpec((1, 1, 1, D_MODEL), lambda l, s: (l, s, 0, 0))],
        out_specs=pl.BlockSpec((1, 1, nb, D_MODEL), lambda l, s: (l, s, 0, 0)),
        out_shape=jax.ShapeDtypeStruct((depth, 6, nb, D_MODEL), F32),
        compiler_params=_cparams(("arbitrary", "arbitrary")),
        name="ada_mod",
    )(c_all, ada_w, ada_b.reshape(depth, 6, 1, D_MODEL))


def _inproj_kernel(x_ref, sh_ref, sc_ref, w_ref, q_ref, k_ref, v_ref, qkvb_ref, z_ref, ba_ref, *, tb, tl):
    h = _ln(x_ref[...]) * (1.0 + sc_ref[...]) + sh_ref[...]
    hb = h.reshape(tb * tl, D_MODEL).astype(BF16)
    outs = dict(q=q_ref, k=k_ref, v=v_ref, qkvb=qkvb_ref, z=z_ref, ba=ba_ref)
    for name, lo, hi in IN_SEGS:
        r = jnp.dot(hb, w_ref[:, lo:hi], preferred_element_type=F32)
        outs[name][...] = r.reshape(tb, tl, hi - lo).astype(outs[name].dtype)


def _in_proj(x, sh, sc, w_perm, tm):
    nb, seq, _ = x.shape
    tb, tl = _row_tile(nb, seq, tm)
    grid = (nb // tb, seq // tl)
    row = lambda w: pl.BlockSpec((tb, tl, w), lambda i, j: (i, j, 0))
    mod = pl.BlockSpec((tb, 1, D_MODEL), lambda i, j: (i, 0, 0))
    widths = [hi - lo for _, lo, hi in IN_SEGS]
    dtypes = [BF16, F32, F32, F32, F32, F32]
    return pl.pallas_call(
        functools.partial(_inproj_kernel, tb=tb, tl=tl),
        grid=grid,
        in_specs=[row(D_MODEL), mod, mod, pl.BlockSpec((D_MODEL, D_IN_PAD), lambda i, j: (0, 0))],
        out_specs=[row(w) for w in widths],
        out_shape=[jax.ShapeDtypeStruct((nb, seq, w), dt) for w, dt in zip(widths, dtypes)],
        compiler_params=_cparams(("arbitrary", "arbitrary")),
        name="in_proj",
    )(x, sh, sc, w_perm)


def _lam_value(lam_ref, lam_init):
    lp = lam_ref[...]
    a = jnp.sum(lp[0:1] * lp[1:2], axis=-1, keepdims=True)
    b = jnp.sum(lp[2:3] * lp[3:4], axis=-1, keepdims=True)
    return jnp.exp(a) - jnp.exp(b) + lam_init


def _sub_ln(o, g, lam_init):
    y = o * lax.rsqrt(jnp.mean(o * o, axis=-1, keepdims=True) + 1e-6)
    return (y * g) * (1.0 - lam_init)


ATT_ROWS = 128


def _attn_kernel(q_ref, k_ref, v_ref, sl_ref, lam_ref, g_ref, o_ref, kb_ref, vb_ref, q2_ref, *stats, tq, lam_init):
    ngrp = 2 * tq // ATT_ROWS
    m_refs, l_refs, acc_refs = stats[0:ngrp], stats[ngrp:2 * ngrp], stats[2 * ngrp:3 * ngrp]
    qi = pl.program_id(2)
    nrep = tq // LANES

    @pl.when(qi == 0)
    def _():
        kb_ref[...] = k_ref[0].astype(BF16)
        vb_ref[...] = v_ref[0].astype(BF16)

    slope = jnp.concatenate([sl_ref[0]] * nrep, axis=1)
    qs = (q_ref[0].astype(F32) * (DQK_A ** -0.5)).astype(BF16)
    lane = lax.broadcasted_iota(jnp.int32, (tq, 2 * DQK_A), 1)
    zero = jnp.zeros_like(qs)
    q2_ref[0:tq, :] = jnp.where(lane < DQK_A, qs, zero)
    q2_ref[tq:2 * tq, :] = jnp.where(lane >= DQK_A, qs, zero)
    col = lax.broadcasted_iota(jnp.int32, (1, tq), 1)
    for g in range(ngrp):
        m_refs[g][...] = jnp.full_like(m_refs[g], NEG_INF)
        l_refs[g][...] = jnp.zeros_like(l_refs[g])
        acc_refs[g][...] = jnp.zeros_like(acc_refs[g])

    def step(j, masked):
        start = pl.multiple_of(j * tq, tq)
        kj = kb_ref[pl.ds(start, tq), :]
        vj = vb_ref[pl.ds(start, tq), :]
        bias = slope * (col + (j - qi) * tq).astype(F32)
        scores = [lax.dot_general(q2_ref[g * ATT_ROWS:(g + 1) * ATT_ROWS, :], kj, (((1,), (1,)), ((), ())),
                                  preferred_element_type=F32) for g in range(ngrp)]
        for g in range(ngrp):
            s = scores[g] + bias
            if masked:
                rq = (g * ATT_ROWS) % tq + lax.broadcasted_iota(jnp.int32, (ATT_ROWS, tq), 0)
                s = jnp.where(rq >= col, s, NEG_INF)
            m_prev = m_refs[g][...]
            m_new = jnp.maximum(m_prev, jnp.max(s, axis=-1, keepdims=True))
            a = jnp.exp(m_prev - m_new)
            p = jnp.exp(s - jnp.concatenate([m_new] * nrep, axis=1))
            l_refs[g][...] = a * l_refs[g][...] + jnp.sum(p, axis=-1, keepdims=True)
            acc_refs[g][...] = a * acc_refs[g][...] + jnp.dot(p.astype(BF16), vj, preferred_element_type=F32)
            m_refs[g][...] = m_new

    def body(j, carry):
        step(j, False)
        return carry

    lax.fori_loop(0, qi, body, 0)
    step(qi, True)

    lam = _lam_value(lam_ref, lam_init)
    half = ngrp // 2
    for g in range(half):
        o1 = acc_refs[g][...] / l_refs[g][...]
        o2 = acc_refs[half + g][...] / l_refs[half + g][...]
        y = _sub_ln(o1 - lam * o2, g_ref[...], lam_init)
        o_ref[0, g * ATT_ROWS:(g + 1) * ATT_ROWS, :] = y.astype(o_ref.dtype)


def _attn_prompt(q, k, v, slopes, lam_p, subln_g, lam_init, tq):
    nb, seq, _ = q.shape
    tq = min(tq, seq)
    assert seq % tq == 0 and tq % LANES == 0 and tq % ATT_ROWS == 0 and DV_A == LANES
    grid = (nb, H_A, seq // tq)
    return pl.pallas_call(
        functools.partial(_attn_kernel, tq=tq, lam_init=lam_init),
        grid=grid,
        in_specs=[pl.BlockSpec((1, tq, DV_A), lambda b, h, i: (b, i, h)),
                  pl.BlockSpec((1, seq, DV_A), lambda b, h, i: (b, 0, h)),
                  pl.BlockSpec((1, seq, DV_A), lambda b, h, i: (b, 0, h)),
                  pl.BlockSpec((1, 1, LANES), lambda b, h, i: (h, 0, 0)),
                  pl.BlockSpec((4, DQK_A), lambda b, h, i: (0, 0)),
                  pl.BlockSpec((1, DV_A), lambda b, h, i: (0, 0))],
        out_specs=pl.BlockSpec((1, tq, DV_A), lambda b, h, i: (b, i, h)),
        out_shape=jax.ShapeDtypeStruct((nb, seq, D_A), BF16),
        scratch_shapes=[pltpu.VMEM((seq, DV_A), BF16), pltpu.VMEM((seq, DV_A), BF16),
                        pltpu.VMEM((2 * tq, DV_A), BF16)]
                       + [pltpu.VMEM((ATT_ROWS, LANES), F32)] * (3 * (2 * tq // ATT_ROWS)),
        compiler_params=_cparams(("arbitrary", "arbitrary", "arbitrary")),
        name="attn_prompt",
    )(q, k, v, slopes, lam_p, subln_g.reshape(1, DV_A))


PAGES_PER_STEP = 8
PAGE_ROWS = PAGE * H_A
HEAD_SHIFT = 2


def _dg_nt(a, b):
    return lax.dot_general(a, b, (((1,), (1,)), ((), ())), preferred_element_type=F32)


def _decode_kernel(pt_ref, q_ref, kn_ref, vn_ref, lam_ref, g_ref, *rest, nq, past_len, lam_init, ngroups):
    kp_refs = rest[:PAGES_PER_STEP]
    vp_refs = rest[PAGES_PER_STEP:2 * PAGES_PER_STEP]
    o_ref = rest[2 * PAGES_PER_STEP]
    qr_ref, m_ref, l_ref, acc_ref, pad_k_ref, pad_v_ref = rest[2 * PAGES_PER_STEP + 1:]
    g = pl.program_id(1)
    rows = 2 * H_A * nq

    row = lax.broadcasted_iota(jnp.int32, (rows, 1), 0)
    hm = jnp.zeros_like(row)
    for t in range(1, 2 * H_A):
        hm = hm + (row >= t * nq).astype(jnp.int32)
    qpos = past_len + (row - nq * hm)
    rhead = jnp.zeros_like(row)
    for h in range(1, H_A):
        rhead = rhead + (row >= 2 * h * nq).astype(jnp.int32)
    slope = jnp.zeros((rows, 1), F32)
    for h in range(H_A):
        slope = jnp.where(rhead == h, 2.0 ** (-8.0 * (h + 1) / H_A), slope)

    @pl.when(g == 0)
    def _():
        lane = lax.broadcasted_iota(jnp.int32, (nq, 2 * DQK_A), 1)
        for h in range(H_A):
            qh = (q_ref[0, :, h * DV_A:(h + 1) * DV_A].astype(F32) * (DQK_A ** -0.5)).astype(BF16)
            zero = jnp.zeros_like(qh)
            qr_ref[(2 * h) * nq:(2 * h + 1) * nq, :] = jnp.where(lane < DQK_A, qh, zero)
            qr_ref[(2 * h + 1) * nq:(2 * h + 2) * nq, :] = jnp.where(lane >= DQK_A, qh, zero)
        m_ref[...] = jnp.full_like(m_ref, NEG_INF)
        l_ref[...] = jnp.zeros_like(l_ref)
        acc_ref[...] = jnp.zeros_like(acc_ref)

    qr = qr_ref[...]

    def update(kbs, vbs, base_pos, causal):
        n = len(kbs)
        ncol = n * PAGE_ROWS
        s = jnp.concatenate([_dg_nt(qr, kb) for kb in kbs], axis=1)
        col = lax.broadcasted_iota(jnp.int32, (1, ncol), 1)
        ctok = col >> HEAD_SHIFT
        kpos = base_pos + ctok
        valid = (col - (ctok << HEAD_SHIFT)) == rhead
        if causal:
            valid = valid & (kpos <= qpos) & (ctok < nq)
        s = jnp.where(valid, s - slope * (qpos - kpos).astype(F32), NEG_INF)
        m_prev = m_ref[...]
        m_new = jnp.maximum(m_prev, jnp.max(s, axis=-1, keepdims=True))
        a = jnp.exp(m_prev - m_new)
        p = jnp.exp(s - jnp.concatenate([m_new] * (ncol // LANES), axis=1)).astype(BF16)
        l_ref[...] = a * l_ref[...] + jnp.sum(p.astype(F32), axis=-1, keepdims=True)
        pv = jnp.dot(p[:, 0:PAGE_ROWS], vbs[0], preferred_element_type=F32)
        for i in range(1, n):
            pv = pv + jnp.dot(p[:, i * PAGE_ROWS:(i + 1) * PAGE_ROWS], vbs[i], preferred_element_type=F32)
        acc_ref[...] = a * acc_ref[...] + pv
        m_ref[...] = m_new

    update([r[...].astype(BF16) for r in kp_refs], [r[...].astype(BF16) for r in vp_refs],
           g * (PAGES_PER_STEP * PAGE), False)

    @pl.when(g == ngroups - 1)
    def _():
        pad_k_ref[...] = jnp.zeros_like(pad_k_ref)
        pad_v_ref[...] = jnp.zeros_like(pad_v_ref)
        pad_k_ref[0:nq * H_A, :] = kn_ref[0]
        pad_v_ref[0:nq * H_A, :] = vn_ref[0]
        update([pad_k_ref[...].astype(BF16)], [pad_v_ref[...].astype(BF16)], past_len, True)
        o = acc_ref[...] / l_ref[...]
        lam = _lam_value(lam_ref, lam_init)
        for h in range(H_A):
            o1 = o[(2 * h) * nq:(2 * h + 1) * nq, :]
            o2 = o[(2 * h + 1) * nq:(2 * h + 2) * nq, :]
            y = _sub_ln(o1 - lam * o2, g_ref[...], lam_init)
            o_ref[0, :, h * DV_A:(h + 1) * DV_A] = y.astype(o_ref.dtype)


def _attn_decode(q, k_new, v_new, cache_k, cache_v, layer, pool, page_table, lam_p, subln_g, lam_init):
    nb, nq, _ = q.shape
    n_pages = page_table.shape[1]
    assert n_pages % PAGES_PER_STEP == 0 and nq * H_A <= PAGE_ROWS and H_A == 1 << HEAD_SHIFT
    ngroups = n_pages // PAGES_PER_STEP
    past_len = n_pages * PAGE
    rows = 2 * H_A * nq

    def page_spec(i):
        return pl.BlockSpec((PAGE_ROWS, DV_A),
                            lambda b, g, pt: (layer * pool + pt[b, g * PAGES_PER_STEP + i], 0))

    per_b = lambda r, w: pl.BlockSpec((1, r, w), lambda b, g, pt: (b, 0, 0))
    grid_spec = pltpu.PrefetchScalarGridSpec(
        num_scalar_prefetch=1,
        grid=(nb, ngroups),
        in_specs=[per_b(nq, D_A), per_b(nq * H_A, DV_A), per_b(nq * H_A, DV_A),
                  pl.BlockSpec((4, DQK_A), lambda b, g, pt: (0, 0)),
                  pl.BlockSpec((1, DV_A), lambda b, g, pt: (0, 0))]
                 + [page_spec(i) for i in range(PAGES_PER_STEP)]
                 + [page_spec(i) for i in range(PAGES_PER_STEP)],
        out_specs=per_b(nq, D_A),
        scratch_shapes=[pltpu.VMEM((rows, DV_A), BF16), pltpu.VMEM((rows, LANES), F32), pltpu.VMEM((rows, LANES), F32),
                        pltpu.VMEM((rows, DV_A), F32), pltpu.VMEM((PAGE_ROWS, DV_A), F32),
                        pltpu.VMEM((PAGE_ROWS, DV_A), F32)],
    )
    return pl.pallas_call(
        functools.partial(_decode_kernel, nq=nq, past_len=past_len, lam_init=lam_init, ngroups=ngroups),
        grid_spec=grid_spec,
        out_shape=jax.ShapeDtypeStruct((nb, nq, D_A), BF16),
        compiler_params=_cparams(("arbitrary", "arbitrary")),
        name="attn_decode",
    )(page_table, q, k_new.reshape(nb, nq * H_A, DV_A), v_new.reshape(nb, nq * H_A, DV_A), lam_p,
      subln_g.reshape(1, DV_A), *([cache_k] * PAGES_PER_STEP), *([cache_v] * PAGES_PER_STEP))


def _gdn_step_kernel(x_ref, cb_ref, cw_ref, ba_ref, z_ref, s0_ref, hp_ref, ng_ref, o_ref, sn_ref, xx_ref, *, seq):
    width = 3 * D_B
    hist = SUBLANES
    xx_ref[0:hist, :] = jnp.zeros((hist, width), F32)
    xx_ref[hist - (CONV_W - 1):hist, :] = cb_ref[0]
    xx_ref[hist:hist + seq, :] = x_ref[0]
    cw = cw_ref[...]
    y = cw[CONV_W - 1:CONV_W] * xx_ref[hist:hist + seq, :]
    for i in range(CONV_W - 1):
        off = hist - (CONV_W - 1) + i
        y = y + cw[i:i + 1] * xx_ref[off:off + seq, :]
    y = _silu(y)

    def l2n(v):
        return v * lax.rsqrt(jnp.sum(v * v, axis=-1, keepdims=True) + 1e-6)

    ba = ba_ref[0]
    ng = ng_ref[...]
    for h in range(H_B):
        hc = slice(h * DK_B, (h + 1) * DK_B)
        q = l2n(y[:, h * DK_B:(h + 1) * DK_B]) * (DK_B ** -0.5)
        k = l2n(y[:, D_B + h * DK_B:D_B + (h + 1) * DK_B])
        v = y[:, 2 * D_B + h * DV_B:2 * D_B + (h + 1) * DV_B]
        beta = jax.nn.sigmoid(ba[:, h:h + 1])
        xsp = ba[:, H_B + h:H_B + h + 1] + hp_ref[h, 1:2, :]
        decay = jnp.exp(-jnp.exp(hp_ref[h, 0:1, :]) * (jnp.maximum(xsp, 0.0) + jnp.log1p(jnp.exp(-jnp.abs(xsp)))))
        kt = k.T
        qt = q.T
        bv = beta * v
        state = s0_ref[0, h]
        outs = []
        for t in range(seq):
            kcol = kt[:, t:t + 1]
            state = state * decay[t:t + 1, :]
            r = jnp.sum(kcol * state, axis=0, keepdims=True)
            state = state + kcol * (bv[t:t + 1, :] - beta[t:t + 1, :] * r)
            outs.append(jnp.sum(qt[:, t:t + 1] * state, axis=0, keepdims=True))
        o = jnp.concatenate(outs, axis=0)
        sn_ref[0, h] = state
        yo = o * lax.rsqrt(jnp.mean(o * o, axis=-1, keepdims=True) + 1e-6) * ng
        o_ref[0, :, hc] = (yo * _silu(z_ref[0, :, hc])).astype(o_ref.dtype)


def _gated_deltanet_step(qkvb, ba, z, conv_buf, s0, conv_w, head_params, norm_g):
    nb, seq, width = qkvb.shape
    row = lambda w: pl.BlockSpec((1, seq, w), lambda b: (b, 0, 0))
    st = pl.BlockSpec((1, H_B, DK_B, DV_B), lambda b: (b, 0, 0, 0))
    return pl.pallas_call(
        functools.partial(_gdn_step_kernel, seq=seq),
        grid=(nb,),
        in_specs=[row(width),
                  pl.BlockSpec((1, CONV_W - 1, width), lambda b: (b, 0, 0)),
                  pl.BlockSpec((CONV_W, width), lambda b: (0, 0)),
                  row(LANES), row(D_B), st,
                  pl.BlockSpec((H_B, 2, LANES), lambda b: (0, 0, 0)),
                  pl.BlockSpec((1, DV_B), lambda b: (0, 0))],
        out_specs=[row(D_B), st],
        out_shape=[jax.ShapeDtypeStruct((nb, seq, D_B), BF16),
                   jax.ShapeDtypeStruct((nb, H_B, DK_B, DV_B), F32)],
        scratch_shapes=[pltpu.VMEM((seq + SUBLANES, width), F32)],
        compiler_params=_cparams(("arbitrary",)),
        name="gated_deltanet_step",
    )(qkvb, conv_buf, conv_w, ba, z, s0, head_params, norm_g.reshape(1, DV_B))


GDN_TILE = 512
GDN_SUPER = 256


def _split(a):
    hi = a.astype(BF16)
    return hi, (a - hi.astype(F32)).astype(BF16)


def _dg(a, b, dims):
    return lax.dot_general(a, b, (dims, ((), ())), preferred_element_type=F32)


_NN = ((1,), (0,))
_NT = ((1,), (1,))
_TN = ((0,), (0,))


def _mm3s(asp, bsp, dims=_NN):
    (ah, al), (bh, bl) = asp, bsp
    return _dg(ah, bh, dims) + (_dg(ah, bl, dims) + _dg(al, bh, dims))


def _mm3(a, b, dims=_NN):
    return _mm3s(_split(a), _split(b), dims)


def _mm_exact_lhs(t, b):
    b1 = b.astype(BF16)
    r1 = b - b1.astype(F32)
    b2 = r1.astype(BF16)
    b3 = (r1 - b2.astype(F32)).astype(BF16)
    return _dg(t, b1, _NN) + (_dg(t, b2, _NN) + _dg(t, b3, _NN))


def _unit_lower_inverse3(lmat, size, csz):
    r = lax.broadcasted_iota(jnp.int32, (size, size), 0)
    c = lax.broadcasted_iota(jnp.int32, (size, size), 1)
    eye = (r == c).astype(F32)
    same_blk = (r // SUBLANES) == (c // SUBLANES)
    ld = jnp.where(same_blk, lmat, 0.0)
    dinv = eye - ld
    pw = ld
    for _ in range(2):
        sp = _split(pw)
        pw = _mm3s(sp, sp)
        dinv = _mm3(dinv, eye + pw)
    dsp = _split(dinv)
    n = _mm3s(dsp, _split(lmat - ld))
    out = eye - n
    pw = n
    k = 2
    while k < csz // SUBLANES:
        sp = _split(pw)
        pw = _mm3s(sp, sp)
        out = _mm3(out, eye + pw)
        k *= 2
    return _mm3s(_split(out), dsp)


def _gdn_long_kernel(x_ref, cb_ref, cw_ref, ba_ref, z_ref, s0_ref, hp_ref, ng_ref, o_ref, sn_ref,
                     xx_ref, y_ref, st_ref, u_ref, w_ref, qd_ref, kd_ref, gt_ref, qk_ref, *, tl):
    t = pl.program_id(1)
    width = 3 * D_B
    hist = SUBLANES

    @pl.when(t == 0)
    def _():
        xx_ref[0:hist, :] = jnp.zeros((hist, width), F32)
        xx_ref[hist - (CONV_W - 1):hist, :] = cb_ref[0]
        st_ref[...] = s0_ref[0]

    @pl.when(t > 0)
    def _():
        xx_ref[0:hist, :] = xx_ref[tl:tl + hist, :]

    xx_ref[hist:hist + tl, :] = x_ref[0]

    cw = cw_ref[...]
    blk = 256
    for r0 in range(0, tl, blk):
        y = cw[CONV_W - 1:CONV_W] * xx_ref[hist + r0:hist + r0 + blk, :]
        for i in range(CONV_W - 1):
            off = hist - (CONV_W - 1) + i + r0
            y = y + cw[i:i + 1] * xx_ref[off:off + blk, :]
        y_ref[r0:r0 + blk, :] = _silu(y)

    sup = GDN_SUPER
    r = lax.broadcasted_iota(jnp.int32, (sup, sup), 0)
    c = lax.broadcasted_iota(jnp.int32, (sup, sup), 1)
    same_chunk = (r // CHUNK) == (c // CHUNK)
    tril = same_chunk & (r >= c)
    strict = same_chunk & (r > c)
    diag = r == c
    sum_mat = jnp.concatenate([tril.astype(BF16), same_chunk.astype(BF16)], axis=0)

    def l2n(v):
        return v * lax.rsqrt(jnp.sum(v * v, axis=-1, keepdims=True) + 1e-6)

    for sc in range(tl // sup):
        rs = slice(sc * sup, (sc + 1) * sup)
        ba = ba_ref[0, rs, :]
        for h in range(H_B):
            hc = slice(h * DK_B, (h + 1) * DK_B)
            q = l2n(y_ref[rs, h * DK_B:(h + 1) * DK_B]) * (DK_B ** -0.5)
            k = l2n(y_ref[rs, D_B + h * DK_B:D_B + (h + 1) * DK_B])
            v = y_ref[rs, 2 * D_B + h * DV_B:2 * D_B + (h + 1) * DV_B]
            beta = jax.nn.sigmoid(ba[:, h:h + 1])
            xsp = ba[:, H_B + h:H_B + h + 1] + hp_ref[h, 1:2, :]
            gl = -jnp.exp(hp_ref[h, 0:1, :]) * (jnp.maximum(xsp, 0.0) + jnp.log1p(jnp.exp(-jnp.abs(xsp))))
            sums = _mm_exact_lhs(sum_mat, gl)
            gcum, gtot = sums[0:sup], sums[sup:2 * sup]
            gi = jnp.concatenate([gcum] * (sup // LANES), axis=1)
            gj = jnp.sum(jnp.where(diag, gi, 0.0), axis=0, keepdims=True)
            decay = jnp.where(tril, jnp.exp(jnp.where(tril, gi - gj, 0.0)), 0.0)
            eg = jnp.exp(gcum)
            kb = k * beta
            kbf = k.astype(BF16)
            lmat = jnp.where(strict, _dg(kb.astype(BF16), kbf, _NT) * decay, 0.0)
            ainv = _unit_lower_inverse3(lmat, sup, CHUNK)
            uw = _mm3(ainv, jnp.concatenate([v * beta, kb * eg], axis=1))
            u_ref[rs, hc] = uw[:, 0:DV_B]
            w_ref[rs, hc] = uw[:, DV_B:DV_B + DK_B]
            qk = _dg(q.astype(BF16), kbf, _NT) * decay
            qkc = qk[:, 0:CHUNK]
            for i in range(1, sup // CHUNK):
                qkc = qkc + qk[:, i * CHUNK:(i + 1) * CHUNK]
            qk_ref[h, rs, :] = qkc
            qd_ref[rs, hc] = q * eg
            kd_ref[rs, hc] = k * jnp.exp(gtot - gcum)
            gt_ref[rs, hc] = jnp.exp(gtot)

    ng = ng_ref[...]

    def chunk(ci, carry):
        start = pl.multiple_of(ci * CHUNK, CHUNK)
        rows = pl.ds(start, CHUNK)
        for h in range(H_B):
            hc = slice(h * DK_B, (h + 1) * DK_B)
            ssp = _split(st_ref[h])
            v_new = u_ref[rows, hc] - _mm3s(_split(w_ref[rows, hc]), ssp)
            vsp = _split(v_new)
            o = _mm3s(_split(qd_ref[rows, hc]), ssp) + _dg(qk_ref[h, rows, :].astype(BF16), vsp[0], _NN)
            st_ref[h] = st_ref[h] * gt_ref[pl.ds(start, 1), hc] + _mm3s(_split(kd_ref[rows, hc]), vsp, _TN)
            y = o * lax.rsqrt(jnp.mean(o * o, axis=-1, keepdims=True) + 1e-6) * ng
            o_ref[0, rows, hc] = (y * _silu(z_ref[0, rows, hc])).astype(o_ref.dtype)
        return carry

    lax.fori_loop(0, tl // CHUNK, chunk, 0)
    sn_ref[0] = st_ref[...]


def _gated_deltanet_long(qkvb, ba, z, conv_buf, s0, conv_w, head_params, norm_g):
    nb, seq, width = qkvb.shape
    tl = GDN_TILE
    assert seq % tl == 0 and tl % GDN_SUPER == 0
    row = lambda w: pl.BlockSpec((1, tl, w), lambda b, t: (b, t, 0))
    st = pl.BlockSpec((1, H_B, DK_B, DV_B), lambda b, t: (b, 0, 0, 0))
    scr = lambda n, w: pltpu.VMEM((n, w), F32)
    return pl.pallas_call(
        functools.partial(_gdn_long_kernel, tl=tl),
        grid=(nb, seq // tl),
        in_specs=[row(width),
                  pl.BlockSpec((1, CONV_W - 1, width), lambda b, t: (b, 0, 0)),
                  pl.BlockSpec((CONV_W, width), lambda b, t: (0, 0)),
                  row(LANES), row(D_B), st,
                  pl.BlockSpec((H_B, 2, LANES), lambda b, t: (0, 0, 0)),
                  pl.BlockSpec((1, DV_B), lambda b, t: (0, 0))],
        out_specs=[row(D_B), st],
        out_shape=[jax.ShapeDtypeStruct((nb, seq, D_B), BF16),
                   jax.ShapeDtypeStruct((nb, H_B, DK_B, DV_B), F32)],
        scratch_shapes=[scr(tl + SUBLANES, width), scr(tl, width), pltpu.VMEM((H_B, DK_B, DV_B), F32),
                        scr(tl, D_B), scr(tl, D_B), scr(tl, D_B), scr(tl, D_B), scr(tl, D_B),
                        pltpu.VMEM((H_B, tl, CHUNK), F32)],
        compiler_params=_cparams(("arbitrary", "arbitrary")),
        name="gated_deltanet_long",
    )(qkvb, conv_buf, conv_w, ba, z, s0, head_params, norm_g.reshape(1, DV_B))


def _outproj_kernel(oa_ref, ob_ref, x_ref, w_ref, g1_ref, lg_ref, lb_ref, sh_ref, sc_ref, *rest, tb, tl, alpha, region):
    router = region is not None
    if router:
        wr_ref, br_ref, x1_ref, h2_ref, pair_ref, cnt_ref, run_ref = rest
    else:
        x1_ref, h2_ref = rest
    rows = tb * tl
    oa = oa_ref[...].reshape(rows, D_A)
    ob = ob_ref[...].reshape(rows, D_B)
    mix = (jnp.dot(oa, w_ref[0:D_A, :], preferred_element_type=F32)
           + jnp.dot(ob, w_ref[D_A:D_A + D_B, :], preferred_element_type=F32))
    y = alpha * x_ref[...] + g1_ref[...] * mix.reshape(tb, tl, D_MODEL)
    x1 = _ln(y) * lg_ref[...] + lb_ref[...]
    x1_ref[...] = x1
    h2 = (_ln(x1) * (1.0 + sc_ref[...]) + sh_ref[...]).astype(BF16)
    h2_ref[...] = h2
    if router:
        @pl.when((pl.program_id(0) == 0) & (pl.program_id(1) == 0))
        def _():
            run_ref[...] = jnp.zeros_like(run_ref)

        logits = jnp.dot(h2.reshape(rows, D_MODEL), wr_ref[...], preferred_element_type=F32) + br_ref[...]
        lane = lax.broadcasted_iota(jnp.int32, (rows, LANES), 1)
        logits = jnp.where(lane < N_EXPERTS, logits, NEG_INF)
        v1 = jnp.max(logits, axis=-1, keepdims=True)
        i1 = jnp.min(jnp.where(logits == v1, lane, LANES), axis=-1, keepdims=True)
        rest_l = jnp.where(lane == i1, NEG_INF, logits)
        v2 = jnp.max(rest_l, axis=-1, keepdims=True)
        i2 = jnp.min(jnp.where(rest_l == v2, lane, LANES), axis=-1, keepdims=True)
        e2 = jnp.exp(v2 - v1)
        gate1 = 1.0 / (1.0 + e2)
        gate2 = e2 / (1.0 + e2)
        sel = jnp.where((lane == i1) | (lane == i2), 1.0, 0.0)
        r = lax.broadcasted_iota(jnp.int32, (rows, rows), 0)
        c = lax.broadcasted_iota(jnp.int32, (rows, rows), 1)
        earlier = jnp.where(r > c, 1.0, 0.0).astype(BF16)
        rank = jnp.dot(earlier, sel.astype(BF16), preferred_element_type=F32) + run_ref[...]
        run_ref[...] = run_ref[...] + jnp.sum(sel, axis=0, keepdims=True)
        slot = lane.astype(F32) * float(region) + rank
        first, second = jnp.minimum(i1, i2), jnp.maximum(i1, i2)
        slot_a = jnp.sum(jnp.where(lane == first, slot, 0.0), axis=-1, keepdims=True)
        slot_b = jnp.sum(jnp.where(lane == second, slot, 0.0), axis=-1, keepdims=True)
        gate_a = jnp.where(first == i1, gate1, gate2)
        gate_b = jnp.where(first == i1, gate2, gate1)
        pair = (jnp.where(lane == 0, slot_a, 0.0) + jnp.where(lane == 1, slot_b, 0.0)
                + jnp.where(lane == 2, gate_a, 0.0) + jnp.where(lane == 3, gate_b, 0.0))
        pair_ref[...] = pair.reshape(tb, tl, LANES)
        cnt_ref[...] = run_ref[...]


def _out_proj(oa, ob, x, w_out_b, g1, ln_g, ln_b, sh2, sc2, alpha, tm, router_w=None, router_b=None, region=None):
    nb, seq, _ = x.shape
    tb, tl = _row_tile(nb, seq, tm)
    grid = (nb // tb, seq // tl)
    row = lambda w: pl.BlockSpec((tb, tl, w), lambda i, j: (i, j, 0))
    mod = pl.BlockSpec((tb, 1, D_MODEL), lambda i, j: (i, 0, 0))
    vec = pl.BlockSpec((1, 1, D_MODEL), lambda i, j: (0, 0, 0))
    in_specs = [row(D_A), row(D_B), row(D_MODEL), pl.BlockSpec((D_MODEL, D_MODEL), lambda i, j: (0, 0)),
                mod, vec, vec, mod, mod]
    args = [oa, ob, x, w_out_b, g1, ln_g.reshape(1, 1, D_MODEL), ln_b.reshape(1, 1, D_MODEL), sh2, sc2]
    out_specs = [row(D_MODEL), row(D_MODEL)]
    out_shape = [jax.ShapeDtypeStruct((nb, seq, D_MODEL), F32), jax.ShapeDtypeStruct((nb, seq, D_MODEL), BF16)]
    scratch = []
    if region is not None:
        in_specs += [pl.BlockSpec((D_MODEL, LANES), lambda i, j: (0, 0)), pl.BlockSpec((1, LANES), lambda i, j: (0, 0))]
        args += [router_w, router_b]
        out_specs += [row(LANES), pl.BlockSpec((1, LANES), lambda i, j: (0, 0))]
        out_shape += [jax.ShapeDtypeStruct((nb, seq, LANES), F32), jax.ShapeDtypeStruct((1, LANES), F32)]
        scratch = [pltpu.VMEM((1, LANES), F32)]
    return pl.pallas_call(
        functools.partial(_outproj_kernel, tb=tb, tl=tl, alpha=alpha, region=region),
        grid=grid, in_specs=in_specs, out_specs=out_specs, out_shape=out_shape, scratch_shapes=scratch,
        compiler_params=_cparams(("arbitrary", "arbitrary")),
        name="out_proj",
    )(*args)


def _ffn_kernel(h_ref, x_ref, wg_ref, wu_ref, wd_ref, g2_ref, lg_ref, lb_ref, o_ref, acc_ref, *, tb, tl, alpha):
    f = pl.program_id(2)
    rows = tb * tl

    @pl.when(f == 0)
    def _():
        acc_ref[...] = jnp.zeros_like(acc_ref)

    hb = h_ref[...].reshape(rows, D_MODEL)
    gt = jnp.dot(hb, wg_ref[...], preferred_element_type=F32)
    up = jnp.dot(hb, wu_ref[...], preferred_element_type=F32)
    a = (_silu(gt) * up).astype(BF16)
    acc_ref[...] += jnp.dot(a, wd_ref[...], preferred_element_type=F32)

    @pl.when(f == pl.num_programs(2) - 1)
    def _():
        y = alpha * x_ref[...] + g2_ref[...] * acc_ref[...].reshape(tb, tl, D_MODEL)
        o_ref[...] = _ln(y) * lg_ref[...] + lb_ref[...]


def _ffn(h2, x1, w_up_b, w_down_b, g2, ln_g, ln_b, alpha, tm, tf):
    nb, seq, _ = x1.shape
    tb, tl = _row_tile(nb, seq, tm)
    nf = D_FF // tf
    grid = (nb // tb, seq // tl, nf)
    row = lambda w: pl.BlockSpec((tb, tl, w), lambda i, j, f: (i, j, 0))
    return pl.pallas_call(
        functools.partial(_ffn_kernel, tb=tb, tl=tl, alpha=alpha),
        grid=grid,
        in_specs=[row(D_MODEL), row(D_MODEL),
                  pl.BlockSpec((D_MODEL, tf), lambda i, j, f: (0, f)),
                  pl.BlockSpec((D_MODEL, tf), lambda i, j, f: (0, nf + f)),
                  pl.BlockSpec((tf, D_MODEL), lambda i, j, f: (f, 0)),
                  pl.BlockSpec((tb, 1, D_MODEL), lambda i, j, f: (i, 0, 0)),
                  pl.BlockSpec((1, 1, D_MODEL), lambda i, j, f: (0, 0, 0)),
                  pl.BlockSpec((1, 1, D_MODEL), lambda i, j, f: (0, 0, 0))],
        out_specs=row(D_MODEL),
        out_shape=jax.ShapeDtypeStruct((nb, seq, D_MODEL), F32),
        scratch_shapes=[pltpu.VMEM((tb * tl, D_MODEL), F32)],
        compiler_params=_cparams(("arbitrary", "arbitrary", "arbitrary")),
        name="ffn",
    )(h2, x1, w_up_b, w_up_b, w_down_b, g2, ln_g.reshape(1, 1, D_MODEL), ln_b.reshape(1, 1, D_MODEL))


MOE_ROWS = 512
ROW_TILE = (SUBLANES, LANES)
ISSUE_UNROLL = 8


def _region_rows(m):
    return -(-m // MOE_ROWS) * MOE_ROWS


def _dispatch_kernel(slot_ref, h_ref, xs_ref, src_ref, sem, *, rows):
    src_ref[...] = h_ref[...].reshape(rows, D_MODEL).astype(F32).reshape(rows, *ROW_TILE)

    def row_copy(t, k):
        return pltpu.make_async_copy(src_ref.at[t], xs_ref.at[slot_ref[k, t]], sem)

    def issue(i, c):
        for u in range(ISSUE_UNROLL):
            t = i * ISSUE_UNROLL + u
            row_copy(t, 0).start()
            row_copy(t, 1).start()
        return c

    lax.fori_loop(0, rows // ISSUE_UNROLL, issue, 0)

    def drain(i, c):
        row_copy(0, 0).wait()
        row_copy(0, 1).wait()
        return c

    lax.fori_loop(0, rows, drain, 0)


def _dispatch(h2, slots, region):
    nb, seq, _ = h2.shape
    tb, tl = _row_tile(nb, seq, MOE_ROWS)
    rows = tb * tl
    nj = seq // tl
    return pl.pallas_call(
        functools.partial(_dispatch_kernel, rows=rows),
        grid=(nb // tb, nj),
        in_specs=[pl.BlockSpec((2, rows), lambda i, j: (0, i * nj + j), memory_space=pltpu.SMEM),
                  pl.BlockSpec((tb, tl, D_MODEL), lambda i, j: (i, j, 0))],
        out_specs=pl.BlockSpec(memory_space=pl.ANY),
        out_shape=jax.ShapeDtypeStruct((N_EXPERTS * region, *ROW_TILE), F32),
        scratch_shapes=[pltpu.VMEM((rows, *ROW_TILE), F32), pltpu.SemaphoreType.DMA(())],
        compiler_params=_cparams(("arbitrary", "arbitrary")),
        name="moe_dispatch",
    )(slots, h2)


def _experts_kernel(cnt_ref, x_ref, wg_ref, wu_ref, wd_ref, o_ref, xb_ref, acc_ref):
    e, j, f = pl.program_id(0), pl.program_id(1), pl.program_id(2)
    nf = pl.num_programs(2)
    active = j * MOE_ROWS < cnt_ref[e]

    @pl.when(active & (f == 0))
    def _():
        x = x_ref[...].reshape(MOE_ROWS, D_MODEL)
        row = lax.broadcasted_iota(jnp.int32, (MOE_ROWS, 1), 0) + j * MOE_ROWS
        xb_ref[...] = jnp.where(row < cnt_ref[e], x, 0.0).astype(BF16)
        acc_ref[...] = jnp.zeros_like(acc_ref)

    @pl.when(active)
    def _():
        xb = xb_ref[...]
        gt = jnp.dot(xb, wg_ref[0], preferred_element_type=F32)
        up = jnp.dot(xb, wu_ref[0], preferred_element_type=F32)
        a = (_silu(gt) * up).astype(BF16)
        acc_ref[...] += jnp.dot(a, wd_ref[0], preferred_element_type=F32)

    @pl.when(active & (f == nf - 1))
    def _():
        o_ref[...] = acc_ref[...].reshape(MOE_ROWS, *ROW_TILE)


def _experts(xs, counts, w_up_b, w_down_b, region, tf):
    nblk = region // MOE_ROWS
    nf = D_FF // tf

    def blk(e, j, cnt):
        last = jnp.maximum((cnt[e] + MOE_ROWS - 1) // MOE_ROWS - 1, 0)
        return e * nblk + jnp.minimum(j, last)

    def fidx(e, j, f, cnt):
        return jnp.where(j * MOE_ROWS < cnt[e], f, nf - 1)

    grid_spec = pltpu.PrefetchScalarGridSpec(
        num_scalar_prefetch=1,
        grid=(N_EXPERTS, nblk, nf),
        in_specs=[pl.BlockSpec((MOE_ROWS, *ROW_TILE), lambda e, j, f, cnt: (blk(e, j, cnt), 0, 0)),
                  pl.BlockSpec((1, D_MODEL, tf), lambda e, j, f, cnt: (e, 0, fidx(e, j, f, cnt))),
                  pl.BlockSpec((1, D_MODEL, tf), lambda e, j, f, cnt: (e, 0, nf + fidx(e, j, f, cnt))),
                  pl.BlockSpec((1, tf, D_MODEL), lambda e, j, f, cnt: (e, fidx(e, j, f, cnt), 0))],
        out_specs=pl.BlockSpec((MOE_ROWS, *ROW_TILE), lambda e, j, f, cnt: (blk(e, j, cnt), 0, 0)),
        scratch_shapes=[pltpu.VMEM((MOE_ROWS, D_MODEL), BF16), pltpu.VMEM((MOE_ROWS, D_MODEL), F32)],
    )
    return pl.pallas_call(
        _experts_kernel, grid_spec=grid_spec,
        out_shape=jax.ShapeDtypeStruct((N_EXPERTS * region, *ROW_TILE), F32),
        compiler_params=_cparams(("arbitrary", "arbitrary", "arbitrary")),
        name="moe_experts",
    )(counts, xs, w_up_b, w_up_b, w_down_b)


def _combine_kernel(slot_ref, ys_ref, pair_ref, x_ref, g2_ref, lg_ref, lb_ref, o_ref, buf0, buf1, sem, *, tb, tl, alpha):
    rows = tb * tl

    def row_copy(t, k):
        return pltpu.make_async_copy(ys_ref.at[slot_ref[k, t]], (buf0 if k == 0 else buf1).at[t], sem)

    def issue(i, c):
        for u in range(ISSUE_UNROLL):
            t = i * ISSUE_UNROLL + u
            row_copy(t, 0).start()
            row_copy(t, 1).start()
        return c

    lax.fori_loop(0, rows // ISSUE_UNROLL, issue, 0)

    def drain(i, c):
        row_copy(0, 0).wait()
        row_copy(0, 1).wait()
        return c

    lax.fori_loop(0, rows, drain, 0)
    pair = pair_ref[...].reshape(rows, LANES)
    ga, gb = pair[:, 2:3], pair[:, 3:4]
    f = ga * buf0[...].reshape(rows, D_MODEL) + gb * buf1[...].reshape(rows, D_MODEL)
    y = alpha * x_ref[...] + g2_ref[...] * f.reshape(tb, tl, D_MODEL)
    o_ref[...] = _ln(y) * lg_ref[...] + lb_ref[...]


def _combine(ys, slots, pair, x1, g2, ln_g, ln_b, alpha):
    nb, seq, _ = x1.shape
    tb, tl = _row_tile(nb, seq, MOE_ROWS)
    rows = tb * tl
    nj = seq // tl
    row = lambda w: pl.BlockSpec((tb, tl, w), lambda i, j: (i, j, 0))
    return pl.pallas_call(
        functools.partial(_combine_kernel, tb=tb, tl=tl, alpha=alpha),
        grid=(nb // tb, nj),
        in_specs=[pl.BlockSpec((2, rows), lambda i, j: (0, i * nj + j), memory_space=pltpu.SMEM),
                  pl.BlockSpec(memory_space=pl.ANY), row(LANES), row(D_MODEL),
                  pl.BlockSpec((tb, 1, D_MODEL), lambda i, j: (i, 0, 0)),
                  pl.BlockSpec((1, 1, D_MODEL), lambda i, j: (0, 0, 0)),
                  pl.BlockSpec((1, 1, D_MODEL), lambda i, j: (0, 0, 0))],
        out_specs=row(D_MODEL),
        out_shape=jax.ShapeDtypeStruct((nb, seq, D_MODEL), F32),
        scratch_shapes=[pltpu.VMEM((rows, *ROW_TILE), F32), pltpu.VMEM((rows, *ROW_TILE), F32),
                        pltpu.SemaphoreType.DMA(())],
        compiler_params=_cparams(("arbitrary", "arbitrary")),
        name="moe_combine",
    )(slots, ys, pair, x1, g2, ln_g.reshape(1, 1, D_MODEL), ln_b.reshape(1, 1, D_MODEL))


def _trunk(x, mod, conv_bufs, delta_states, past, weights, tm):
    depth = weights["w_in"].shape[0]
    alpha = (2 * depth) ** 0.25
    nb, seq, _ = x.shape
    new_k, new_v, new_conv, new_delta = [], [], [], []
    slopes = jnp.broadcast_to(
        jnp.asarray([2.0 ** (-8.0 * (h + 1) / H_A) for h in range(H_A)], F32)[:, None, None], (H_A, 1, LANES))
    for l in range(depth):
        sh1, sc1, g1, sh2, sc2, g2 = [mod[l, s] for s in range(6)]
        lam_init = 0.8 - 0.6 * math.exp(-0.3 * l)
        q, k, v, qkvb, z, ba = _in_proj(x, sh1, sc1, weights["w_in"][l], tm)
        new_k.append(k.reshape(nb, seq, H_A, 2 * DQK_A))
        new_v.append(v.reshape(nb, seq, H_A, DV_A))
        if past is None:
            o_a = _attn_prompt(q, k, v, slopes, weights["lam_params"][l], weights["subln_a"][l], lam_init, 256)
        else:
            cache_k, cache_v, pool, page_table = past
            o_a = _attn_decode(q, k, v, cache_k, cache_v, l, pool, page_table,
                               weights["lam_params"][l], weights["subln_a"][l], lam_init)
        gdn = _gated_deltanet_long if seq % GDN_TILE == 0 else _gated_deltanet_step
        o_b, s_new = gdn(qkvb, ba, z, conv_bufs[l], delta_states[l], weights["conv_w"][l],
                         weights["head_params"][l], weights["norm_b"][l])
        new_conv.append(qkvb[:, seq - (CONV_W - 1):, :] if seq >= CONV_W - 1 else
                        jnp.concatenate([conv_bufs[l], qkvb], axis=1)[:, -(CONV_W - 1):, :])
        new_delta.append(s_new)
        j = l // 2
        if l % 2 == 0:
            x1, h2 = _out_proj(o_a, o_b, x, weights["w_out"][l], g1, weights["ln1_g"][l], weights["ln1_b"][l],
                               sh2, sc2, alpha, tm)
            x = _ffn(h2, x1, weights["dense_w_up"][j], weights["dense_w_down"][j], g2,
                     weights["ln2_g"][l], weights["ln2_b"][l], alpha, tm, D_FF // 2)
        else:
            region = _region_rows(nb * seq)
            x1, h2, pair, cnt = _out_proj(o_a, o_b, x, weights["w_out"][l], g1, weights["ln1_g"][l],
                                          weights["ln1_b"][l], sh2, sc2, alpha, MOE_ROWS,
                                          weights["router_w"][j], weights["router_b"][j], region)
            slots = pair[..., 0:2].reshape(nb * seq, 2).astype(jnp.int32).T
            counts = cnt[0, :N_EXPERTS].astype(jnp.int32)
            xs = _dispatch(h2, slots, region)
            ys = _experts(xs, counts, weights["moe_w_up"][j], weights["moe_w_down"][j], region, D_FF // 2)
            x = _combine(ys, slots, pair, x1, g2, weights["ln2_g"][l], weights["ln2_b"][l], alpha)
    return x, jnp.stack(new_k), jnp.stack(new_v), jnp.stack(new_conv), jnp.stack(new_delta)


def kernel(x_prompt, x_sample, cache_k, cache_v, state_conv, state_delta, page_table, c_prompt, c_sample,
           w_in, w_out, lam_params, subln_a, conv_w, a_log, dt_bias, norm_b, ada_w, ada_b,
           ln1_g, ln1_b, ln2_g, ln2_b, dense_w_up, dense_w_down, moe_router, moe_router_b, moe_w_up, moe_w_down):
    depth = w_in.shape[0]
    nbp, nbs = x_prompt.shape[0], x_sample.shape[0]
    a_q, a_k, a_v, b_qkv = 2 * H_A * DQK_A, 2 * H_A * DQK_A, H_A * DV_A, 3 * D_B
    n0 = a_q + a_k + a_v + b_qkv
    w_perm = jnp.concatenate(
        [w_in[:, :, :n0], w_in[:, :, n0 + 2 * H_B:], w_in[:, :, n0:n0 + 2 * H_B],
         jnp.zeros((depth, D_MODEL, LANES - 2 * H_B), w_in.dtype)], axis=-1).astype(BF16)
    n_moe = moe_router.shape[0]
    router_w = jnp.concatenate([moe_router, jnp.zeros((n_moe, D_MODEL, LANES - N_EXPERTS), moe_router.dtype)],
                               axis=-1).astype(BF16)
    router_b = jnp.concatenate([moe_router_b, jnp.zeros((n_moe, LANES - N_EXPERTS), moe_router_b.dtype)],
                               axis=-1).reshape(n_moe, 1, LANES)
    head_params = jnp.broadcast_to(jnp.stack([a_log, dt_bias], axis=-1)[..., None], (depth, H_B, 2, LANES))
    weights = dict(w_in=w_perm, w_out=w_out.astype(BF16), lam_params=lam_params, subln_a=subln_a, conv_w=conv_w,
                   head_params=head_params, norm_b=norm_b, ln1_g=ln1_g, ln1_b=ln1_b, ln2_g=ln2_g, ln2_b=ln2_b,
                   dense_w_up=dense_w_up.astype(BF16), dense_w_down=dense_w_down.astype(BF16),
                   router_w=router_w, router_b=router_b,
                   moe_w_up=moe_w_up.astype(BF16), moe_w_down=moe_w_down.astype(BF16))

    mod = _ada_mod(jnp.concatenate([c_prompt, c_sample], axis=0), ada_w, ada_b)
    mod = mod[:, :, :, None, :]
    conv0 = jnp.zeros((depth, nbp, CONV_W - 1, 3 * D_B), x_prompt.dtype)
    delta0 = jnp.zeros((depth, nbp, H_B, DK_B, DV_B), x_prompt.dtype)
    y_p, k_p, v_p, conv_p, delta_p = _trunk(x_prompt, mod[:, :, :nbp], conv0, delta0, None, weights, 512)

    n_pool = cache_k.shape[1]
    ck = cache_k.reshape(depth * n_pool * PAGE * H_A, 2 * DQK_A)
    cv = cache_v.reshape(depth * n_pool * PAGE * H_A, DV_A)
    y_s, k_s, v_s, conv_s, delta_s = _trunk(x_sample, mod[:, :, nbp:], state_conv, state_delta,
                                            (ck, cv, n_pool, page_table), weights, 512)
    return (y_p, y_s, k_p, v_p, conv_p, delta_p, k_s, v_s, conv_s, delta_s)
```

```python
import functools
import math

import jax
import jax.numpy as jnp
from jax import lax
from jax.experimental import pallas as pl
from jax.experimental.pallas import tpu as pltpu

F32 = jnp.float32
BF16 = jnp.bfloat16

D_MODEL = 1024
H_A = 4
DV_A = 128
DQK_A = 64
D_A = H_A * DV_A
H_B = 4
DK_B = 128
DV_B = 128
D_B = H_B * DV_B
CONV_W = 4
CHUNK = 64
D_FF = 2816
N_EXPERTS = 8
LN_EPS = 1e-5
PAGE = 128
LANES = 128
SUBLANES = 8
VMEM_LIMIT = 56 * 1024 * 1024
NEG_INF = float("-inf")

IN_SEGS = (("q", 0, 512), ("k", 512, 1024), ("v", 1024, 1536), ("qkvb", 1536, 3072),
           ("z", 3072, 3584), ("ba", 3584, 3712))
D_IN_PAD = 3712


def _cparams(sem):
    return pltpu.CompilerParams(dimension_semantics=sem, vmem_limit_bytes=VMEM_LIMIT)


def _ln(x):
    mu = jnp.mean(x, axis=-1, keepdims=True)
    xc = x - mu
    var = jnp.mean(xc * xc, axis=-1, keepdims=True)
    return xc * lax.rsqrt(var + LN_EPS)


def _silu(x):
    return x * jax.nn.sigmoid(x)


def _row_tile(nb, seq, target):
    if seq >= target:
        assert seq % target == 0
        return 1, target
    tb = max(1, min(nb, target // seq))
    while nb % tb:
        tb -= 1
    return tb, seq


def _ada_kernel(c_ref, w_ref, b_ref, o_ref):
    a = _silu(c_ref[...]).astype(BF16)
    o_ref[0, 0] = jnp.dot(a, w_ref[0].astype(BF16), preferred_element_type=F32) + b_ref[0, 0]


def _ada_mod(c_all, ada_w, ada_b):
    depth = ada_w.shape[0]
    nb = c_all.shape[0]
    return pl.pallas_call(
        _ada_kernel,
        grid=(depth, 6),
        in_specs=[pl.BlockSpec((nb, D_MODEL), lambda l, s: (0, 0)),
                  pl.BlockSpec((1, D_MODEL, D_MODEL), lambda l, s: (l, 0, s)),
                  pl.BlockSpec((1, 1, 1, D_MODEL), lambda l, s: (l, s, 0, 0))],
        out_specs=pl.BlockSpec((1, 1, nb, D_MODEL), lambda l, s: (l, s, 0, 0)),
        out_shape=jax.ShapeDtypeStruct((depth, 6, nb, D_MODEL), F32),
        compiler_params=_cparams(("arbitrary", "arbitrary")),
        name="ada_mod",
    )(c_all, ada_w, ada_b.reshape(depth, 6, 1, D_MODEL))


def _inproj_kernel(x_ref, sh_ref, sc_ref, w_ref, q_ref, k_ref, v_ref, qkvb_ref, z_ref, ba_ref, *, tb, tl):
    h = _ln(x_ref[...]) * (1.0 + sc_ref[...]) + sh_ref[...]
    hb = h.reshape(tb * tl, D_MODEL).astype(BF16)
    outs = dict(q=q_ref, k=k_ref, v=v_ref, qkvb=qkvb_ref, z=z_ref, ba=ba_ref)
    for name, lo, hi in IN_SEGS:
        r = jnp.dot(hb, w_ref[:, lo:hi], preferred_element_type=F32)
        outs[name][...] = r.reshape(tb, tl, hi - lo).astype(outs[name].dtype)


def _in_proj(x, sh, sc, w_perm, tm):
    nb, seq, _ = x.shape
    tb, tl = _row_tile(nb, seq, tm)
    grid = (nb // tb, seq // tl)
    row = lambda w: pl.BlockSpec((tb, tl, w), lambda i, j: (i, j, 0))
    mod = pl.BlockSpec((tb, 1, D_MODEL), lambda i, j: (i, 0, 0))
    widths = [hi - lo for _, lo, hi in IN_SEGS]
    dtypes = [BF16, F32, F32, F32, F32, F32]
    return pl.pallas_call(
        functools.partial(_inproj_kernel, tb=tb, tl=tl),
        grid=grid,
        in_specs=[row(D_MODEL), mod, mod, pl.BlockSpec((D_MODEL, D_IN_PAD), lambda i, j: (0, 0))],
        out_specs=[row(w) for w in widths],
        out_shape=[jax.ShapeDtypeStruct((nb, seq, w), dt) for w, dt in zip(widths, dtypes)],
        compiler_params=_cparams(("arbitrary", "arbitrary")),
        name="in_proj",
    )(x, sh, sc, w_perm)


def _lam_value(lam_ref, lam_init):
    lp = lam_ref[...]
    a = jnp.sum(lp[0:1] * lp[1:2], axis=-1, keepdims=True)
    b = jnp.sum(lp[2:3] * lp[3:4], axis=-1, keepdims=True)
    return jnp.exp(a) - jnp.exp(b) + lam_init


def _sub_ln(o, g, lam_init):
    y = o * lax.rsqrt(jnp.mean(o * o, axis=-1, keepdims=True) + 1e-6)
    return (y * g) * (1.0 - lam_init)


def _attn_kernel(q_ref, k_ref, v_ref, sl_ref, lam_ref, g_ref, o_ref, kb_ref, vt_ref, qt_ref, *acc_refs, tq, lam_init):
    qi = pl.program_id(2)
    ngrp = 2 * tq // LANES
    nblk = k_ref.shape[1] // tq

    @pl.when(qi == 0)
    def _():
        kb_ref[...] = k_ref[0].astype(BF16)
        for j in range(nblk):
            vt_ref[j] = v_ref[0, j * tq:(j + 1) * tq, :].T.astype(BF16)

    qs = q_ref[0].astype(F32) * (DQK_A ** -0.5)
    lane = lax.broadcasted_iota(jnp.int32, (tq, 2 * DQK_A), 1)
    q1 = jnp.where(lane < DQK_A, qs, 0.0)
    q2 = jnp.where(lane >= DQK_A, qs, 0.0)
    for g in range(ngrp // 2):
        rows = slice(g * LANES, (g + 1) * LANES)
        qt_ref[:, g * LANES:(g + 1) * LANES] = q1[rows, :].T.astype(BF16)
        qt_ref[:, (ngrp // 2 + g) * LANES:(ngrp // 2 + g + 1) * LANES] = q2[rows, :].T.astype(BF16)
    for g in range(ngrp):
        acc_refs[g][...] = jnp.zeros_like(acc_refs[g])

    slope = sl_ref[0]
    krow = lax.broadcasted_iota(jnp.int32, (tq, LANES), 0)
    qcol = lax.broadcasted_iota(jnp.int32, (tq, LANES), 1)
    qt = qt_ref[...]

    def step(j, stats, masked):
        start = pl.multiple_of(j * tq, tq)
        st = lax.dot_general(kb_ref[pl.ds(start, tq), :], qt, (((1,), (0,)), ((), ())),
                             preferred_element_type=F32)
        vt = vt_ref[j]
        bias = slope * (krow + (j - qi) * tq).astype(F32)
        new_stats = []
        for g in range(ngrp):
            m_prev, l_prev = stats[g]
            s = st[:, g * LANES:(g + 1) * LANES] + bias
            if masked:
                s = jnp.where(krow <= qcol + (g * LANES) % tq, s, NEG_INF)
            m_new = jnp.maximum(m_prev, jnp.max(s, axis=0, keepdims=True))
            a = jnp.exp(m_prev - m_new)
            p = jnp.exp(s - m_new)
            l_new = a * l_prev + jnp.sum(p, axis=0, keepdims=True)
            acc_refs[g][...] = a * acc_refs[g][...] + jnp.dot(vt, p.astype(BF16), preferred_element_type=F32)
            new_stats.append((m_new, l_new))
        return tuple(new_stats)

    init = tuple((jnp.full((1, LANES), NEG_INF, F32), jnp.zeros((1, LANES), F32)) for _ in range(ngrp))
    stats = lax.fori_loop(0, qi, lambda j, c: step(j, c, False), init)
    stats = step(qi, stats, True)

    lam = _lam_value(lam_ref, lam_init)
    half = ngrp // 2
    for g in range(half):
        o1 = acc_refs[g][...] / stats[g][1]
        o2 = acc_refs[half + g][...] / stats[half + g][1]
        od = o1 - lam * o2
        y = od * lax.rsqrt(jnp.mean(od * od, axis=0, keepdims=True) + 1e-6) * g_ref[...] * (1.0 - lam_init)
        o_ref[0, g * LANES:(g + 1) * LANES, :] = y.T.astype(o_ref.dtype)


def _attn_prompt(q, k, v, slopes, lam_p, subln_g, lam_init, tq):
    nb, seq, _ = q.shape
    tq = min(tq, seq)
    assert seq % tq == 0 and tq % LANES == 0 and DV_A == LANES
    grid = (nb, H_A, seq // tq)
    g_cols = jnp.broadcast_to(subln_g.reshape(DV_A, 1), (DV_A, LANES))
    return pl.pallas_call(
        functools.partial(_attn_kernel, tq=tq, lam_init=lam_init),
        grid=grid,
        in_specs=[pl.BlockSpec((1, tq, DV_A), lambda b, h, i: (b, i, h)),
                  pl.BlockSpec((1, seq, DV_A), lambda b, h, i: (b, 0, h)),
                  pl.BlockSpec((1, seq, DV_A), lambda b, h, i: (b, 0, h)),
                  pl.BlockSpec((1, 1, LANES), lambda b, h, i: (h, 0, 0)),
                  pl.BlockSpec((4, DQK_A), lambda b, h, i: (0, 0)),
                  pl.BlockSpec((DV_A, LANES), lambda b, h, i: (0, 0))],
        out_specs=pl.BlockSpec((1, tq, DV_A), lambda b, h, i: (b, i, h)),
        out_shape=jax.ShapeDtypeStruct((nb, seq, D_A), BF16),
        scratch_shapes=[pltpu.VMEM((seq, DV_A), BF16), pltpu.VMEM((seq // tq, DV_A, tq), BF16),
                        pltpu.VMEM((2 * DQK_A, 2 * tq), BF16)]
                       + [pltpu.VMEM((DV_A, LANES), F32)] * (2 * tq // LANES),
        compiler_params=_cparams(("arbitrary", "arbitrary", "arbitrary")),
        name="attn_prompt",
    )(q, k, v, slopes, lam_p, g_cols)


PAGES_PER_STEP = 8
PAGE_ROWS = PAGE * H_A
HEAD_SHIFT = 2


def _dg_nt(a, b):
    return lax.dot_general(a, b, (((1,), (1,)), ((), ())), preferred_element_type=F32)


def _decode_kernel(pt_ref, q_ref, kn_ref, vn_ref, lam_ref, g_ref, *rest, nq, past_len, lam_init, ngroups):
    kp_refs = rest[:PAGES_PER_STEP]
    vp_refs = rest[PAGES_PER_STEP:2 * PAGES_PER_STEP]
    o_ref = rest[2 * PAGES_PER_STEP]
    qr_ref, m_ref, l_ref, acc_ref, pad_k_ref, pad_v_ref = rest[2 * PAGES_PER_STEP + 1:]
    g = pl.program_id(1)
    rows = 2 * H_A * nq

    row = lax.broadcasted_iota(jnp.int32, (rows, 1), 0)
    hm = jnp.zeros_like(row)
    for t in range(1, 2 * H_A):
        hm = hm + (row >= t * nq).astype(jnp.int32)
    qpos = past_len + (row - nq * hm)
    rhead = jnp.zeros_like(row)
    for h in range(1, H_A):
        rhead = rhead + (row >= 2 * h * nq).astype(jnp.int32)
    slope = jnp.zeros((rows, 1), F32)
    for h in range(H_A):
        slope = jnp.where(rhead == h, 2.0 ** (-8.0 * (h + 1) / H_A), slope)

    @pl.when(g == 0)
    def _():
        lane = lax.broadcasted_iota(jnp.int32, (nq, 2 * DQK_A), 1)
        for h in range(H_A):
            qh = (q_ref[0, :, h * DV_A:(h + 1) * DV_A].astype(F32) * (DQK_A ** -0.5)).astype(BF16)
            zero = jnp.zeros_like(qh)
            qr_ref[(2 * h) * nq:(2 * h + 1) * nq, :] = jnp.where(lane < DQK_A, qh, zero)
            qr_ref[(2 * h + 1) * nq:(2 * h + 2) * nq, :] = jnp.where(lane >= DQK_A, qh, zero)
        m_ref[...] = jnp.full_like(m_ref, NEG_INF)
        l_ref[...] = jnp.zeros_like(l_ref)
        acc_ref[...] = jnp.zeros_like(acc_ref)

    qr = qr_ref[...]

    def update(kbs, vbs, base_pos, causal):
        n = len(kbs)
        ncol = n * PAGE_ROWS
        s = jnp.concatenate([_dg_nt(qr, kb) for kb in kbs], axis=1)
        col = lax.broadcasted_iota(jnp.int32, (1, ncol), 1)
        ctok = col >> HEAD_SHIFT
        kpos = base_pos + ctok
        valid = (col - (ctok << HEAD_SHIFT)) == rhead
        if causal:
            valid = valid & (kpos <= qpos) & (ctok < nq)
        s = jnp.where(valid, s - slope * (qpos - kpos).astype(F32), NEG_INF)
        m_prev = m_ref[...]
        m_new = jnp.maximum(m_prev, jnp.max(s, axis=-1, keepdims=True))
        a = jnp.exp(m_prev - m_new)
        p = jnp.exp(s - jnp.concatenate([m_new] * (ncol // LANES), axis=1)).astype(BF16)
        l_ref[...] = a * l_ref[...] + jnp.sum(p.astype(F32), axis=-1, keepdims=True)
        pv = jnp.dot(p[:, 0:PAGE_ROWS], vbs[0], preferred_element_type=F32)
        for i in range(1, n):
            pv = pv + jnp.dot(p[:, i * PAGE_ROWS:(i + 1) * PAGE_ROWS], vbs[i], preferred_element_type=F32)
        acc_ref[...] = a * acc_ref[...] + pv
        m_ref[...] = m_new

    update([r[...].astype(BF16) for r in kp_refs], [r[...].astype(BF16) for r in vp_refs],
           g * (PAGES_PER_STEP * PAGE), False)

    @pl.when(g == ngroups - 1)
    def _():
        pad_k_ref[...] = jnp.zeros_like(pad_k_ref)
        pad_v_ref[...] = jnp.zeros_like(pad_v_ref)
        pad_k_ref[0:nq * H_A, :] = kn_ref[0]
        pad_v_ref[0:nq * H_A, :] = vn_ref[0]
        update([pad_k_ref[...].astype(BF16)], [pad_v_ref[...].astype(BF16)], past_len, True)
        o = acc_ref[...] / l_ref[...]
        lam = _lam_value(lam_ref, lam_init)
        for h in range(H_A):
            o1 = o[(2 * h) * nq:(2 * h + 1) * nq, :]
            o2 = o[(2 * h + 1) * nq:(2 * h + 2) * nq, :]
            y = _sub_ln(o1 - lam * o2, g_ref[...], lam_init)
            o_ref[0, :, h * DV_A:(h + 1) * DV_A] = y.astype(o_ref.dtype)


def _attn_decode(q, k_new, v_new, cache_k, cache_v, layer, pool, page_table, lam_p, subln_g, lam_init):
    nb, nq, _ = q.shape
    n_pages = page_table.shape[1]
    assert n_pages % PAGES_PER_STEP == 0 and nq * H_A <= PAGE_ROWS and H_A == 1 << HEAD_SHIFT
    ngroups = n_pages // PAGES_PER_STEP
    past_len = n_pages * PAGE
    rows = 2 * H_A * nq

    def page_spec(i):
        return pl.BlockSpec((PAGE_ROWS, DV_A),
                            lambda b, g, pt: (layer * pool + pt[b, g * PAGES_PER_STEP + i], 0))

    per_b = lambda r, w: pl.BlockSpec((1, r, w), lambda b, g, pt: (b, 0, 0))
    grid_spec = pltpu.PrefetchScalarGridSpec(
        num_scalar_prefetch=1,
        grid=(nb, ngroups),
        in_specs=[per_b(nq, D_A), per_b(nq * H_A, DV_A), per_b(nq * H_A, DV_A),
                  pl.BlockSpec((4, DQK_A), lambda b, g, pt: (0, 0)),
                  pl.BlockSpec((1, DV_A), lambda b, g, pt: (0, 0))]
                 + [page_spec(i) for i in range(PAGES_PER_STEP)]
                 + [page_spec(i) for i in range(PAGES_PER_STEP)],
        out_specs=per_b(nq, D_A),
        scratch_shapes=[pltpu.VMEM((rows, DV_A), BF16), pltpu.VMEM((rows, LANES), F32), pltpu.VMEM((rows, LANES), F32),
                        pltpu.VMEM((rows, DV_A), F32), pltpu.VMEM((PAGE_ROWS, DV_A), F32),
                        pltpu.VMEM((PAGE_ROWS, DV_A), F32)],
    )
    return pl.pallas_call(
        functools.partial(_decode_kernel, nq=nq, past_len=past_len, lam_init=lam_init, ngroups=ngroups),
        grid_spec=grid_spec,
        out_shape=jax.ShapeDtypeStruct((nb, nq, D_A), BF16),
        compiler_params=_cparams(("arbitrary", "arbitrary")),
        name="attn_decode",
    )(page_table, q, k_new.reshape(nb, nq * H_A, DV_A), v_new.reshape(nb, nq * H_A, DV_A), lam_p,
      subln_g.reshape(1, DV_A), *([cache_k] * PAGES_PER_STEP), *([cache_v] * PAGES_PER_STEP))


def _gdn_step_kernel(x_ref, cb_ref, cw_ref, ba_ref, z_ref, s0_ref, hp_ref, ng_ref, o_ref, sn_ref, xx_ref, *, seq):
    width = 3 * D_B
    hist = SUBLANES
    xx_ref[0:hist, :] = jnp.zeros((hist, width), F32)
    xx_ref[hist - (CONV_W - 1):hist, :] = cb_ref[0]
    xx_ref[hist:hist + seq, :] = x_ref[0]
    cw = cw_ref[...]
    y = cw[CONV_W - 1:CONV_W] * xx_ref[hist:hist + seq, :]
    for i in range(CONV_W - 1):
        off = hist - (CONV_W - 1) + i
        y = y + cw[i:i + 1] * xx_ref[off:off + seq, :]
    y = _silu(y)

    def l2n(v):
        return v * lax.rsqrt(jnp.sum(v * v, axis=-1, keepdims=True) + 1e-6)

    ba = ba_ref[0]
    ng = ng_ref[...]
    for h in range(H_B):
        hc = slice(h * DK_B, (h + 1) * DK_B)
        q = l2n(y[:, h * DK_B:(h + 1) * DK_B]) * (DK_B ** -0.5)
        k = l2n(y[:, D_B + h * DK_B:D_B + (h + 1) * DK_B])
        v = y[:, 2 * D_B + h * DV_B:2 * D_B + (h + 1) * DV_B]
        beta = jax.nn.sigmoid(ba[:, h:h + 1])
        xsp = ba[:, H_B + h:H_B + h + 1] + hp_ref[h, 1:2, :]
        decay = jnp.exp(-jnp.exp(hp_ref[h, 0:1, :]) * (jnp.maximum(xsp, 0.0) + jnp.log1p(jnp.exp(-jnp.abs(xsp)))))
        kt = k.T
        qt = q.T
        bv = beta * v
        state = s0_ref[0, h]
        outs = []
        for t in range(seq):
            kcol = kt[:, t:t + 1]
            state = state * decay[t:t + 1, :]
            r = jnp.sum(kcol * state, axis=0, keepdims=True)
            state = state + kcol * (bv[t:t + 1, :] - beta[t:t + 1, :] * r)
            outs.append(jnp.sum(qt[:, t:t + 1] * state, axis=0, keepdims=True))
        o = jnp.concatenate(outs, axis=0)
        sn_ref[0, h] = state
        yo = o * lax.rsqrt(jnp.mean(o * o, axis=-1, keepdims=True) + 1e-6) * ng
        o_ref[0, :, hc] = (yo * _silu(z_ref[0, :, hc])).astype(o_ref.dtype)


def _gated_deltanet_step(qkvb, ba, z, conv_buf, s0, conv_w, head_params, norm_g):
    nb, seq, width = qkvb.shape
    row = lambda w: pl.BlockSpec((1, seq, w), lambda b: (b, 0, 0))
    st = pl.BlockSpec((1, H_B, DK_B, DV_B), lambda b: (b, 0, 0, 0))
    return pl.pallas_call(
        functools.partial(_gdn_step_kernel, seq=seq),
        grid=(nb,),
        in_specs=[row(width),
                  pl.BlockSpec((1, CONV_W - 1, width), lambda b: (b, 0, 0)),
                  pl.BlockSpec((CONV_W, width), lambda b: (0, 0)),
                  row(LANES), row(D_B), st,
                  pl.BlockSpec((H_B, 2, LANES), lambda b: (0, 0, 0)),
                  pl.BlockSpec((1, DV_B), lambda b: (0, 0))],
        out_specs=[row(D_B), st],
        out_shape=[jax.ShapeDtypeStruct((nb, seq, D_B), BF16),
                   jax.ShapeDtypeStruct((nb, H_B, DK_B, DV_B), F32)],
        scratch_shapes=[pltpu.VMEM((seq + SUBLANES, width), F32)],
        compiler_params=_cparams(("arbitrary",)),
        name="gated_deltanet_step",
    )(qkvb, conv_buf, conv_w, ba, z, s0, head_params, norm_g.reshape(1, DV_B))


GDN_TILE = 512
GDN_SUPER = 256


def _split(a):
    hi = a.astype(BF16)
    return hi, (a - hi.astype(F32)).astype(BF16)


def _dg(a, b, dims):
    return lax.dot_general(a, b, (dims, ((), ())), preferred_element_type=F32)


_NN = ((1,), (0,))
_NT = ((1,), (1,))
_TN = ((0,), (0,))


def _mm3s(asp, bsp, dims=_NN):
    (ah, al), (bh, bl) = asp, bsp
    return _dg(ah, bh, dims) + (_dg(ah, bl, dims) + _dg(al, bh, dims))


def _mm3(a, b, dims=_NN):
    return _mm3s(_split(a), _split(b), dims)


def _mm_exact_lhs(t, b):
    b1 = b.astype(BF16)
    r1 = b - b1.astype(F32)
    b2 = r1.astype(BF16)
    b3 = (r1 - b2.astype(F32)).astype(BF16)
    return _dg(t, b1, _NN) + (_dg(t, b2, _NN) + _dg(t, b3, _NN))


def _prod(a, b):
    return _dg(a.astype(BF16), b.astype(BF16), _NN)


def _unit_lower_inverse_minus_eye(lmat, size, csz):
    r = lax.broadcasted_iota(jnp.int32, (size, size), 0)
    c = lax.broadcasted_iota(jnp.int32, (size, size), 1)
    ld = jnp.where((r // SUBLANES) == (c // SUBLANES), lmat, 0.0)
    p2 = _prod(ld, ld)
    d = (p2 - ld) - _prod(ld, p2)
    p4 = _prod(p2, p2)
    d = d + p4 + _prod(d, p4)
    e = lmat - ld
    n = e + _prod(d, e)
    out = -n
    pw = n
    k = 2
    while k < csz // SUBLANES:
        pw = _prod(pw, pw)
        out = out + pw + _prod(out, pw)
        k *= 2
    return out + d + _prod(out, d)


def _gdn_long_kernel(x_ref, cb_ref, cw_ref, ba_ref, z_ref, s0_ref, hp_ref, ng_ref, o_ref, sn_ref,
                     xx_ref, y_ref, st_ref, u_ref, w_ref, qd_ref, kd_ref, gt_ref, qk_ref, *, tl):
    t = pl.program_id(1)
    width = 3 * D_B
    hist = SUBLANES

    @pl.when(t == 0)
    def _():
        xx_ref[0:hist, :] = jnp.zeros((hist, width), F32)
        xx_ref[hist - (CONV_W - 1):hist, :] = cb_ref[0]
        st_ref[...] = s0_ref[0]

    @pl.when(t > 0)
    def _():
        xx_ref[0:hist, :] = xx_ref[tl:tl + hist, :]

    xx_ref[hist:hist + tl, :] = x_ref[0]

    cw = cw_ref[...]
    blk = 256
    for r0 in range(0, tl, blk):
        y = cw[CONV_W - 1:CONV_W] * xx_ref[hist + r0:hist + r0 + blk, :]
        for i in range(CONV_W - 1):
            off = hist - (CONV_W - 1) + i + r0
            y = y + cw[i:i + 1] * xx_ref[off:off + blk, :]
        y_ref[r0:r0 + blk, :] = _silu(y)

    sup = GDN_SUPER
    r = lax.broadcasted_iota(jnp.int32, (sup, sup), 0)
    c = lax.broadcasted_iota(jnp.int32, (sup, sup), 1)
    same_chunk = (r // CHUNK) == (c // CHUNK)
    tril = same_chunk & (r >= c)
    strict = same_chunk & (r > c)
    diag = r == c
    sum_mat = jnp.concatenate([tril.astype(BF16), same_chunk.astype(BF16)], axis=0)

    def l2n(v):
        return v * lax.rsqrt(jnp.sum(v * v, axis=-1, keepdims=True) + 1e-6)

    for sc in range(tl // sup):
        rs = slice(sc * sup, (sc + 1) * sup)
        ba = ba_ref[0, rs, :]
        for h in range(H_B):
            hc = slice(h * DK_B, (h + 1) * DK_B)
            q = l2n(y_ref[rs, h * DK_B:(h + 1) * DK_B]) * (DK_B ** -0.5)
            k = l2n(y_ref[rs, D_B + h * DK_B:D_B + (h + 1) * DK_B])
            v = y_ref[rs, 2 * D_B + h * DV_B:2 * D_B + (h + 1) * DV_B]
            beta = jax.nn.sigmoid(ba[:, h:h + 1])
            xsp = ba[:, H_B + h:H_B + h + 1] + hp_ref[h, 1:2, :]
            gl = -jnp.exp(hp_ref[h, 0:1, :]) * (jnp.maximum(xsp, 0.0) + jnp.log1p(jnp.exp(-jnp.abs(xsp))))
            sums = _mm_exact_lhs(sum_mat, gl)
            gcum, gtot = sums[0:sup], sums[sup:2 * sup]
            gi = jnp.concatenate([gcum] * (sup // LANES), axis=1)
            gj = jnp.sum(jnp.where(diag, gi, 0.0), axis=0, keepdims=True)
            decay = jnp.where(tril, jnp.exp(jnp.where(tril, gi - gj, 0.0)), 0.0)
            eg = jnp.exp(gcum)
            kb = k * beta
            kbf = k.astype(BF16)
            lmat = jnp.where(strict, _dg(kb.astype(BF16), kbf, _NT) * decay, 0.0)
            rhs = jnp.concatenate([v * beta, kb * eg], axis=1)
            uw = rhs + _mm3(_unit_lower_inverse_minus_eye(lmat, sup, CHUNK), rhs)
            u_ref[rs, hc] = uw[:, 0:DV_B]
            w_ref[rs, hc] = uw[:, DV_B:DV_B + DK_B]
            qk = _dg(q.astype(BF16), kbf, _NT) * decay
            qkc = qk[:, 0:CHUNK]
            for i in range(1, sup // CHUNK):
                qkc = qkc + qk[:, i * CHUNK:(i + 1) * CHUNK]
            qk_ref[h, rs, :] = qkc
            qd_ref[rs, hc] = q * eg
            kd_ref[rs, hc] = k * jnp.exp(gtot - gcum)
            gt_ref[rs, hc] = jnp.exp(gtot)

    ng = ng_ref[...]

    def chunk(ci, carry):
        start = pl.multiple_of(ci * CHUNK, CHUNK)
        rows = pl.ds(start, CHUNK)
        for h in range(H_B):
            hc = slice(h * DK_B, (h + 1) * DK_B)
            ssp = _split(st_ref[h])
            v_new = u_ref[rows, hc] - _mm3s(_split(w_ref[rows, hc]), ssp)
            vsp = _split(v_new)
            o = _dg(qd_ref[rows, hc].astype(BF16), ssp[0], _NN) + _dg(qk_ref[h, rows, :].astype(BF16), vsp[0], _NN)
            st_ref[h] = st_ref[h] * gt_ref[pl.ds(start, 1), hc] + _mm3s(_split(kd_ref[rows, hc]), vsp, _TN)
            y = o * lax.rsqrt(jnp.mean(o * o, axis=-1, keepdims=True) + 1e-6) * ng
            o_ref[0, rows, hc] = (y * _silu(z_ref[0, rows, hc])).astype(o_ref.dtype)
        return carry

    lax.fori_loop(0, tl // CHUNK, chunk, 0)
    sn_ref[0] = st_ref[...]


def _gated_deltanet_long(qkvb, ba, z, conv_buf, s0, conv_w, head_params, norm_g):
    nb, seq, width = qkvb.shape
    tl = GDN_TILE
    assert seq % tl == 0 and tl % GDN_SUPER == 0
    row = lambda w: pl.BlockSpec((1, tl, w), lambda b, t: (b, t, 0))
    st = pl.BlockSpec((1, H_B, DK_B, DV_B), lambda b, t: (b, 0, 0, 0))
    scr = lambda n, w: pltpu.VMEM((n, w), F32)
    return pl.pallas_call(
        functools.partial(_gdn_long_kernel, tl=tl),
        grid=(nb, seq // tl),
        in_specs=[row(width),
                  pl.BlockSpec((1, CONV_W - 1, width), lambda b, t: (b, 0, 0)),
                  pl.BlockSpec((CONV_W, width), lambda b, t: (0, 0)),
                  row(LANES), row(D_B), st,
                  pl.BlockSpec((H_B, 2, LANES), lambda b, t: (0, 0, 0)),
                  pl.BlockSpec((1, DV_B), lambda b, t: (0, 0))],
        out_specs=[row(D_B), st],
        out_shape=[jax.ShapeDtypeStruct((nb, seq, D_B), BF16),
                   jax.ShapeDtypeStruct((nb, H_B, DK_B, DV_B), F32)],
        scratch_shapes=[scr(tl + SUBLANES, width), scr(tl, width), pltpu.VMEM((H_B, DK_B, DV_B), F32),
                        scr(tl, D_B), scr(tl, D_B), scr(tl, D_B), scr(tl, D_B), scr(tl, D_B),
                        pltpu.VMEM((H_B, tl, CHUNK), F32)],
        compiler_params=_cparams(("arbitrary", "arbitrary")),
        name="gated_deltanet_long",
    )(qkvb, conv_buf, conv_w, ba, z, s0, head_params, norm_g.reshape(1, DV_B))


def _outproj_kernel(oa_ref, ob_ref, x_ref, w_ref, g1_ref, lg_ref, lb_ref, sh_ref, sc_ref, *rest, tb, tl, alpha, region):
    router = region is not None
    if router:
        wr_ref, br_ref, x1_ref, h2_ref, pair_ref, cnt_ref, run_ref = rest
    else:
        x1_ref, h2_ref = rest
    rows = tb * tl
    oa = oa_ref[...].reshape(rows, D_A)
    ob = ob_ref[...].reshape(rows, D_B)
    mix = (jnp.dot(oa, w_ref[0:D_A, :], preferred_element_type=F32)
           + jnp.dot(ob, w_ref[D_A:D_A + D_B, :], preferred_element_type=F32))
    y = alpha * x_ref[...] + g1_ref[...] * mix.reshape(tb, tl, D_MODEL)
    x1 = _ln(y) * lg_ref[...] + lb_ref[...]
    x1_ref[...] = x1
    h2 = (_ln(x1) * (1.0 + sc_ref[...]) + sh_ref[...]).astype(BF16)
    h2_ref[...] = h2
    if router:
        @pl.when((pl.program_id(0) == 0) & (pl.program_id(1) == 0))
        def _():
            run_ref[...] = jnp.zeros_like(run_ref)

        logits = jnp.dot(h2.reshape(rows, D_MODEL), wr_ref[...], preferred_element_type=F32) + br_ref[...]
        lane = lax.broadcasted_iota(jnp.int32, (rows, LANES), 1)
        logits = jnp.where(lane < N_EXPERTS, logits, NEG_INF)
        v1 = jnp.max(logits, axis=-1, keepdims=True)
        i1 = jnp.min(jnp.where(logits == v1, lane, LANES), axis=-1, keepdims=True)
        rest_l = jnp.where(lane == i1, NEG_INF, logits)
        v2 = jnp.max(rest_l, axis=-1, keepdims=True)
        i2 = jnp.min(jnp.where(rest_l == v2, lane, LANES), axis=-1, keepdims=True)
        e2 = jnp.exp(v2 - v1)
        gate1 = 1.0 / (1.0 + e2)
        gate2 = e2 / (1.0 + e2)
        sel = jnp.where((lane == i1) | (lane == i2), 1.0, 0.0)
        r = lax.broadcasted_iota(jnp.int32, (rows, rows), 0)
        c = lax.broadcasted_iota(jnp.int32, (rows, rows), 1)
        earlier = jnp.where(r > c, 1.0, 0.0).astype(BF16)
        rank = jnp.dot(earlier, sel.astype(BF16), preferred_element_type=F32) + run_ref[...]
        run_ref[...] = run_ref[...] + jnp.sum(sel, axis=0, keepdims=True)
        slot = lane.astype(F32) * float(region) + rank
        first, second = jnp.minimum(i1, i2), jnp.maximum(i1, i2)
        slot_a = jnp.sum(jnp.where(lane == first, slot, 0.0), axis=-1, keepdims=True)
        slot_b = jnp.sum(jnp.where(lane == second, slot, 0.0), axis=-1, keepdims=True)
        gate_a = jnp.where(first == i1, gate1, gate2)
        gate_b = jnp.where(first == i1, gate2, gate1)
        pair = (jnp.where(lane == 0, slot_a, 0.0) + jnp.where(lane == 1, slot_b, 0.0)
                + jnp.where(lane == 2, gate_a, 0.0) + jnp.where(lane == 3, gate_b, 0.0))
        pair_ref[...] = pair.reshape(tb, tl, LANES)
        cnt_ref[...] = run_ref[...]


def _out_proj(oa, ob, x, w_out_b, g1, ln_g, ln_b, sh2, sc2, alpha, tm, router_w=None, router_b=None, region=None):
    nb, seq, _ = x.shape
    tb, tl = _row_tile(nb, seq, tm)
    grid = (nb // tb, seq // tl)
    row = lambda w: pl.BlockSpec((tb, tl, w), lambda i, j: (i, j, 0))
    mod = pl.BlockSpec((tb, 1, D_MODEL), lambda i, j: (i, 0, 0))
    vec = pl.BlockSpec((1, 1, D_MODEL), lambda i, j: (0, 0, 0))
    in_specs = [row(D_A), row(D_B), row(D_MODEL), pl.BlockSpec((D_MODEL, D_MODEL), lambda i, j: (0, 0)),
                mod, vec, vec, mod, mod]
    args = [oa, ob, x, w_out_b, g1, ln_g.reshape(1, 1, D_MODEL), ln_b.reshape(1, 1, D_MODEL), sh2, sc2]
    out_specs = [row(D_MODEL), row(D_MODEL)]
    out_shape = [jax.ShapeDtypeStruct((nb, seq, D_MODEL), F32), jax.ShapeDtypeStruct((nb, seq, D_MODEL), BF16)]
    scratch = []
    if region is not None:
        in_specs += [pl.BlockSpec((D_MODEL, LANES), lambda i, j: (0, 0)), pl.BlockSpec((1, LANES), lambda i, j: (0, 0))]
        args += [router_w, router_b]
        out_specs += [row(LANES), pl.BlockSpec((1, LANES), lambda i, j: (0, 0))]
        out_shape += [jax.ShapeDtypeStruct((nb, seq, LANES), F32), jax.ShapeDtypeStruct((1, LANES), F32)]
        scratch = [pltpu.VMEM((1, LANES), F32)]
    return pl.pallas_call(
        functools.partial(_outproj_kernel, tb=tb, tl=tl, alpha=alpha, region=region),
        grid=grid, in_specs=in_specs, out_specs=out_specs, out_shape=out_shape, scratch_shapes=scratch,
        compiler_params=_cparams(("arbitrary", "arbitrary")),
        name="out_proj",
    )(*args)


def _ffn_kernel(h_ref, x_ref, wg_ref, wu_ref, wd_ref, g2_ref, lg_ref, lb_ref, o_ref, acc_ref, *, tb, tl, alpha):
    f = pl.program_id(2)
    rows = tb * tl

    @pl.when(f == 0)
    def _():
        acc_ref[...] = jnp.zeros_like(acc_ref)

    hb = h_ref[...].reshape(rows, D_MODEL)
    gt = jnp.dot(hb, wg_ref[...], preferred_element_type=F32)
    up = jnp.dot(hb, wu_ref[...], preferred_element_type=F32)
    a = (_silu(gt) * up).astype(BF16)
    acc_ref[...] += jnp.dot(a, wd_ref[...], preferred_element_type=F32)

    @pl.when(f == pl.num_programs(2) - 1)
    def _():
        y = alpha * x_ref[...] + g2_ref[...] * acc_ref[...].reshape(tb, tl, D_MODEL)
        o_ref[...] = _ln(y) * lg_ref[...] + lb_ref[...]


def _ffn(h2, x1, w_up_b, w_down_b, g2, ln_g, ln_b, alpha, tm, tf):
    nb, seq, _ = x1.shape
    tb, tl = _row_tile(nb, seq, tm)
    nf = D_FF // tf
    grid = (nb // tb, seq // tl, nf)
    row = lambda w: pl.BlockSpec((tb, tl, w), lambda i, j, f: (i, j, 0))
    return pl.pallas_call(
        functools.partial(_ffn_kernel, tb=tb, tl=tl, alpha=alpha),
        grid=grid,
        in_specs=[row(D_MODEL), row(D_MODEL),
                  pl.BlockSpec((D_MODEL, tf), lambda i, j, f: (0, f)),
                  pl.BlockSpec((D_MODEL, tf), lambda i, j, f: (0, nf + f)),
                  pl.BlockSpec((tf, D_MODEL), lambda i, j, f: (f, 0)),
                  pl.BlockSpec((tb, 1, D_MODEL), lambda i, j, f: (i, 0, 0)),
                  pl.BlockSpec((1, 1, D_MODEL), lambda i, j, f: (0, 0, 0)),
                  pl.BlockSpec((1, 1, D_MODEL), lambda i, j, f: (0, 0, 0))],
        out_specs=row(D_MODEL),
        out_shape=jax.ShapeDtypeStruct((nb, seq, D_MODEL), F32),
        scratch_shapes=[pltpu.VMEM((tb * tl, D_MODEL), F32)],
        compiler_params=_cparams(("arbitrary", "arbitrary", "arbitrary")),
        name="ffn",
    )(h2, x1, w_up_b, w_up_b, w_down_b, g2, ln_g.reshape(1, 1, D_MODEL), ln_b.reshape(1, 1, D_MODEL))


MOE_ROWS = 512
ROW_TILE = (SUBLANES, LANES)
ISSUE_UNROLL = 8


def _region_rows(m):
    return -(-m // MOE_ROWS) * MOE_ROWS


def _dispatch_kernel(slot_ref, h_ref, xs_ref, src_ref, sem, *, rows):
    src_ref[...] = h_ref[...].reshape(rows, D_MODEL).astype(F32).reshape(rows, *ROW_TILE)

    def row_copy(t, k):
        return pltpu.make_async_copy(src_ref.at[t], xs_ref.at[slot_ref[k, t]], sem)

    def issue(i, c):
        for u in range(ISSUE_UNROLL):
            t = i * ISSUE_UNROLL + u
            row_copy(t, 0).start()
            row_copy(t, 1).start()
        return c

    lax.fori_loop(0, rows // ISSUE_UNROLL, issue, 0)

    def drain(i, c):
        row_copy(0, 0).wait()
        row_copy(0, 1).wait()
        return c

    lax.fori_loop(0, rows, drain, 0)


def _dispatch(h2, slots, region):
    nb, seq, _ = h2.shape
    tb, tl = _row_tile(nb, seq, MOE_ROWS)
    rows = tb * tl
    nj = seq // tl
    return pl.pallas_call(
        functools.partial(_dispatch_kernel, rows=rows),
        grid=(nb // tb, nj),
        in_specs=[pl.BlockSpec((2, rows), lambda i, j: (0, i * nj + j), memory_space=pltpu.SMEM),
                  pl.BlockSpec((tb, tl, D_MODEL), lambda i, j: (i, j, 0))],
        out_specs=pl.BlockSpec(memory_space=pl.ANY),
        out_shape=jax.ShapeDtypeStruct((N_EXPERTS * region, *ROW_TILE), F32),
        scratch_shapes=[pltpu.VMEM((rows, *ROW_TILE), F32), pltpu.SemaphoreType.DMA(())],
        compiler_params=_cparams(("arbitrary", "arbitrary")),
        name="moe_dispatch",
    )(slots, h2)


def _experts_kernel(cnt_ref, x_ref, wg_ref, wu_ref, wd_ref, o_ref, xb_ref, acc_ref):
    e, j, f = pl.program_id(0), pl.program_id(1), pl.program_id(2)
    nf = pl.num_programs(2)
    active = j * MOE_ROWS < cnt_ref[e]

    @pl.when(active & (f == 0))
    def _():
        x = x_ref[...].reshape(MOE_ROWS, D_MODEL)
        row = lax.broadcasted_iota(jnp.int32, (MOE_ROWS, 1), 0) + j * MOE_ROWS
        xb_ref[...] = jnp.where(row < cnt_ref[e], x, 0.0).astype(BF16)
        acc_ref[...] = jnp.zeros_like(acc_ref)

    @pl.when(active)
    def _():
        xb = xb_ref[...]
        gt = jnp.dot(xb, wg_ref[0], preferred_element_type=F32)
        up = jnp.dot(xb, wu_ref[0], preferred_element_type=F32)
        a = (_silu(gt) * up).astype(BF16)
        acc_ref[...] += jnp.dot(a, wd_ref[0], preferred_element_type=F32)

    @pl.when(active & (f == nf - 1))
    def _():
        o_ref[...] = acc_ref[...].reshape(MOE_ROWS, *ROW_TILE)


def _experts(xs, counts, w_up_b, w_down_b, region, tf):
    nblk = region // MOE_ROWS
    nf = D_FF // tf

    def blk(e, j, cnt):
        last = jnp.maximum((cnt[e] + MOE_ROWS - 1) // MOE_ROWS - 1, 0)
        return e * nblk + jnp.minimum(j, last)

    def fidx(e, j, f, cnt):
        return jnp.where(j * MOE_ROWS < cnt[e], f, nf - 1)

    grid_spec = pltpu.PrefetchScalarGridSpec(
        num_scalar_prefetch=1,
        grid=(N_EXPERTS, nblk, nf),
        in_specs=[pl.BlockSpec((MOE_ROWS, *ROW_TILE), lambda e, j, f, cnt: (blk(e, j, cnt), 0, 0)),
                  pl.BlockSpec((1, D_MODEL, tf), lambda e, j, f, cnt: (e, 0, fidx(e, j, f, cnt))),
                  pl.BlockSpec((1, D_MODEL, tf), lambda e, j, f, cnt: (e, 0, nf + fidx(e, j, f, cnt))),
                  pl.BlockSpec((1, tf, D_MODEL), lambda e, j, f, cnt: (e, fidx(e, j, f, cnt), 0))],
        out_specs=pl.BlockSpec((MOE_ROWS, *ROW_TILE), lambda e, j, f, cnt: (blk(e, j, cnt), 0, 0)),
        scratch_shapes=[pltpu.VMEM((MOE_ROWS, D_MODEL), BF16), pltpu.VMEM((MOE_ROWS, D_MODEL), F32)],
    )
    return pl.pallas_call(
        _experts_kernel, grid_spec=grid_spec,
        out_shape=jax.ShapeDtypeStruct((N_EXPERTS * region, *ROW_TILE), F32),
        compiler_params=_cparams(("arbitrary", "arbitrary", "arbitrary")),
        name="moe_experts",
    )(counts, xs, w_up_b, w_up_b, w_down_b)


def _combine_kernel(slot_ref, ys_ref, pair_ref, x_ref, g2_ref, lg_ref, lb_ref, o_ref, buf0, buf1, sem, *, tb, tl, alpha):
    rows = tb * tl

    def row_copy(t, k):
        return pltpu.make_async_copy(ys_ref.at[slot_ref[k, t]], (buf0 if k == 0 else buf1).at[t], sem)

    def issue(i, c):
        for u in range(ISSUE_UNROLL):
            t = i * ISSUE_UNROLL + u
            row_copy(t, 0).start()
            row_copy(t, 1).start()
        return c

    lax.fori_loop(0, rows // ISSUE_UNROLL, issue, 0)

    def drain(i, c):
        row_copy(0, 0).wait()
        row_copy(0, 1).wait()
        return c

    lax.fori_loop(0, rows, drain, 0)
    pair = pair_ref[...].reshape(rows, LANES)
    ga, gb = pair[:, 2:3], pair[:, 3:4]
    f = ga * buf0[...].reshape(rows, D_MODEL) + gb * buf1[...].reshape(rows, D_MODEL)
    y = alpha * x_ref[...] + g2_ref[...] * f.reshape(tb, tl, D_MODEL)
    o_ref[...] = _ln(y) * lg_ref[...] + lb_ref[...]


def _combine(ys, slots, pair, x1, g2, ln_g, ln_b, alpha):
    nb, seq, _ = x1.shape
    tb, tl = _row_tile(nb, seq, MOE_ROWS)
    rows = tb * tl
    nj = seq // tl
    row = lambda w: pl.BlockSpec((tb, tl, w), lambda i, j: (i, j, 0))
    return pl.pallas_call(
        functools.partial(_combine_kernel, tb=tb, tl=tl, alpha=alpha),
        grid=(nb // tb, nj),
        in_specs=[pl.BlockSpec((2, rows), lambda i, j: (0, i * nj + j), memory_space=pltpu.SMEM),
                  pl.BlockSpec(memory_space=pl.ANY), row(LANES), row(D_MODEL),
                  pl.BlockSpec((tb, 1, D_MODEL), lambda i, j: (i, 0, 0)),
                  pl.BlockSpec((1, 1, D_MODEL), lambda i, j: (0, 0, 0)),
                  pl.BlockSpec((1, 1, D_MODEL), lambda i, j: (0, 0, 0))],
        out_specs=row(D_MODEL),
        out_shape=jax.ShapeDtypeStruct((nb, seq, D_MODEL), F32),
        scratch_shapes=[pltpu.VMEM((rows, *ROW_TILE), F32), pltpu.VMEM((rows, *ROW_TILE), F32),
                        pltpu.SemaphoreType.DMA(())],
        compiler_params=_cparams(("arbitrary", "arbitrary")),
        name="moe_combine",
    )(slots, ys, pair, x1, g2, ln_g.reshape(1, 1, D_MODEL), ln_b.reshape(1, 1, D_MODEL))


def _trunk(x, mod, conv_bufs, delta_states, past, weights, tm):
    depth = weights["w_in"].shape[0]
    alpha = (2 * depth) ** 0.25
    nb, seq, _ = x.shape
    new_k, new_v, new_conv, new_delta = [], [], [], []
    slopes = jnp.broadcast_to(
        jnp.asarray([2.0 ** (-8.0 * (h + 1) / H_A) for h in range(H_A)], F32)[:, None, None], (H_A, 1, LANES))
    for l in range(depth):
        sh1, sc1, g1, sh2, sc2, g2 = [mod[l, s] for s in range(6)]
        lam_init = 0.8 - 0.6 * math.exp(-0.3 * l)
        q, k, v, qkvb, z, ba = _in_proj(x, sh1, sc1, weights["w_in"][l], tm)
        new_k.append(k.reshape(nb, seq, H_A, 2 * DQK_A))
        new_v.append(v.reshape(nb, seq, H_A, DV_A))
        if past is None:
            o_a = _attn_prompt(q, k, v, slopes, weights["lam_params"][l], weights["subln_a"][l], lam_init, 512)
        else:
            cache_k, cache_v, pool, page_table = past
            o_a = _attn_decode(q, k, v, cache_k, cache_v, l, pool, page_table,
                               weights["lam_params"][l], weights["subln_a"][l], lam_init)
        gdn = _gated_deltanet_long if seq % GDN_TILE == 0 else _gated_deltanet_step
        o_b, s_new = gdn(qkvb, ba, z, conv_bufs[l], delta_states[l], weights["conv_w"][l],
                         weights["head_params"][l], weights["norm_b"][l])
        new_conv.append(qkvb[:, seq - (CONV_W - 1):, :] if seq >= CONV_W - 1 else
                        jnp.concatenate([conv_bufs[l], qkvb], axis=1)[:, -(CONV_W - 1):, :])
        new_delta.append(s_new)
        j = l // 2
        if l % 2 == 0:
            x1, h2 = _out_proj(o_a, o_b, x, weights["w_out"][l], g1, weights["ln1_g"][l], weights["ln1_b"][l],
                               sh2, sc2, alpha, tm)
            x = _ffn(h2, x1, weights["dense_w_up"][j], weights["dense_w_down"][j], g2,
                     weights["ln2_g"][l], weights["ln2_b"][l], alpha, tm, D_FF // 2)
        else:
            region = _region_rows(nb * seq)
            x1, h2, pair, cnt = _out_proj(o_a, o_b, x, weights["w_out"][l], g1, weights["ln1_g"][l],
                                          weights["ln1_b"][l], sh2, sc2, alpha, MOE_ROWS,
                                          weights["router_w"][j], weights["router_b"][j], region)
            slots = pair[..., 0:2].reshape(nb * seq, 2).astype(jnp.int32).T
            counts = cnt[0, :N_EXPERTS].astype(jnp.int32)
            xs = _dispatch(h2, slots, region)
            ys = _experts(xs, counts, weights["moe_w_up"][j], weights["moe_w_down"][j], region, D_FF // 2)
            x = _combine(ys, slots, pair, x1, g2, weights["ln2_g"][l], weights["ln2_b"][l], alpha)
    return x, jnp.stack(new_k), jnp.stack(new_v), jnp.stack(new_conv), jnp.stack(new_delta)


def kernel(x_prompt, x_sample, cache_k, cache_v, state_conv, state_delta, page_table, c_prompt, c_sample,
           w_in, w_out, lam_params, subln_a, conv_w, a_log, dt_bias, norm_b, ada_w, ada_b,
           ln1_g, ln1_b, ln2_g, ln2_b, dense_w_up, dense_w_down, moe_router, moe_router_b, moe_w_up, moe_w_down):
    depth = w_in.shape[0]
    nbp, nbs = x_prompt.shape[0], x_sample.shape[0]
    a_q, a_k, a_v, b_qkv = 2 * H_A * DQK_A, 2 * H_A * DQK_A, H_A * DV_A, 3 * D_B
    n0 = a_q + a_k + a_v + b_qkv
    w_perm = jnp.concatenate(
        [w_in[:, :, :n0], w_in[:, :, n0 + 2 * H_B:], w_in[:, :, n0:n0 + 2 * H_B],
         jnp.zeros((depth, D_MODEL, LANES - 2 * H_B), w_in.dtype)], axis=-1).astype(BF16)
    n_moe = moe_router.shape[0]
    router_w = jnp.concatenate([moe_router, jnp.zeros((n_moe, D_MODEL, LANES - N_EXPERTS), moe_router.dtype)],
                               axis=-1).astype(BF16)
    router_b = jnp.concatenate([moe_router_b, jnp.zeros((n_moe, LANES - N_EXPERTS), moe_router_b.dtype)],
                               axis=-1).reshape(n_moe, 1, LANES)
    head_params = jnp.broadcast_to(jnp.stack([a_log, dt_bias], axis=-1)[..., None], (depth, H_B, 2, LANES))
    weights = dict(w_in=w_perm, w_out=w_out.astype(BF16), lam_params=lam_params, subln_a=subln_a, conv_w=conv_w,
                   head_params=head_params, norm_b=norm_b, ln1_g=ln1_g, ln1_b=ln1_b, ln2_g=ln2_g, ln2_b=ln2_b,
                   dense_w_up=dense_w_up.astype(BF16), dense_w_down=dense_w_down.astype(BF16),
                   router_w=router_w, router_b=router_b,
                   moe_w_up=moe_w_up.astype(BF16), moe_w_down=moe_w_down.astype(BF16))

    mod = _ada_mod(jnp.concatenate([c_prompt, c_sample], axis=0), ada_w, ada_b)
    mod = mod[:, :, :, None, :]
    conv0 = jnp.zeros((depth, nbp, CONV_W - 1, 3 * D_B), x_prompt.dtype)
    delta0 = jnp.zeros((depth, nbp, H_B, DK_B, DV_B), x_prompt.dtype)
    y_p, k_p, v_p, conv_p, delta_p = _trunk(x_prompt, mod[:, :, :nbp], conv0, delta0, None, weights, 512)

    n_pool = cache_k.shape[1]
    ck = cache_k.reshape(depth * n_pool * PAGE * H_A, 2 * DQK_A)
    cv = cache_v.reshape(depth * n_pool * PAGE * H_A, DV_A)
    y_s, k_s, v_s, conv_s, delta_s = _trunk(x_sample, mod[:, :, nbp:], state_conv, state_delta,
                                            (ck, cv, n_pool, page_table), weights, 512)
    return (y_p, y_s, k_p, v_p, conv_p, delta_p, k_s, v_s, conv_s, delta_s)
```

```python
import functools
import math

import jax
import jax.numpy as jnp
from jax import lax
from jax.experimental import pallas as pl
from jax.experimental.pallas import tpu as pltpu

F32 = jnp.float32
BF16 = jnp.bfloat16

D_MODEL = 1024
H_A = 4
DV_A = 128
DQK_A = 64
D_A = H_A * DV_A
H_B = 4
DK_B = 128
DV_B = 128
D_B = H_B * DV_B
CONV_W = 4
CHUNK = 64
D_FF = 2816
N_EXPERTS = 8
LN_EPS = 1e-5
PAGE = 128
LANES = 128
SUBLANES = 8
VMEM_LIMIT = 56 * 1024 * 1024
NEG_INF = float("-inf")

IN_SEGS = (("q", 0, 512), ("k", 512, 1024), ("v", 1024, 1536), ("qkvb", 1536, 3072),
           ("z", 3072, 3584), ("ba", 3584, 3712))
D_IN_PAD = 3712


def _cparams(sem):
    return pltpu.CompilerParams(dimension_semantics=sem, vmem_limit_bytes=VMEM_LIMIT)


def _ln(x):
    mu = jnp.mean(x, axis=-1, keepdims=True)
    xc = x - mu
    var = jnp.mean(xc * xc, axis=-1, keepdims=True)
    return xc * lax.rsqrt(var + LN_EPS)


def _silu(x):
    return x * jax.nn.sigmoid(x)


def _row_tile(nb, seq, target):
    if seq >= target:
        assert seq % target == 0
        return 1, target
    tb = max(1, min(nb, target // seq))
    while nb % tb:
        tb -= 1
    return tb, seq


def _ada_kernel(c_ref, w_ref, b_ref, o_ref):
    a = _silu(c_ref[...]).astype(BF16)
    o_ref[0, 0] = jnp.dot(a, w_ref[0].astype(BF16), preferred_element_type=F32) + b_ref[0, 0]


def _ada_mod(c_all, ada_w, ada_b):
    depth = ada_w.shape[0]
    nb = c_all.shape[0]
    return pl.pallas_call(
        _ada_kernel,
        grid=(depth, 6),
        in_specs=[pl.BlockSpec((nb, D_MODEL), lambda l, s: (0, 0)),
                  pl.BlockSpec((1, D_MODEL, D_MODEL), lambda l, s: (l, 0, s)),
                  pl.BlockSpec((1, 1, 1, D_MODEL), lambda l, s: (l, s, 0, 0))],
        out_specs=pl.BlockSpec((1, 1, nb, D_MODEL), lambda l, s: (l, s, 0, 0)),
        out_shape=jax.ShapeDtypeStruct((depth, 6, nb, D_MODEL), F32),
        compiler_params=_cparams(("arbitrary", "arbitrary")),
        name="ada_mod",
    )(c_all, ada_w, ada_b.reshape(depth, 6, 1, D_MODEL))


def _inproj_kernel(x_ref, sh_ref, sc_ref, w_ref, q_ref, k_ref, v_ref, qkvb_ref, z_ref, ba_ref, *, tb, tl):
    h = _ln(x_ref[...]) * (1.0 + sc_ref[...]) + sh_ref[...]
    hb = h.reshape(tb * tl, D_MODEL).astype(BF16)
    outs = dict(q=q_ref, k=k_ref, v=v_ref, qkvb=qkvb_ref, z=z_ref, ba=ba_ref)
    for name, lo, hi in IN_SEGS:
        r = jnp.dot(hb, w_ref[:, lo:hi], preferred_element_type=F32)
        outs[name][...] = r.reshape(tb, tl, hi - lo).astype(outs[name].dtype)


def _in_proj(x, sh, sc, w_perm, tm):
    nb, seq, _ = x.shape
    tb, tl = _row_tile(nb, seq, tm)
    grid = (nb // tb, seq // tl)
    row = lambda w: pl.BlockSpec((tb, tl, w), lambda i, j: (i, j, 0))
    mod = pl.BlockSpec((tb, 1, D_MODEL), lambda i, j: (i, 0, 0))
    widths = [hi - lo for _, lo, hi in IN_SEGS]
    dtypes = [BF16, F32, F32, F32, F32, F32]
    return pl.pallas_call(
        functools.partial(_inproj_kernel, tb=tb, tl=tl),
        grid=grid,
        in_specs=[row(D_MODEL), mod, mod, pl.BlockSpec((D_MODEL, D_IN_PAD), lambda i, j: (0, 0))],
        out_specs=[row(w) for w in widths],
        out_shape=[jax.ShapeDtypeStruct((nb, seq, w), dt) for w, dt in zip(widths, dtypes)],
        compiler_params=_cparams(("arbitrary", "arbitrary")),
        name="in_proj",
    )(x, sh, sc, w_perm)


def _lam_value(lam_ref, lam_init):
    lp = lam_ref[...]
    a = jnp.sum(lp[0:1] * lp[1:2], axis=-1, keepdims=True)
    b = jnp.sum(lp[2:3] * lp[3:4], axis=-1, keepdims=True)
    return jnp.exp(a) - jnp.exp(b) + lam_init


def _sub_ln(o, g, lam_init):
    y = o * lax.rsqrt(jnp.mean(o * o, axis=-1, keepdims=True) + 1e-6)
    return (y * g) * (1.0 - lam_init)


def _attn_kernel(q_ref, k_ref, v_ref, sl_ref, lam_ref, g_ref, o_ref, kb_ref, vt_ref, qt_ref, *acc_refs, tq, lam_init):
    qi = pl.program_id(2)
    ngrp = 2 * tq // LANES
    nblk = k_ref.shape[1] // tq

    @pl.when(qi == 0)
    def _():
        kb_ref[...] = k_ref[0].astype(BF16)
        for j in range(nblk):
            vt_ref[j] = v_ref[0, j * tq:(j + 1) * tq, :].T.astype(BF16)

    qs = q_ref[0].astype(F32) * (DQK_A ** -0.5)
    lane = lax.broadcasted_iota(jnp.int32, (tq, 2 * DQK_A), 1)
    q1 = jnp.where(lane < DQK_A, qs, 0.0)
    q2 = jnp.where(lane >= DQK_A, qs, 0.0)
    for g in range(ngrp // 2):
        rows = slice(g * LANES, (g + 1) * LANES)
        qt_ref[:, g * LANES:(g + 1) * LANES] = q1[rows, :].T.astype(BF16)
        qt_ref[:, (ngrp // 2 + g) * LANES:(ngrp // 2 + g + 1) * LANES] = q2[rows, :].T.astype(BF16)
    for g in range(ngrp):
        acc_refs[g][...] = jnp.zeros_like(acc_refs[g])

    slope = sl_ref[0]
    krow = lax.broadcasted_iota(jnp.int32, (tq, LANES), 0)
    qcol = lax.broadcasted_iota(jnp.int32, (tq, LANES), 1)
    qt = qt_ref[...]

    def step(j, stats, masked):
        start = pl.multiple_of(j * tq, tq)
        st = lax.dot_general(kb_ref[pl.ds(start, tq), :], qt, (((1,), (0,)), ((), ())),
                             preferred_element_type=F32)
        vt = vt_ref[j]
        bias = slope * (krow + (j - qi) * tq).astype(F32)
        new_stats = []
        for g in range(ngrp):
            m_prev, l_prev = stats[g]
            s = st[:, g * LANES:(g + 1) * LANES] + bias
            if masked:
                s = jnp.where(krow <= qcol + (g * LANES) % tq, s, NEG_INF)
            m_new = jnp.maximum(m_prev, jnp.max(s, axis=0, keepdims=True))
            a = jnp.exp(m_prev - m_new)
            p = jnp.exp(s - m_new)
            l_new = a * l_prev + jnp.sum(p, axis=0, keepdims=True)
            acc_refs[g][...] = a * acc_refs[g][...] + jnp.dot(vt, p.astype(BF16), preferred_element_type=F32)
            new_stats.append((m_new, l_new))
        return tuple(new_stats)

    init = tuple((jnp.full((1, LANES), NEG_INF, F32), jnp.zeros((1, LANES), F32)) for _ in range(ngrp))
    stats = lax.fori_loop(0, qi, lambda j, c: step(j, c, False), init)
    stats = step(qi, stats, True)

    lam = _lam_value(lam_ref, lam_init)
    half = ngrp // 2
    for g in range(half):
        o1 = acc_refs[g][...] / stats[g][1]
        o2 = acc_refs[half + g][...] / stats[half + g][1]
        od = o1 - lam * o2
        y = od * lax.rsqrt(jnp.mean(od * od, axis=0, keepdims=True) + 1e-6) * g_ref[...] * (1.0 - lam_init)
        o_ref[0, g * LANES:(g + 1) * LANES, :] = y.T.astype(o_ref.dtype)


def _attn_prompt(q, k, v, slopes, lam_p, subln_g, lam_init, tq):
    nb, seq, _ = q.shape
    tq = min(tq, seq)
    assert seq % tq == 0 and tq % LANES == 0 and DV_A == LANES
    grid = (nb, H_A, seq // tq)
    g_cols = jnp.broadcast_to(subln_g.reshape(DV_A, 1), (DV_A, LANES))
    return pl.pallas_call(
        functools.partial(_attn_kernel, tq=tq, lam_init=lam_init),
        grid=grid,
        in_specs=[pl.BlockSpec((1, tq, DV_A), lambda b, h, i: (b, i, h)),
                  pl.BlockSpec((1, seq, DV_A), lambda b, h, i: (b, 0, h)),
                  pl.BlockSpec((1, seq, DV_A), lambda b, h, i: (b, 0, h)),
                  pl.BlockSpec((1, 1, LANES), lambda b, h, i: (h, 0, 0)),
                  pl.BlockSpec((4, DQK_A), lambda b, h, i: (0, 0)),
                  pl.BlockSpec((DV_A, LANES), lambda b, h, i: (0, 0))],
        out_specs=pl.BlockSpec((1, tq, DV_A), lambda b, h, i: (b, i, h)),
        out_shape=jax.ShapeDtypeStruct((nb, seq, D_A), BF16),
        scratch_shapes=[pltpu.VMEM((seq, DV_A), BF16), pltpu.VMEM((seq // tq, DV_A, tq), BF16),
                        pltpu.VMEM((2 * DQK_A, 2 * tq), BF16)]
                       + [pltpu.VMEM((DV_A, LANES), F32)] * (2 * tq // LANES),
        compiler_params=_cparams(("arbitrary", "arbitrary", "arbitrary")),
        name="attn_prompt",
    )(q, k, v, slopes, lam_p, g_cols)


PAGES_PER_STEP = 16
PAGE_ROWS = PAGE * H_A
HEAD_SHIFT = 2


def _dg_nt(a, b):
    return lax.dot_general(a, b, (((1,), (1,)), ((), ())), preferred_element_type=F32)


def _decode_kernel(pt_ref, q_ref, kn_ref, vn_ref, lam_ref, g_ref, *rest, nq, past_len, lam_init, ngroups):
    kp_refs = rest[:PAGES_PER_STEP]
    vp_refs = rest[PAGES_PER_STEP:2 * PAGES_PER_STEP]
    o_ref = rest[2 * PAGES_PER_STEP]
    qr_ref, m_ref, l_ref, acc_ref, pad_k_ref, pad_v_ref = rest[2 * PAGES_PER_STEP + 1:]
    g = pl.program_id(1)
    rows = 2 * H_A * nq

    row = lax.broadcasted_iota(jnp.int32, (rows, 1), 0)
    hm = jnp.zeros_like(row)
    for t in range(1, 2 * H_A):
        hm = hm + (row >= t * nq).astype(jnp.int32)
    qpos = past_len + (row - nq * hm)
    rhead = jnp.zeros_like(row)
    for h in range(1, H_A):
        rhead = rhead + (row >= 2 * h * nq).astype(jnp.int32)
    slope = jnp.zeros((rows, 1), F32)
    for h in range(H_A):
        slope = jnp.where(rhead == h, 2.0 ** (-8.0 * (h + 1) / H_A), slope)

    @pl.when(g == 0)
    def _():
        lane = lax.broadcasted_iota(jnp.int32, (nq, 2 * DQK_A), 1)
        for h in range(H_A):
            qh = (q_ref[0, :, h * DV_A:(h + 1) * DV_A].astype(F32) * (DQK_A ** -0.5)).astype(BF16)
            zero = jnp.zeros_like(qh)
            qr_ref[(2 * h) * nq:(2 * h + 1) * nq, :] = jnp.where(lane < DQK_A, qh, zero)
            qr_ref[(2 * h + 1) * nq:(2 * h + 2) * nq, :] = jnp.where(lane >= DQK_A, qh, zero)
        m_ref[...] = jnp.full_like(m_ref, NEG_INF)
        l_ref[...] = jnp.zeros_like(l_ref)
        acc_ref[...] = jnp.zeros_like(acc_ref)

    qr = qr_ref[...]

    def update(kbs, vbs, base_pos, causal):
        n = len(kbs)
        ncol = n * PAGE_ROWS
        s = jnp.concatenate([_dg_nt(qr, kb) for kb in kbs], axis=1)
        col = lax.broadcasted_iota(jnp.int32, (1, ncol), 1)
        ctok = col >> HEAD_SHIFT
        kpos = base_pos + ctok
        valid = (col - (ctok << HEAD_SHIFT)) == rhead
        if causal:
            valid = valid & (kpos <= qpos) & (ctok < nq)
        s = jnp.where(valid, s - slope * (qpos - kpos).astype(F32), NEG_INF)
        m_prev = m_ref[...]
        m_new = jnp.maximum(m_prev, jnp.max(s, axis=-1, keepdims=True))
        a = jnp.exp(m_prev - m_new)
        p = jnp.exp(s - jnp.concatenate([m_new] * (ncol // LANES), axis=1)).astype(BF16)
        l_ref[...] = a * l_ref[...] + jnp.sum(p.astype(F32), axis=-1, keepdims=True)
        pv = jnp.dot(p[:, 0:PAGE_ROWS], vbs[0], preferred_element_type=F32)
        for i in range(1, n):
            pv = pv + jnp.dot(p[:, i * PAGE_ROWS:(i + 1) * PAGE_ROWS], vbs[i], preferred_element_type=F32)
        acc_ref[...] = a * acc_ref[...] + pv
        m_ref[...] = m_new

    update([r[...].astype(BF16) for r in kp_refs], [r[...].astype(BF16) for r in vp_refs],
           g * (PAGES_PER_STEP * PAGE), False)

    @pl.when(g == ngroups - 1)
    def _():
        pad_k_ref[...] = jnp.zeros_like(pad_k_ref)
        pad_v_ref[...] = jnp.zeros_like(pad_v_ref)
        pad_k_ref[0:nq * H_A, :] = kn_ref[0]
        pad_v_ref[0:nq * H_A, :] = vn_ref[0]
        update([pad_k_ref[...].astype(BF16)], [pad_v_ref[...].astype(BF16)], past_len, True)
        o = acc_ref[...] / l_ref[...]
        lam = _lam_value(lam_ref, lam_init)
        for h in range(H_A):
            o1 = o[(2 * h) * nq:(2 * h + 1) * nq, :]
            o2 = o[(2 * h + 1) * nq:(2 * h + 2) * nq, :]
            y = _sub_ln(o1 - lam * o2, g_ref[...], lam_init)
            o_ref[0, :, h * DV_A:(h + 1) * DV_A] = y.astype(o_ref.dtype)


def _attn_decode(q, k_new, v_new, cache_k, cache_v, layer, pool, page_table, lam_p, subln_g, lam_init):
    nb, nq, _ = q.shape
    n_pages = page_table.shape[1]
    assert n_pages % PAGES_PER_STEP == 0 and nq * H_A <= PAGE_ROWS and H_A == 1 << HEAD_SHIFT
    ngroups = n_pages // PAGES_PER_STEP
    past_len = n_pages * PAGE
    rows = 2 * H_A * nq

    def page_spec(i):
        return pl.BlockSpec((PAGE_ROWS, DV_A),
                            lambda b, g, pt: (layer * pool + pt[b, g * PAGES_PER_STEP + i], 0))

    per_b = lambda r, w: pl.BlockSpec((1, r, w), lambda b, g, pt: (b, 0, 0))
    grid_spec = pltpu.PrefetchScalarGridSpec(
        num_scalar_prefetch=1,
        grid=(nb, ngroups),
        in_specs=[per_b(nq, D_A), per_b(nq * H_A, DV_A), per_b(nq * H_A, DV_A),
                  pl.BlockSpec((4, DQK_A), lambda b, g, pt: (0, 0)),
                  pl.BlockSpec((1, DV_A), lambda b, g, pt: (0, 0))]
                 + [page_spec(i) for i in range(PAGES_PER_STEP)]
                 + [page_spec(i) for i in range(PAGES_PER_STEP)],
        out_specs=per_b(nq, D_A),
        scratch_shapes=[pltpu.VMEM((rows, DV_A), BF16), pltpu.VMEM((rows, LANES), F32), pltpu.VMEM((rows, LANES), F32),
                        pltpu.VMEM((rows, DV_A), F32), pltpu.VMEM((PAGE_ROWS, DV_A), F32),
                        pltpu.VMEM((PAGE_ROWS, DV_A), F32)],
    )
    return pl.pallas_call(
        functools.partial(_decode_kernel, nq=nq, past_len=past_len, lam_init=lam_init, ngroups=ngroups),
        grid_spec=grid_spec,
        out_shape=jax.ShapeDtypeStruct((nb, nq, D_A), BF16),
        compiler_params=_cparams(("arbitrary", "arbitrary")),
        name="attn_decode",
    )(page_table, q, k_new.reshape(nb, nq * H_A, DV_A), v_new.reshape(nb, nq * H_A, DV_A), lam_p,
      subln_g.reshape(1, DV_A), *([cache_k] * PAGES_PER_STEP), *([cache_v] * PAGES_PER_STEP))


def _gdn_step_kernel(x_ref, cb_ref, cw_ref, ba_ref, z_ref, s0_ref, hp_ref, ng_ref, o_ref, sn_ref, xx_ref, *, seq):
    width = 3 * D_B
    hist = SUBLANES
    xx_ref[0:hist, :] = jnp.zeros((hist, width), F32)
    xx_ref[hist - (CONV_W - 1):hist, :] = cb_ref[0]
    xx_ref[hist:hist + seq, :] = x_ref[0]
    cw = cw_ref[...]
    y = cw[CONV_W - 1:CONV_W] * xx_ref[hist:hist + seq, :]
    for i in range(CONV_W - 1):
        off = hist - (CONV_W - 1) + i
        y = y + cw[i:i + 1] * xx_ref[off:off + seq, :]
    y = _silu(y)

    def l2n(v):
        return v * lax.rsqrt(jnp.sum(v * v, axis=-1, keepdims=True) + 1e-6)

    ba = ba_ref[0]
    ng = ng_ref[...]
    for h in range(H_B):
        hc = slice(h * DK_B, (h + 1) * DK_B)
        q = l2n(y[:, h * DK_B:(h + 1) * DK_B]) * (DK_B ** -0.5)
        k = l2n(y[:, D_B + h * DK_B:D_B + (h + 1) * DK_B])
        v = y[:, 2 * D_B + h * DV_B:2 * D_B + (h + 1) * DV_B]
        beta = jax.nn.sigmoid(ba[:, h:h + 1])
        xsp = ba[:, H_B + h:H_B + h + 1] + hp_ref[h, 1:2, :]
        decay = jnp.exp(-jnp.exp(hp_ref[h, 0:1, :]) * (jnp.maximum(xsp, 0.0) + jnp.log1p(jnp.exp(-jnp.abs(xsp)))))
        kt = k.T
        qt = q.T
        bv = beta * v
        state = s0_ref[0, h]
        outs = []
        for t in range(seq):
            kcol = kt[:, t:t + 1]
            state = state * decay[t:t + 1, :]
            r = jnp.sum(kcol * state, axis=0, keepdims=True)
            state = state + kcol * (bv[t:t + 1, :] - beta[t:t + 1, :] * r)
            outs.append(jnp.sum(qt[:, t:t + 1] * state, axis=0, keepdims=True))
        o = jnp.concatenate(outs, axis=0)
        sn_ref[0, h] = state
        yo = o * lax.rsqrt(jnp.mean(o * o, axis=-1, keepdims=True) + 1e-6) * ng
        o_ref[0, :, hc] = (yo * _silu(z_ref[0, :, hc])).astype(o_ref.dtype)


def _gated_deltanet_step(qkvb, ba, z, conv_buf, s0, conv_w, head_params, norm_g):
    nb, seq, width = qkvb.shape
    row = lambda w: pl.BlockSpec((1, seq, w), lambda b: (b, 0, 0))
    st = pl.BlockSpec((1, H_B, DK_B, DV_B), lambda b: (b, 0, 0, 0))
    return pl.pallas_call(
        functools.partial(_gdn_step_kernel, seq=seq),
        grid=(nb,),
        in_specs=[row(width),
                  pl.BlockSpec((1, CONV_W - 1, width), lambda b: (b, 0, 0)),
                  pl.BlockSpec((CONV_W, width), lambda b: (0, 0)),
                  row(LANES), row(D_B), st,
                  pl.BlockSpec((H_B, 2, LANES), lambda b: (0, 0, 0)),
                  pl.BlockSpec((1, DV_B), lambda b: (0, 0))],
        out_specs=[row(D_B), st],
        out_shape=[jax.ShapeDtypeStruct((nb, seq, D_B), BF16),
                   jax.ShapeDtypeStruct((nb, H_B, DK_B, DV_B), F32)],
        scratch_shapes=[pltpu.VMEM((seq + SUBLANES, width), F32)],
        compiler_params=_cparams(("arbitrary",)),
        name="gated_deltanet_step",
    )(qkvb, conv_buf, conv_w, ba, z, s0, head_params, norm_g.reshape(1, DV_B))


GDN_TILE = 512
GDN_SUPER = 256


def _split(a):
    hi = a.astype(BF16)
    return hi, (a - hi.astype(F32)).astype(BF16)


def _dg(a, b, dims):
    return lax.dot_general(a, b, (dims, ((), ())), preferred_element_type=F32)


_NN = ((1,), (0,))
_NT = ((1,), (1,))
_TN = ((0,), (0,))


def _mm3s(asp, bsp, dims=_NN):
    (ah, al), (bh, bl) = asp, bsp
    return _dg(ah, bh, dims) + (_dg(ah, bl, dims) + _dg(al, bh, dims))


def _mm3(a, b, dims=_NN):
    return _mm3s(_split(a), _split(b), dims)


def _mm_exact_lhs(t, b):
    b1 = b.astype(BF16)
    r1 = b - b1.astype(F32)
    b2 = r1.astype(BF16)
    b3 = (r1 - b2.astype(F32)).astype(BF16)
    return _dg(t, b1, _NN) + (_dg(t, b2, _NN) + _dg(t, b3, _NN))


def _prod(a, b):
    return _dg(a.astype(BF16), b.astype(BF16), _NN)


def _unit_lower_inverse_minus_eye(lmat, size, csz):
    r = lax.broadcasted_iota(jnp.int32, (size, size), 0)
    c = lax.broadcasted_iota(jnp.int32, (size, size), 1)
    ld = jnp.where((r // SUBLANES) == (c // SUBLANES), lmat, 0.0)
    p2 = _prod(ld, ld)
    d = (p2 - ld) - _prod(ld, p2)
    p4 = _prod(p2, p2)
    d = d + p4 + _prod(d, p4)
    e = lmat - ld
    n = e + _prod(d, e)
    out = -n
    pw = n
    k = 2
    while k < csz // SUBLANES:
        pw = _prod(pw, pw)
        out = out + pw + _prod(out, pw)
        k *= 2
    return out + d + _prod(out, d)


def _gdn_long_kernel(x_ref, cb_ref, cw_ref, ba_ref, z_ref, s0_ref, hp_ref, ng_ref, o_ref, sn_ref,
                     xx_ref, y_ref, st_ref, u_ref, w_ref, qd_ref, kd_ref, gt_ref, qk_ref, *, tl):
    t = pl.program_id(1)
    width = 3 * D_B
    hist = SUBLANES

    @pl.when(t == 0)
    def _():
        xx_ref[0:hist, :] = jnp.zeros((hist, width), F32)
        xx_ref[hist - (CONV_W - 1):hist, :] = cb_ref[0]
        st_ref[...] = s0_ref[0]

    @pl.when(t > 0)
    def _():
        xx_ref[0:hist, :] = xx_ref[tl:tl + hist, :]

    xx_ref[hist:hist + tl, :] = x_ref[0]

    cw = cw_ref[...]
    blk = 256
    for r0 in range(0, tl, blk):
        y = cw[CONV_W - 1:CONV_W] * xx_ref[hist + r0:hist + r0 + blk, :]
        for i in range(CONV_W - 1):
            off = hist - (CONV_W - 1) + i + r0
            y = y + cw[i:i + 1] * xx_ref[off:off + blk, :]
        y_ref[r0:r0 + blk, :] = _silu(y)

    sup = GDN_SUPER
    r = lax.broadcasted_iota(jnp.int32, (sup, sup), 0)
    c = lax.broadcasted_iota(jnp.int32, (sup, sup), 1)
    same_chunk = (r // CHUNK) == (c // CHUNK)
    tril = same_chunk & (r >= c)
    strict = same_chunk & (r > c)
    diag = r == c
    sum_mat = jnp.concatenate([tril.astype(BF16), same_chunk.astype(BF16)], axis=0)

    def l2n(v):
        return v * lax.rsqrt(jnp.sum(v * v, axis=-1, keepdims=True) + 1e-6)

    for sc in range(tl // sup):
        rs = slice(sc * sup, (sc + 1) * sup)
        ba = ba_ref[0, rs, :]
        for h in range(H_B):
            hc = slice(h * DK_B, (h + 1) * DK_B)
            q = l2n(y_ref[rs, h * DK_B:(h + 1) * DK_B]) * (DK_B ** -0.5)
            k = l2n(y_ref[rs, D_B + h * DK_B:D_B + (h + 1) * DK_B])
            v = y_ref[rs, 2 * D_B + h * DV_B:2 * D_B + (h + 1) * DV_B]
            beta = jax.nn.sigmoid(ba[:, h:h + 1])
            xsp = ba[:, H_B + h:H_B + h + 1] + hp_ref[h, 1:2, :]
            gl = -jnp.exp(hp_ref[h, 0:1, :]) * (jnp.maximum(xsp, 0.0) + jnp.log1p(jnp.exp(-jnp.abs(xsp))))
            sums = _mm_exact_lhs(sum_mat, gl)
            gcum, gtot = sums[0:sup], sums[sup:2 * sup]
            gi = jnp.concatenate([gcum] * (sup // LANES), axis=1)
            gj = jnp.sum(jnp.where(diag, gi, 0.0), axis=0, keepdims=True)
            decay = jnp.where(tril, jnp.exp(jnp.where(tril, gi - gj, 0.0)), 0.0)
            eg = jnp.exp(gcum)
            kb = k * beta
            kbf = k.astype(BF16)
            lmat = jnp.where(strict, _dg(kb.astype(BF16), kbf, _NT) * decay, 0.0)
            rhs = jnp.concatenate([v * beta, kb * eg], axis=1)
            uw = rhs + _mm3(_unit_lower_inverse_minus_eye(lmat, sup, CHUNK), rhs)
            u_ref[rs, hc] = uw[:, 0:DV_B]
            w_ref[rs, hc] = uw[:, DV_B:DV_B + DK_B]
            qk = _dg(q.astype(BF16), kbf, _NT) * decay
            qkc = qk[:, 0:CHUNK]
            for i in range(1, sup // CHUNK):
                qkc = qkc + qk[:, i * CHUNK:(i + 1) * CHUNK]
            qk_ref[h, rs, :] = qkc
            qd_ref[rs, hc] = q * eg
            kd_ref[rs, hc] = k * jnp.exp(gtot - gcum)
            gt_ref[rs, hc] = jnp.exp(gtot)

    ng = ng_ref[...]

    def chunk(ci, carry):
        start = ci * CHUNK
        rows = pl.ds(start, CHUNK)
        for h in range(H_B):
            hc = slice(h * DK_B, (h + 1) * DK_B)
            ssp = _split(st_ref[h])
            v_new = u_ref[rows, hc] - _mm3s(_split(w_ref[rows, hc]), ssp)
            vsp = _split(v_new)
            o = _dg(qd_ref[rows, hc].astype(BF16), ssp[0], _NN) + _dg(qk_ref[h, rows, :].astype(BF16), vsp[0], _NN)
            st_ref[h] = st_ref[h] * gt_ref[pl.ds(start, 1), hc] + _mm3s(_split(kd_ref[rows, hc]), vsp, _TN)
            y = o * lax.rsqrt(jnp.mean(o * o, axis=-1, keepdims=True) + 1e-6) * ng
            o_ref[0, rows, hc] = (y * _silu(z_ref[0, rows, hc])).astype(o_ref.dtype)
        return carry

    for ci in range(tl // CHUNK):
        chunk(ci, 0)
    sn_ref[0] = st_ref[...]


def _gated_deltanet_long(qkvb, ba, z, conv_buf, s0, conv_w, head_params, norm_g):
    nb, seq, width = qkvb.shape
    tl = GDN_TILE
    assert seq % tl == 0 and tl % GDN_SUPER == 0
    row = lambda w: pl.BlockSpec((1, tl, w), lambda b, t: (b, t, 0))
    st = pl.BlockSpec((1, H_B, DK_B, DV_B), lambda b, t: (b, 0, 0, 0))
    scr = lambda n, w: pltpu.VMEM((n, w), F32)
    return pl.pallas_call(
        functools.partial(_gdn_long_kernel, tl=tl),
        grid=(nb, seq // tl),
        in_specs=[row(width),
                  pl.BlockSpec((1, CONV_W - 1, width), lambda b, t: (b, 0, 0)),
                  pl.BlockSpec((CONV_W, width), lambda b, t: (0, 0)),
                  row(LANES), row(D_B), st,
                  pl.BlockSpec((H_B, 2, LANES), lambda b, t: (0, 0, 0)),
                  pl.BlockSpec((1, DV_B), lambda b, t: (0, 0))],
        out_specs=[row(D_B), st],
        out_shape=[jax.ShapeDtypeStruct((nb, seq, D_B), BF16),
                   jax.ShapeDtypeStruct((nb, H_B, DK_B, DV_B), F32)],
        scratch_shapes=[scr(tl + SUBLANES, width), scr(tl, width), pltpu.VMEM((H_B, DK_B, DV_B), F32),
                        scr(tl, D_B), scr(tl, D_B), scr(tl, D_B), scr(tl, D_B), scr(tl, D_B),
                        pltpu.VMEM((H_B, tl, CHUNK), F32)],
        compiler_params=_cparams(("arbitrary", "arbitrary")),
        name="gated_deltanet_long",
    )(qkvb, conv_buf, conv_w, ba, z, s0, head_params, norm_g.reshape(1, DV_B))


def _outproj_kernel(oa_ref, ob_ref, x_ref, w_ref, g1_ref, lg_ref, lb_ref, sh_ref, sc_ref, *rest, tb, tl, alpha, region):
    router = region is not None
    if router:
        wr_ref, br_ref, x1_ref, h2_ref, pair_ref, cnt_ref, run_ref = rest
    else:
        x1_ref, h2_ref = rest
    rows = tb * tl
    oa = oa_ref[...].reshape(rows, D_A)
    ob = ob_ref[...].reshape(rows, D_B)
    mix = (jnp.dot(oa, w_ref[0:D_A, :], preferred_element_type=F32)
           + jnp.dot(ob, w_ref[D_A:D_A + D_B, :], preferred_element_type=F32))
    y = alpha * x_ref[...] + g1_ref[...] * mix.reshape(tb, tl, D_MODEL)
    x1 = _ln(y) * lg_ref[...] + lb_ref[...]
    x1_ref[...] = x1
    h2 = (_ln(x1) * (1.0 + sc_ref[...]) + sh_ref[...]).astype(BF16)
    h2_ref[...] = h2
    if router:
        @pl.when((pl.program_id(0) == 0) & (pl.program_id(1) == 0))
        def _():
            run_ref[...] = jnp.zeros_like(run_ref)

        logits = jnp.dot(h2.reshape(rows, D_MODEL), wr_ref[...], preferred_element_type=F32) + br_ref[...]
        lane = lax.broadcasted_iota(jnp.int32, (rows, LANES), 1)
        logits = jnp.where(lane < N_EXPERTS, logits, NEG_INF)
        v1 = jnp.max(logits, axis=-1, keepdims=True)
        i1 = jnp.min(jnp.where(logits == v1, lane, LANES), axis=-1, keepdims=True)
        rest_l = jnp.where(lane == i1, NEG_INF, logits)
        v2 = jnp.max(rest_l, axis=-1, keepdims=True)
        i2 = jnp.min(jnp.where(rest_l == v2, lane, LANES), axis=-1, keepdims=True)
        e2 = jnp.exp(v2 - v1)
        gate1 = 1.0 / (1.0 + e2)
        gate2 = e2 / (1.0 + e2)
        sel = jnp.where((lane == i1) | (lane == i2), 1.0, 0.0)
        r = lax.broadcasted_iota(jnp.int32, (rows, rows), 0)
        c = lax.broadcasted_iota(jnp.int32, (rows, rows), 1)
        earlier = jnp.where(r > c, 1.0, 0.0).astype(BF16)
        rank = jnp.dot(earlier, sel.astype(BF16), preferred_element_type=F32) + run_ref[...]
        run_ref[...] = run_ref[...] + jnp.sum(sel, axis=0, keepdims=True)
        slot = lane.astype(F32) * float(region) + rank
        first, second = jnp.minimum(i1, i2), jnp.maximum(i1, i2)
        slot_a = jnp.sum(jnp.where(lane == first, slot, 0.0), axis=-1, keepdims=True)
        slot_b = jnp.sum(jnp.where(lane == second, slot, 0.0), axis=-1, keepdims=True)
        gate_a = jnp.where(first == i1, gate1, gate2)
        gate_b = jnp.where(first == i1, gate2, gate1)
        pair = (jnp.where(lane == 0, slot_a, 0.0) + jnp.where(lane == 1, slot_b, 0.0)
                + jnp.where(lane == 2, gate_a, 0.0) + jnp.where(lane == 3, gate_b, 0.0))
        pair_ref[...] = pair.reshape(tb, tl, LANES)
        cnt_ref[...] = run_ref[...]


def _out_proj(oa, ob, x, w_out_b, g1, ln_g, ln_b, sh2, sc2, alpha, tm, router_w=None, router_b=None, region=None):
    nb, seq, _ = x.shape
    tb, tl = _row_tile(nb, seq, tm)
    grid = (nb // tb, seq // tl)
    row = lambda w: pl.BlockSpec((tb, tl, w), lambda i, j: (i, j, 0))
    mod = pl.BlockSpec((tb, 1, D_MODEL), lambda i, j: (i, 0, 0))
    vec = pl.BlockSpec((1, 1, D_MODEL), lambda i, j: (0, 0, 0))
    in_specs = [row(D_A), row(D_B), row(D_MODEL), pl.BlockSpec((D_MODEL, D_MODEL), lambda i, j: (0, 0)),
                mod, vec, vec, mod, mod]
    args = [oa, ob, x, w_out_b, g1, ln_g.reshape(1, 1, D_MODEL), ln_b.reshape(1, 1, D_MODEL), sh2, sc2]
    out_specs = [row(D_MODEL), row(D_MODEL)]
    out_shape = [jax.ShapeDtypeStruct((nb, seq, D_MODEL), F32), jax.ShapeDtypeStruct((nb, seq, D_MODEL), BF16)]
    scratch = []
    if region is not None:
        in_specs += [pl.BlockSpec((D_MODEL, LANES), lambda i, j: (0, 0)), pl.BlockSpec((1, LANES), lambda i, j: (0, 0))]
        args += [router_w, router_b]
        out_specs += [row(LANES), pl.BlockSpec((1, LANES), lambda i, j: (0, 0))]
        out_shape += [jax.ShapeDtypeStruct((nb, seq, LANES), F32), jax.ShapeDtypeStruct((1, LANES), F32)]
        scratch = [pltpu.VMEM((1, LANES), F32)]
    return pl.pallas_call(
        functools.partial(_outproj_kernel, tb=tb, tl=tl, alpha=alpha, region=region),
        grid=grid, in_specs=in_specs, out_specs=out_specs, out_shape=out_shape, scratch_shapes=scratch,
        compiler_params=_cparams(("arbitrary", "arbitrary")),
        name="out_proj",
    )(*args)


def _ffn_kernel(h_ref, x_ref, wg_ref, wu_ref, wd_ref, g2_ref, lg_ref, lb_ref, o_ref, acc_ref, *, tb, tl, alpha):
    f = pl.program_id(2)
    rows = tb * tl

    @pl.when(f == 0)
    def _():
        acc_ref[...] = jnp.zeros_like(acc_ref)

    hb = h_ref[...].reshape(rows, D_MODEL)
    gt = jnp.dot(hb, wg_ref[...], preferred_element_type=F32)
    up = jnp.dot(hb, wu_ref[...], preferred_element_type=F32)
    a = (_silu(gt) * up).astype(BF16)
    acc_ref[...] += jnp.dot(a, wd_ref[...], preferred_element_type=F32)

    @pl.when(f == pl.num_programs(2) - 1)
    def _():
        y = alpha * x_ref[...] + g2_ref[...] * acc_ref[...].reshape(tb, tl, D_MODEL)
        o_ref[...] = _ln(y) * lg_ref[...] + lb_ref[...]


def _ffn(h2, x1, w_up_b, w_down_b, g2, ln_g, ln_b, alpha, tm, tf):
    nb, seq, _ = x1.shape
    tb, tl = _row_tile(nb, seq, tm)
    nf = D_FF // tf
    grid = (nb // tb, seq // tl, nf)
    row = lambda w: pl.BlockSpec((tb, tl, w), lambda i, j, f: (i, j, 0))
    return pl.pallas_call(
        functools.partial(_ffn_kernel, tb=tb, tl=tl, alpha=alpha),
        grid=grid,
        in_specs=[row(D_MODEL), row(D_MODEL),
                  pl.BlockSpec((D_MODEL, tf), lambda i, j, f: (0, f)),
                  pl.BlockSpec((D_MODEL, tf), lambda i, j, f: (0, nf + f)),
                  pl.BlockSpec((tf, D_MODEL), lambda i, j, f: (f, 0)),
                  pl.BlockSpec((tb, 1, D_MODEL), lambda i, j, f: (i, 0, 0)),
                  pl.BlockSpec((1, 1, D_MODEL), lambda i, j, f: (0, 0, 0)),
                  pl.BlockSpec((1, 1, D_MODEL), lambda i, j, f: (0, 0, 0))],
        out_specs=row(D_MODEL),
        out_shape=jax.ShapeDtypeStruct((nb, seq, D_MODEL), F32),
        scratch_shapes=[pltpu.VMEM((tb * tl, D_MODEL), F32)],
        compiler_params=_cparams(("arbitrary", "arbitrary", "arbitrary")),
        name="ffn",
    )(h2, x1, w_up_b, w_up_b, w_down_b, g2, ln_g.reshape(1, 1, D_MODEL), ln_b.reshape(1, 1, D_MODEL))


MOE_ROWS = 512
ROW_TILE = (SUBLANES, LANES)
ISSUE_UNROLL = 8


def _region_rows(m):
    return -(-m // MOE_ROWS) * MOE_ROWS


def _dispatch_kernel(slot_ref, pad_ref, h_ref, xs_ref, src_ref, zero_ref, sem, *, rows, npad):
    @pl.when((pl.program_id(0) == 0) & (pl.program_id(1) == 0))
    def _():
        zero_ref[...] = jnp.zeros_like(zero_ref)

        def pad_copy(k):
            return pltpu.make_async_copy(zero_ref.at[0], xs_ref.at[pad_ref[0, k]], sem)

        def issue_pad(k, c):
            pad_copy(k).start()
            return c

        def drain_pad(k, c):
            pad_copy(0).wait()
            return c

        lax.fori_loop(0, npad, issue_pad, 0)
        lax.fori_loop(0, npad, drain_pad, 0)

    src_ref[...] = h_ref[...].reshape(rows, D_MODEL).astype(F32).reshape(rows, *ROW_TILE)

    def row_copy(t, k):
        return pltpu.make_async_copy(src_ref.at[t], xs_ref.at[slot_ref[k, t]], sem)

    def issue(i, c):
        for u in range(ISSUE_UNROLL):
            t = i * ISSUE_UNROLL + u
            row_copy(t, 0).start()
            row_copy(t, 1).start()
        return c

    lax.fori_loop(0, rows // ISSUE_UNROLL, issue, 0)

    def drain(i, c):
        row_copy(0, 0).wait()
        row_copy(0, 1).wait()
        return c

    lax.fori_loop(0, rows, drain, 0)


def _dispatch(h2, slots, pad_slots, total_rows):
    nb, seq, _ = h2.shape
    tb, tl = _row_tile(nb, seq, MOE_ROWS)
    rows = tb * tl
    nj = seq // tl
    npad = pad_slots.shape[1]
    return pl.pallas_call(
        functools.partial(_dispatch_kernel, rows=rows, npad=npad),
        grid=(nb // tb, nj),
        in_specs=[pl.BlockSpec((2, rows), lambda i, j: (0, i * nj + j), memory_space=pltpu.SMEM),
                  pl.BlockSpec((1, npad), lambda i, j: (0, 0), memory_space=pltpu.SMEM),
                  pl.BlockSpec((tb, tl, D_MODEL), lambda i, j: (i, j, 0))],
        out_specs=pl.BlockSpec(memory_space=pl.ANY),
        out_shape=jax.ShapeDtypeStruct((total_rows, *ROW_TILE), F32),
        scratch_shapes=[pltpu.VMEM((rows, *ROW_TILE), F32), pltpu.VMEM((1, *ROW_TILE), F32),
                        pltpu.SemaphoreType.DMA(())],
        compiler_params=_cparams(("arbitrary", "arbitrary")),
        name="moe_dispatch",
    )(slots, pad_slots, h2)


def _experts_kernel(blk_e_ref, blk_rows_ref, nused_ref, x_ref, wg_ref, wu_ref, wd_ref, o_ref, xb_ref, acc_ref):
    i, f = pl.program_id(0), pl.program_id(1)
    nf = pl.num_programs(1)
    nrows = blk_rows_ref[i]
    active = nrows > 0

    @pl.when(active & (f == 0))
    def _():
        x = x_ref[...].reshape(MOE_ROWS, D_MODEL)
        row = lax.broadcasted_iota(jnp.int32, (MOE_ROWS, 1), 0)
        xb_ref[...] = jnp.where(row < nrows, x, 0.0).astype(BF16)
        acc_ref[...] = jnp.zeros_like(acc_ref)

    @pl.when(active)
    def _():
        xb = xb_ref[...]
        gt = jnp.dot(xb, wg_ref[0], preferred_element_type=F32)
        up = jnp.dot(xb, wu_ref[0], preferred_element_type=F32)
        a = (_silu(gt) * up).astype(BF16)
        acc_ref[...] += jnp.dot(a, wd_ref[0], preferred_element_type=F32)

    @pl.when(active & (f == nf - 1))
    def _():
        o_ref[...] = acc_ref[...].reshape(MOE_ROWS, *ROW_TILE)

    @pl.when(jnp.logical_not(active) & (f == nf - 1))
    def _():
        o_ref[...] = jnp.zeros_like(o_ref)


def _experts(xs, blk_e, blk_rows, nused, w_up_b, w_down_b, tf):
    nblk = xs.shape[0] // MOE_ROWS
    nf = D_FF // tf

    def xi(i, f, be, br, nu):
        return jnp.minimum(i, nu[0] - 1)

    def ei(i, f, be, br, nu):
        return be[jnp.minimum(i, nu[0] - 1)]

    def fi(i, f, be, br, nu):
        return jnp.where(br[i] > 0, f, nf - 1)

    grid_spec = pltpu.PrefetchScalarGridSpec(
        num_scalar_prefetch=3,
        grid=(nblk, nf),
        in_specs=[pl.BlockSpec((MOE_ROWS, *ROW_TILE), lambda *a: (xi(*a), 0, 0)),
                  pl.BlockSpec((1, D_MODEL, tf), lambda *a: (ei(*a), 0, fi(*a))),
                  pl.BlockSpec((1, D_MODEL, tf), lambda *a: (ei(*a), 0, nf + fi(*a))),
                  pl.BlockSpec((1, tf, D_MODEL), lambda *a: (ei(*a), fi(*a), 0))],
        out_specs=pl.BlockSpec((MOE_ROWS, *ROW_TILE), lambda i, f, be, br, nu: (i, 0, 0)),
        scratch_shapes=[pltpu.VMEM((MOE_ROWS, D_MODEL), BF16), pltpu.VMEM((MOE_ROWS, D_MODEL), F32)],
    )
    return pl.pallas_call(
        _experts_kernel, grid_spec=grid_spec,
        out_shape=jax.ShapeDtypeStruct(xs.shape, F32),
        compiler_params=_cparams(("arbitrary", "arbitrary")),
        name="moe_experts",
    )(blk_e, blk_rows, nused, xs, w_up_b, w_up_b, w_down_b)


def _moe_layout(pair, cnt, m, region):
    raw = pair[..., 0:2].reshape(m, 2).astype(jnp.int32)
    e = raw // region
    counts = cnt[0, :N_EXPERTS].astype(jnp.int32)
    padded = (counts + MOE_ROWS - 1) // MOE_ROWS * MOE_ROWS
    off = jnp.cumsum(padded) - padded
    slots = (off[e] + (raw - e * region)).T
    nblk = (2 * m + MOE_ROWS - 1) // MOE_ROWS + N_EXPERTS
    bstart = jnp.arange(nblk, dtype=jnp.int32) * MOE_ROWS
    blk_e = jnp.minimum(jnp.sum((bstart[:, None] >= (off + padded)[None, :]).astype(jnp.int32), axis=1), N_EXPERTS - 1)
    blk_rows = jnp.clip(counts[blk_e] - (bstart - off[blk_e]), 0, MOE_ROWS)
    nused = (jnp.sum(padded) // MOE_ROWS).reshape(1)
    pos = jnp.arange(nblk * MOE_ROWS, dtype=jnp.int32)
    unwritten = (pos % MOE_ROWS) >= blk_rows[pos // MOE_ROWS]
    pad_slots = jnp.nonzero(unwritten, size=nblk * MOE_ROWS - 2 * m, fill_value=0)[0].astype(jnp.int32).reshape(1, -1)
    return slots, pad_slots, blk_e, blk_rows, nused


def _combine_kernel(slot_ref, ys_ref, pair_ref, x_ref, g2_ref, lg_ref, lb_ref, o_ref, buf0, buf1, sem, *, tb, tl, alpha):
    rows = tb * tl

    def row_copy(t, k):
        return pltpu.make_async_copy(ys_ref.at[slot_ref[k, t]], (buf0 if k == 0 else buf1).at[t], sem)

    def issue(i, c):
        for u in range(ISSUE_UNROLL):
            t = i * ISSUE_UNROLL + u
            row_copy(t, 0).start()
            row_copy(t, 1).start()
        return c

    lax.fori_loop(0, rows // ISSUE_UNROLL, issue, 0)

    def drain(i, c):
        row_copy(0, 0).wait()
        row_copy(0, 1).wait()
        return c

    lax.fori_loop(0, rows, drain, 0)
    pair = pair_ref[...].reshape(rows, LANES)
    ga, gb = pair[:, 2:3], pair[:, 3:4]
    f = ga * buf0[...].reshape(rows, D_MODEL) + gb * buf1[...].reshape(rows, D_MODEL)
    y = alpha * x_ref[...] + g2_ref[...] * f.reshape(tb, tl, D_MODEL)
    o_ref[...] = _ln(y) * lg_ref[...] + lb_ref[...]


def _combine(ys, slots, pair, x1, g2, ln_g, ln_b, alpha):
    nb, seq, _ = x1.shape
    tb, tl = _row_tile(nb, seq, MOE_ROWS)
    rows = tb * tl
    nj = seq // tl
    row = lambda w: pl.BlockSpec((tb, tl, w), lambda i, j: (i, j, 0))
    return pl.pallas_call(
        functools.partial(_combine_kernel, tb=tb, tl=tl, alpha=alpha),
        grid=(nb // tb, nj),
        in_specs=[pl.BlockSpec((2, rows), lambda i, j: (0, i * nj + j), memory_space=pltpu.SMEM),
                  pl.BlockSpec(memory_space=pl.ANY), row(LANES), row(D_MODEL),
                  pl.BlockSpec((tb, 1, D_MODEL), lambda i, j: (i, 0, 0)),
                  pl.BlockSpec((1, 1, D_MODEL), lambda i, j: (0, 0, 0)),
                  pl.BlockSpec((1, 1, D_MODEL), lambda i, j: (0, 0, 0))],
        out_specs=row(D_MODEL),
        out_shape=jax.ShapeDtypeStruct((nb, seq, D_MODEL), F32),
        scratch_shapes=[pltpu.VMEM((rows, *ROW_TILE), F32), pltpu.VMEM((rows, *ROW_TILE), F32),
                        pltpu.SemaphoreType.DMA(())],
        compiler_params=_cparams(("arbitrary", "arbitrary")),
        name="moe_combine",
    )(slots, ys, pair, x1, g2, ln_g.reshape(1, 1, D_MODEL), ln_b.reshape(1, 1, D_MODEL))


def _trunk(x, mod, conv_bufs, delta_states, past, weights, tm):
    depth = weights["w_in"].shape[0]
    alpha = (2 * depth) ** 0.25
    nb, seq, _ = x.shape
    new_k, new_v, new_conv, new_delta = [], [], [], []
    slopes = jnp.broadcast_to(
        jnp.asarray([2.0 ** (-8.0 * (h + 1) / H_A) for h in range(H_A)], F32)[:, None, None], (H_A, 1, LANES))
    for l in range(depth):
        sh1, sc1, g1, sh2, sc2, g2 = [mod[l, s] for s in range(6)]
        lam_init = 0.8 - 0.6 * math.exp(-0.3 * l)
        q, k, v, qkvb, z, ba = _in_proj(x, sh1, sc1, weights["w_in"][l], tm)
        new_k.append(k.reshape(nb, seq, H_A, 2 * DQK_A))
        new_v.append(v.reshape(nb, seq, H_A, DV_A))
        if past is None:
            o_a = _attn_prompt(q, k, v, slopes, weights["lam_params"][l], weights["subln_a"][l], lam_init, 512)
        else:
            cache_k, cache_v, pool, page_table = past
            o_a = _attn_decode(q, k, v, cache_k, cache_v, l, pool, page_table,
                               weights["lam_params"][l], weights["subln_a"][l], lam_init)
        gdn = _gated_deltanet_long if seq % GDN_TILE == 0 else _gated_deltanet_step
        o_b, s_new = gdn(qkvb, ba, z, conv_bufs[l], delta_states[l], weights["conv_w"][l],
                         weights["head_params"][l], weights["norm_b"][l])
        new_conv.append(qkvb[:, seq - (CONV_W - 1):, :] if seq >= CONV_W - 1 else
                        jnp.concatenate([conv_bufs[l], qkvb], axis=1)[:, -(CONV_W - 1):, :])
        new_delta.append(s_new)
        j = l // 2
        if l % 2 == 0:
            x1, h2 = _out_proj(o_a, o_b, x, weights["w_out"][l], g1, weights["ln1_g"][l], weights["ln1_b"][l],
                               sh2, sc2, alpha, tm)
            x = _ffn(h2, x1, weights["dense_w_up"][j], weights["dense_w_down"][j], g2,
                     weights["ln2_g"][l], weights["ln2_b"][l], alpha, tm, D_FF // 2)
        else:
            region = _region_rows(nb * seq)
            x1, h2, pair, cnt = _out_proj(o_a, o_b, x, weights["w_out"][l], g1, weights["ln1_g"][l],
                                          weights["ln1_b"][l], sh2, sc2, alpha, MOE_ROWS,
                                          weights["router_w"][j], weights["router_b"][j], region)
            slots, pad_slots, blk_e, blk_rows, nused = _moe_layout(pair, cnt, nb * seq, region)
            xs = _dispatch(h2, slots, pad_slots, blk_e.shape[0] * MOE_ROWS)
            ys = _experts(xs, blk_e, blk_rows, nused, weights["moe_w_up"][j], weights["moe_w_down"][j], D_FF // 2)
            x = _combine(ys, slots, pair, x1, g2, weights["ln2_g"][l], weights["ln2_b"][l], alpha)
    return x, jnp.stack(new_k), jnp.stack(new_v), jnp.stack(new_conv), jnp.stack(new_delta)


def kernel(x_prompt, x_sample, cache_k, cache_v, state_conv, state_delta, page_table, c_prompt, c_sample,
           w_in, w_out, lam_params, subln_a, conv_w, a_log, dt_bias, norm_b, ada_w, ada_b,
           ln1_g, ln1_b, ln2_g, ln2_b, dense_w_up, dense_w_down, moe_router, moe_router_b, moe_w_up, moe_w_down):
    depth = w_in.shape[0]
    nbp, nbs = x_prompt.shape[0], x_sample.shape[0]
    a_q, a_k, a_v, b_qkv = 2 * H_A * DQK_A, 2 * H_A * DQK_A, H_A * DV_A, 3 * D_B
    n0 = a_q + a_k + a_v + b_qkv
    w_perm = jnp.concatenate(
        [w_in[:, :, :n0], w_in[:, :, n0 + 2 * H_B:], w_in[:, :, n0:n0 + 2 * H_B],
         jnp.zeros((depth, D_MODEL, LANES - 2 * H_B), w_in.dtype)], axis=-1).astype(BF16)
    n_moe = moe_router.shape[0]
    router_w = jnp.concatenate([moe_router, jnp.zeros((n_moe, D_MODEL, LANES - N_EXPERTS), moe_router.dtype)],
                               axis=-1).astype(BF16)
    router_b = jnp.concatenate([moe_router_b, jnp.zeros((n_moe, LANES - N_EXPERTS), moe_router_b.dtype)],
                               axis=-1).reshape(n_moe, 1, LANES)
    head_params = jnp.broadcast_to(jnp.stack([a_log, dt_bias], axis=-1)[..., None], (depth, H_B, 2, LANES))
    weights = dict(w_in=w_perm, w_out=w_out.astype(BF16), lam_params=lam_params, subln_a=subln_a, conv_w=conv_w,
                   head_params=head_params, norm_b=norm_b, ln1_g=ln1_g, ln1_b=ln1_b, ln2_g=ln2_g, ln2_b=ln2_b,
                   dense_w_up=dense_w_up.astype(BF16), dense_w_down=dense_w_down.astype(BF16),
                   router_w=router_w, router_b=router_b,
                   moe_w_up=moe_w_up.astype(BF16), moe_w_down=moe_w_down.astype(BF16))

    mod = _ada_mod(jnp.concatenate([c_prompt, c_sample], axis=0), ada_w, ada_b)
    mod = mod[:, :, :, None, :]
    conv0 = jnp.zeros((depth, nbp, CONV_W - 1, 3 * D_B), x_prompt.dtype)
    delta0 = jnp.zeros((depth, nbp, H_B, DK_B, DV_B), x_prompt.dtype)
    y_p, k_p, v_p, conv_p, delta_p = _trunk(x_prompt, mod[:, :, :nbp], conv0, delta0, None, weights, 512)

    n_pool = cache_k.shape[1]
    ck = cache_k.reshape(depth * n_pool * PAGE * H_A, 2 * DQK_A)
    cv = cache_v.reshape(depth * n_pool * PAGE * H_A, DV_A)
    y_s, k_s, v_s, conv_s, delta_s = _trunk(x_sample, mod[:, :, nbp:], state_conv, state_delta,
                                            (ck, cv, n_pool, page_table), weights, 512)
    return (y_p, y_s, k_p, v_p, conv_p, delta_p, k_s, v_s, conv_s, delta_s)
```

```python
import functools
import math

import jax
import jax.numpy as jnp
from jax import lax
from jax.experimental import pallas as pl
from jax.experimental.pallas import tpu as pltpu

F32 = jnp.float32
BF16 = jnp.bfloat16

D_MODEL = 1024
H_A = 4
DV_A = 128
DQK_A = 64
D_A = H_A * DV_A
H_B = 4
DK_B = 128
DV_B = 128
D_B = H_B * DV_B
CONV_W = 4
CHUNK = 64
D_FF = 2816
N_EXPERTS = 8
LN_EPS = 1e-5
PAGE = 128
LANES = 128
SUBLANES = 8
VMEM_LIMIT = 56 * 1024 * 1024
NEG_INF = float("-inf")

IN_SEGS = (("q", 0, 512), ("k", 512, 1024), ("v", 1024, 1536), ("qkvb", 1536, 3072),
           ("z", 3072, 3584), ("ba", 3584, 3712))
D_IN_PAD = 3712


def _cparams(sem):
    return pltpu.CompilerParams(dimension_semantics=sem, vmem_limit_bytes=VMEM_LIMIT)


def _ln(x):
    mu = jnp.mean(x, axis=-1, keepdims=True)
    xc = x - mu
    var = jnp.mean(xc * xc, axis=-1, keepdims=True)
    return xc * lax.rsqrt(var + LN_EPS)


def _silu(x):
    return x * jax.nn.sigmoid(x)


def _row_tile(nb, seq, target):
    if seq >= target:
        assert seq % target == 0
        return 1, target
    tb = max(1, min(nb, target // seq))
    while nb % tb:
        tb -= 1
    return tb, seq


def _ada_kernel(c_ref, w_ref, b_ref, o_ref):
    a = _silu(c_ref[...]).astype(BF16)
    o_ref[0, 0] = jnp.dot(a, w_ref[0].astype(BF16), preferred_element_type=F32) + b_ref[0, 0]


def _ada_mod(c_all, ada_w, ada_b):
    depth = ada_w.shape[0]
    nb = c_all.shape[0]
    return pl.pallas_call(
        _ada_kernel,
        grid=(depth, 6),
        in_specs=[pl.BlockSpec((nb, D_MODEL), lambda l, s: (0, 0)),
                  pl.BlockSpec((1, D_MODEL, D_MODEL), lambda l, s: (l, 0, s)),
                  pl.BlockSpec((1, 1, 1, D_MODEL), lambda l, s: (l, s, 0, 0))],
        out_specs=pl.BlockSpec((1, 1, nb, D_MODEL), lambda l, s: (l, s, 0, 0)),
        out_shape=jax.ShapeDtypeStruct((depth, 6, nb, D_MODEL), F32),
        compiler_params=_cparams(("arbitrary", "arbitrary")),
        name="ada_mod",
    )(c_all, ada_w, ada_b.reshape(depth, 6, 1, D_MODEL))


def _inproj_kernel(x_ref, sh_ref, sc_ref, w_ref, q_ref, k_ref, v_ref, qkvb_ref, z_ref, ba_ref, kh_ref, vh_ref, *, tb, tl):
    h = _ln(x_ref[...]) * (1.0 + sc_ref[...]) + sh_ref[...]
    hb = h.reshape(tb * tl, D_MODEL).astype(BF16)
    outs = dict(q=q_ref, k=k_ref, v=v_ref, qkvb=qkvb_ref, z=z_ref, ba=ba_ref)
    heads = dict(k=kh_ref, v=vh_ref)
    for name, lo, hi in IN_SEGS:
        r = jnp.dot(hb, w_ref[:, lo:hi], preferred_element_type=F32)
        outs[name][...] = r.reshape(tb, tl, hi - lo).astype(outs[name].dtype)
        if name in heads:
            heads[name][...] = r.reshape(tb, tl, H_A, DV_A)


def _in_proj(x, sh, sc, w_perm, tm):
    nb, seq, _ = x.shape
    tb, tl = _row_tile(nb, seq, tm)
    grid = (nb // tb, seq // tl)
    row = lambda w: pl.BlockSpec((tb, tl, w), lambda i, j: (i, j, 0))
    per_head = pl.BlockSpec((tb, tl, H_A, DV_A), lambda i, j: (i, j, 0, 0))
    mod = pl.BlockSpec((tb, 1, D_MODEL), lambda i, j: (i, 0, 0))
    widths = [hi - lo for _, lo, hi in IN_SEGS]
    dtypes = [BF16, F32, F32, F32, F32, F32]
    return pl.pallas_call(
        functools.partial(_inproj_kernel, tb=tb, tl=tl),
        grid=grid,
        in_specs=[row(D_MODEL), mod, mod, pl.BlockSpec((D_MODEL, D_IN_PAD), lambda i, j: (0, 0))],
        out_specs=[row(w) for w in widths] + [per_head, per_head],
        out_shape=[jax.ShapeDtypeStruct((nb, seq, w), dt) for w, dt in zip(widths, dtypes)]
                  + [jax.ShapeDtypeStruct((nb, seq, H_A, DV_A), F32)] * 2,
        compiler_params=_cparams(("arbitrary", "arbitrary")),
        name="in_proj",
    )(x, sh, sc, w_perm)


def _lam_value(lam_ref, lam_init):
    lp = lam_ref[...]
    a = jnp.sum(lp[0:1] * lp[1:2], axis=-1, keepdims=True)
    b = jnp.sum(lp[2:3] * lp[3:4], axis=-1, keepdims=True)
    return jnp.exp(a) - jnp.exp(b) + lam_init


def _sub_ln(o, g, lam_init):
    y = o * lax.rsqrt(jnp.mean(o * o, axis=-1, keepdims=True) + 1e-6)
    return (y * g) * (1.0 - lam_init)


def _attn_kernel(q_ref, k_ref, v_ref, sl_ref, lam_ref, g_ref, o_ref, kb_ref, vt_ref, qt_ref, *acc_refs, tq, lam_init):
    qi = pl.program_id(2)
    ngrp = 2 * tq // LANES
    nblk = k_ref.shape[1] // tq

    @pl.when(qi == 0)
    def _():
        kb_ref[...] = k_ref[0].astype(BF16)
        for j in range(nblk):
            vt_ref[j] = v_ref[0, j * tq:(j + 1) * tq, :].T.astype(BF16)

    qs = q_ref[0].astype(F32) * (DQK_A ** -0.5)
    lane = lax.broadcasted_iota(jnp.int32, (tq, 2 * DQK_A), 1)
    q1 = jnp.where(lane < DQK_A, qs, 0.0)
    q2 = jnp.where(lane >= DQK_A, qs, 0.0)
    for g in range(ngrp // 2):
        rows = slice(g * LANES, (g + 1) * LANES)
        qt_ref[:, g * LANES:(g + 1) * LANES] = q1[rows, :].T.astype(BF16)
        qt_ref[:, (ngrp // 2 + g) * LANES:(ngrp // 2 + g + 1) * LANES] = q2[rows, :].T.astype(BF16)
    for g in range(ngrp):
        acc_refs[g][...] = jnp.zeros_like(acc_refs[g])

    slope = sl_ref[0]
    krow = lax.broadcasted_iota(jnp.int32, (tq, LANES), 0)
    qcol = lax.broadcasted_iota(jnp.int32, (tq, LANES), 1)
    qt = qt_ref[...]

    def step(j, stats, masked):
        start = pl.multiple_of(j * tq, tq)
        st = lax.dot_general(kb_ref[pl.ds(start, tq), :], qt, (((1,), (0,)), ((), ())),
                             preferred_element_type=F32)
        vt = vt_ref[j]
        bias = slope * (krow + (j - qi) * tq).astype(F32)
        new_stats = []
        for g in range(ngrp):
            m_prev, l_prev = stats[g]
            s = st[:, g * LANES:(g + 1) * LANES] + bias
            if masked:
                s = jnp.where(krow <= qcol + (g * LANES) % tq, s, NEG_INF)
            m_new = jnp.maximum(m_prev, jnp.max(s, axis=0, keepdims=True))
            a = jnp.exp(m_prev - m_new)
            p = jnp.exp(s - m_new)
            l_new = a * l_prev + jnp.sum(p, axis=0, keepdims=True)
            acc_refs[g][...] = a * acc_refs[g][...] + jnp.dot(vt, p.astype(BF16), preferred_element_type=F32)
            new_stats.append((m_new, l_new))
        return tuple(new_stats)

    init = tuple((jnp.full((1, LANES), NEG_INF, F32), jnp.zeros((1, LANES), F32)) for _ in range(ngrp))
    stats = lax.fori_loop(0, qi, lambda j, c: step(j, c, False), init)
    stats = step(qi, stats, True)

    lam = _lam_value(lam_ref, lam_init)
    half = ngrp // 2
    for g in range(half):
        o1 = acc_refs[g][...] / stats[g][1]
        o2 = acc_refs[half + g][...] / stats[half + g][1]
        od = o1 - lam * o2
        y = od * lax.rsqrt(jnp.mean(od * od, axis=0, keepdims=True) + 1e-6) * g_ref[...] * (1.0 - lam_init)
        o_ref[0, g * LANES:(g + 1) * LANES, :] = y.T.astype(o_ref.dtype)


def _attn_prompt(q, k, v, slopes, lam_p, subln_g, lam_init, tq):
    nb, seq, _ = q.shape
    tq = min(tq, seq)
    assert seq % tq == 0 and tq % LANES == 0 and DV_A == LANES
    grid = (nb, H_A, seq // tq)
    g_cols = jnp.broadcast_to(subln_g.reshape(DV_A, 1), (DV_A, LANES))
    return pl.pallas_call(
        functools.partial(_attn_kernel, tq=tq, lam_init=lam_init),
        grid=grid,
        in_specs=[pl.BlockSpec((1, tq, DV_A), lambda b, h, i: (b, i, h)),
                  pl.BlockSpec((1, seq, DV_A), lambda b, h, i: (b, 0, h)),
                  pl.BlockSpec((1, seq, DV_A), lambda b, h, i: (b, 0, h)),
                  pl.BlockSpec((1, 1, LANES), lambda b, h, i: (h, 0, 0)),
                  pl.BlockSpec((4, DQK_A), lambda b, h, i: (0, 0)),
                  pl.BlockSpec((DV_A, LANES), lambda b, h, i: (0, 0))],
        out_specs=pl.BlockSpec((1, tq, DV_A), lambda b, h, i: (b, i, h)),
        out_shape=jax.ShapeDtypeStruct((nb, seq, D_A), BF16),
        scratch_shapes=[pltpu.VMEM((seq, DV_A), BF16), pltpu.VMEM((seq // tq, DV_A, tq), BF16),
                        pltpu.VMEM((2 * DQK_A, 2 * tq), BF16)]
                       + [pltpu.VMEM((DV_A, LANES), F32)] * (2 * tq // LANES),
        compiler_params=_cparams(("arbitrary", "arbitrary", "arbitrary")),
        name="attn_prompt",
    )(q, k, v, slopes, lam_p, g_cols)


PAGES_PER_STEP = 16
PAGE_ROWS = PAGE * H_A
HEAD_SHIFT = 2


def _dg_nt(a, b):
    return lax.dot_general(a, b, (((1,), (1,)), ((), ())), preferred_element_type=F32)


def _decode_kernel(pt_ref, q_ref, kn_ref, vn_ref, lam_ref, g_ref, *rest, nq, past_len, lam_init, ngroups):
    kp_refs = rest[:PAGES_PER_STEP]
    vp_refs = rest[PAGES_PER_STEP:2 * PAGES_PER_STEP]
    o_ref = rest[2 * PAGES_PER_STEP]
    qr_ref, m_ref, l_ref, acc_ref, pad_k_ref, pad_v_ref = rest[2 * PAGES_PER_STEP + 1:]
    g = pl.program_id(1)
    rows = 2 * H_A * nq

    row = lax.broadcasted_iota(jnp.int32, (rows, 1), 0)
    hm = jnp.zeros_like(row)
    for t in range(1, 2 * H_A):
        hm = hm + (row >= t * nq).astype(jnp.int32)
    qpos = past_len + (row - nq * hm)
    rhead = jnp.zeros_like(row)
    for h in range(1, H_A):
        rhead = rhead + (row >= 2 * h * nq).astype(jnp.int32)
    slope = jnp.zeros((rows, 1), F32)
    for h in range(H_A):
        slope = jnp.where(rhead == h, 2.0 ** (-8.0 * (h + 1) / H_A), slope)

    @pl.when(g == 0)
    def _():
        lane = lax.broadcasted_iota(jnp.int32, (nq, 2 * DQK_A), 1)
        for h in range(H_A):
            qh = (q_ref[0, :, h * DV_A:(h + 1) * DV_A].astype(F32) * (DQK_A ** -0.5)).astype(BF16)
            zero = jnp.zeros_like(qh)
            qr_ref[(2 * h) * nq:(2 * h + 1) * nq, :] = jnp.where(lane < DQK_A, qh, zero)
            qr_ref[(2 * h + 1) * nq:(2 * h + 2) * nq, :] = jnp.where(lane >= DQK_A, qh, zero)
        m_ref[...] = jnp.full_like(m_ref, NEG_INF)
        l_ref[...] = jnp.zeros_like(l_ref)
        acc_ref[...] = jnp.zeros_like(acc_ref)

    qr = qr_ref[...]

    def update(kbs, vbs, base_pos, causal):
        n = len(kbs)
        ncol = n * PAGE_ROWS
        s = jnp.concatenate([_dg_nt(qr, kb) for kb in kbs], axis=1)
        col = lax.broadcasted_iota(jnp.int32, (1, ncol), 1)
        ctok = col >> HEAD_SHIFT
        kpos = base_pos + ctok
        valid = (col - (ctok << HEAD_SHIFT)) == rhead
        if causal:
            valid = valid & (kpos <= qpos) & (ctok < nq)
        s = jnp.where(valid, s - slope * (qpos - kpos).astype(F32), NEG_INF)
        m_prev = m_ref[...]
        m_new = jnp.maximum(m_prev, jnp.max(s, axis=-1, keepdims=True))
        a = jnp.exp(m_prev - m_new)
        p = jnp.exp(s - jnp.concatenate([m_new] * (ncol // LANES), axis=1)).astype(BF16)
        l_ref[...] = a * l_ref[...] + jnp.sum(p.astype(F32), axis=-1, keepdims=True)
        pv = jnp.dot(p[:, 0:PAGE_ROWS], vbs[0], preferred_element_type=F32)
        for i in range(1, n):
            pv = pv + jnp.dot(p[:, i * PAGE_ROWS:(i + 1) * PAGE_ROWS], vbs[i], preferred_element_type=F32)
        acc_ref[...] = a * acc_ref[...] + pv
        m_ref[...] = m_new

    update([r[...].astype(BF16) for r in kp_refs], [r[...].astype(BF16) for r in vp_refs],
           g * (PAGES_PER_STEP * PAGE), False)

    @pl.when(g == ngroups - 1)
    def _():
        pad_k_ref[...] = jnp.zeros_like(pad_k_ref)
        pad_v_ref[...] = jnp.zeros_like(pad_v_ref)
        pad_k_ref[0:nq * H_A, :] = kn_ref[0]
        pad_v_ref[0:nq * H_A, :] = vn_ref[0]
        update([pad_k_ref[...].astype(BF16)], [pad_v_ref[...].astype(BF16)], past_len, True)
        o = acc_ref[...] / l_ref[...]
        lam = _lam_value(lam_ref, lam_init)
        for h in range(H_A):
            o1 = o[(2 * h) * nq:(2 * h + 1) * nq, :]
            o2 = o[(2 * h + 1) * nq:(2 * h + 2) * nq, :]
            y = _sub_ln(o1 - lam * o2, g_ref[...], lam_init)
            o_ref[0, :, h * DV_A:(h + 1) * DV_A] = y.astype(o_ref.dtype)


def _attn_decode(q, k_new, v_new, cache_k, cache_v, layer, pool, page_table, lam_p, subln_g, lam_init):
    nb, nq, _ = q.shape
    n_pages = page_table.shape[1]
    assert n_pages % PAGES_PER_STEP == 0 and nq * H_A <= PAGE_ROWS and H_A == 1 << HEAD_SHIFT
    ngroups = n_pages // PAGES_PER_STEP
    past_len = n_pages * PAGE
    rows = 2 * H_A * nq

    def page_spec(i):
        return pl.BlockSpec((PAGE_ROWS, DV_A),
                            lambda b, g, pt: (layer * pool + pt[b, g * PAGES_PER_STEP + i], 0))

    per_b = lambda r, w: pl.BlockSpec((1, r, w), lambda b, g, pt: (b, 0, 0))
    grid_spec = pltpu.PrefetchScalarGridSpec(
        num_scalar_prefetch=1,
        grid=(nb, ngroups),
        in_specs=[per_b(nq, D_A), per_b(nq * H_A, DV_A), per_b(nq * H_A, DV_A),
                  pl.BlockSpec((4, DQK_A), lambda b, g, pt: (0, 0)),
                  pl.BlockSpec((1, DV_A), lambda b, g, pt: (0, 0))]
                 + [page_spec(i) for i in range(PAGES_PER_STEP)]
                 + [page_spec(i) for i in range(PAGES_PER_STEP)],
        out_specs=per_b(nq, D_A),
        scratch_shapes=[pltpu.VMEM((rows, DV_A), BF16), pltpu.VMEM((rows, LANES), F32), pltpu.VMEM((rows, LANES), F32),
                        pltpu.VMEM((rows, DV_A), F32), pltpu.VMEM((PAGE_ROWS, DV_A), F32),
                        pltpu.VMEM((PAGE_ROWS, DV_A), F32)],
    )
    return pl.pallas_call(
        functools.partial(_decode_kernel, nq=nq, past_len=past_len, lam_init=lam_init, ngroups=ngroups),
        grid_spec=grid_spec,
        out_shape=jax.ShapeDtypeStruct((nb, nq, D_A), BF16),
        compiler_params=_cparams(("arbitrary", "arbitrary")),
        name="attn_decode",
    )(page_table, q, k_new.reshape(nb, nq * H_A, DV_A), v_new.reshape(nb, nq * H_A, DV_A), lam_p,
      subln_g.reshape(1, DV_A), *([cache_k] * PAGES_PER_STEP), *([cache_v] * PAGES_PER_STEP))


def _gdn_step_kernel(x_ref, cb_ref, cw_ref, ba_ref, z_ref, s0_ref, hp_ref, ng_ref, o_ref, sn_ref, xx_ref, *, seq):
    width = 3 * D_B
    hist = SUBLANES
    xx_ref[0:hist, :] = jnp.zeros((hist, width), F32)
    xx_ref[hist - (CONV_W - 1):hist, :] = cb_ref[0]
    xx_ref[hist:hist + seq, :] = x_ref[0]
    cw = cw_ref[...]
    y = cw[CONV_W - 1:CONV_W] * xx_ref[hist:hist + seq, :]
    for i in range(CONV_W - 1):
        off = hist - (CONV_W - 1) + i
        y = y + cw[i:i + 1] * xx_ref[off:off + seq, :]
    y = _silu(y)

    def l2n(v):
        return v * lax.rsqrt(jnp.sum(v * v, axis=-1, keepdims=True) + 1e-6)

    ba = ba_ref[0]
    ng = ng_ref[...]
    for h in range(H_B):
        hc = slice(h * DK_B, (h + 1) * DK_B)
        q = l2n(y[:, h * DK_B:(h + 1) * DK_B]) * (DK_B ** -0.5)
        k = l2n(y[:, D_B + h * DK_B:D_B + (h + 1) * DK_B])
        v = y[:, 2 * D_B + h * DV_B:2 * D_B + (h + 1) * DV_B]
        beta = jax.nn.sigmoid(ba[:, h:h + 1])
        xsp = ba[:, H_B + h:H_B + h + 1] + hp_ref[h, 1:2, :]
        decay = jnp.exp(-jnp.exp(hp_ref[h, 0:1, :]) * (jnp.maximum(xsp, 0.0) + jnp.log1p(jnp.exp(-jnp.abs(xsp)))))
        kt = k.T
        qt = q.T
        bv = beta * v
        state = s0_ref[0, h]
        outs = []
        for t in range(seq):
            kcol = kt[:, t:t + 1]
            state = state * decay[t:t + 1, :]
            r = jnp.sum(kcol * state, axis=0, keepdims=True)
            state = state + kcol * (bv[t:t + 1, :] - beta[t:t + 1, :] * r)
            outs.append(jnp.sum(qt[:, t:t + 1] * state, axis=0, keepdims=True))
        o = jnp.concatenate(outs, axis=0)
        sn_ref[0, h] = state
        yo = o * lax.rsqrt(jnp.mean(o * o, axis=-1, keepdims=True) + 1e-6) * ng
        o_ref[0, :, hc] = (yo * _silu(z_ref[0, :, hc])).astype(o_ref.dtype)


def _gated_deltanet_step(qkvb, ba, z, conv_buf, s0, conv_w, head_params, norm_g):
    nb, seq, width = qkvb.shape
    row = lambda w: pl.BlockSpec((1, seq, w), lambda b: (b, 0, 0))
    st = pl.BlockSpec((1, H_B, DK_B, DV_B), lambda b: (b, 0, 0, 0))
    return pl.pallas_call(
        functools.partial(_gdn_step_kernel, seq=seq),
        grid=(nb,),
        in_specs=[row(width),
                  pl.BlockSpec((1, CONV_W - 1, width), lambda b: (b, 0, 0)),
                  pl.BlockSpec((CONV_W, width), lambda b: (0, 0)),
                  row(LANES), row(D_B), st,
                  pl.BlockSpec((H_B, 2, LANES), lambda b: (0, 0, 0)),
                  pl.BlockSpec((1, DV_B), lambda b: (0, 0))],
        out_specs=[row(D_B), st],
        out_shape=[jax.ShapeDtypeStruct((nb, seq, D_B), BF16),
                   jax.ShapeDtypeStruct((nb, H_B, DK_B, DV_B), F32)],
        scratch_shapes=[pltpu.VMEM((seq + SUBLANES, width), F32)],
        compiler_params=_cparams(("arbitrary",)),
        name="gated_deltanet_step",
    )(qkvb, conv_buf, conv_w, ba, z, s0, head_params, norm_g.reshape(1, DV_B))


GDN_TILE = 512
GDN_SUPER = 256


def _split(a):
    hi = a.astype(BF16)
    return hi, (a - hi.astype(F32)).astype(BF16)


def _dg(a, b, dims):
    return lax.dot_general(a, b, (dims, ((), ())), preferred_element_type=F32)


_NN = ((1,), (0,))
_NT = ((1,), (1,))
_TN = ((0,), (0,))


def _mm3s(asp, bsp, dims=_NN):
    (ah, al), (bh, bl) = asp, bsp
    return _dg(ah, bh, dims) + (_dg(ah, bl, dims) + _dg(al, bh, dims))


def _mm3(a, b, dims=_NN):
    return _mm3s(_split(a), _split(b), dims)


def _mm_exact_lhs(t, b):
    b1 = b.astype(BF16)
    r1 = b - b1.astype(F32)
    b2 = r1.astype(BF16)
    b3 = (r1 - b2.astype(F32)).astype(BF16)
    return _dg(t, b1, _NN) + (_dg(t, b2, _NN) + _dg(t, b3, _NN))


def _prod(a, b):
    return _dg(a.astype(BF16), b.astype(BF16), _NN)


def _unit_lower_inverse_minus_eye(lmat, size, csz):
    r = lax.broadcasted_iota(jnp.int32, (size, size), 0)
    c = lax.broadcasted_iota(jnp.int32, (size, size), 1)
    ld = jnp.where((r // SUBLANES) == (c // SUBLANES), lmat, 0.0)
    p2 = _prod(ld, ld)
    d = (p2 - ld) - _prod(ld, p2)
    p4 = _prod(p2, p2)
    d = d + p4 + _prod(d, p4)
    e = lmat - ld
    n = e + _prod(d, e)
    out = -n
    pw = n
    k = 2
    while k < csz // SUBLANES:
        pw = _prod(pw, pw)
        out = out + pw + _prod(out, pw)
        k *= 2
    return out + d + _prod(out, d)


def _gdn_long_kernel(x_ref, cb_ref, cw_ref, ba_ref, z_ref, s0_ref, hp_ref, ng_ref, o_ref, sn_ref,
                     xx_ref, y_ref, st_ref, u_ref, w_ref, qd_ref, kd_ref, gt_ref, qk_ref, *, tl):
    t = pl.program_id(1)
    width = 3 * D_B
    hist = SUBLANES

    @pl.when(t == 0)
    def _():
        xx_ref[0:hist, :] = jnp.zeros((hist, width), F32)
        xx_ref[hist - (CONV_W - 1):hist, :] = cb_ref[0]
        st_ref[...] = s0_ref[0]

    @pl.when(t > 0)
    def _():
        xx_ref[0:hist, :] = xx_ref[tl:tl + hist, :]

    xx_ref[hist:hist + tl, :] = x_ref[0]

    cw = cw_ref[...]
    blk = 256
    for r0 in range(0, tl, blk):
        y = cw[CONV_W - 1:CONV_W] * xx_ref[hist + r0:hist + r0 + blk, :]
        for i in range(CONV_W - 1):
            off = hist - (CONV_W - 1) + i + r0
            y = y + cw[i:i + 1] * xx_ref[off:off + blk, :]
        y_ref[r0:r0 + blk, :] = _silu(y)

    sup = GDN_SUPER
    r = lax.broadcasted_iota(jnp.int32, (sup, sup), 0)
    c = lax.broadcasted_iota(jnp.int32, (sup, sup), 1)
    same_chunk = (r // CHUNK) == (c // CHUNK)
    tril = same_chunk & (r >= c)
    strict = same_chunk & (r > c)
    diag = r == c
    sum_mat = jnp.concatenate([tril.astype(BF16), same_chunk.astype(BF16)], axis=0)

    def l2n(v):
        return v * lax.rsqrt(jnp.sum(v * v, axis=-1, keepdims=True) + 1e-6)

    for sc in range(tl // sup):
        rs = slice(sc * sup, (sc + 1) * sup)
        ba = ba_ref[0, rs, :]
        for h in range(H_B):
            hc = slice(h * DK_B, (h + 1) * DK_B)
            q = l2n(y_ref[rs, h * DK_B:(h + 1) * DK_B]) * (DK_B ** -0.5)
            k = l2n(y_ref[rs, D_B + h * DK_B:D_B + (h + 1) * DK_B])
            v = y_ref[rs, 2 * D_B + h * DV_B:2 * D_B + (h + 1) * DV_B]
            beta = jax.nn.sigmoid(ba[:, h:h + 1])
            xsp = ba[:, H_B + h:H_B + h + 1] + hp_ref[h, 1:2, :]
            gl = -jnp.exp(hp_ref[h, 0:1, :]) * (jnp.maximum(xsp, 0.0) + jnp.log1p(jnp.exp(-jnp.abs(xsp))))
            sums = _mm_exact_lhs(sum_mat, gl)
            gcum, gtot = sums[0:sup], sums[sup:2 * sup]
            gi = jnp.concatenate([gcum] * (sup // LANES), axis=1)
            gj = jnp.sum(jnp.where(diag, gi, 0.0), axis=0, keepdims=True)
            decay = jnp.where(tril, jnp.exp(jnp.where(tril, gi - gj, 0.0)), 0.0)
            eg = jnp.exp(gcum)
            kb = k * beta
            kbf = k.astype(BF16)
            lmat = jnp.where(strict, _dg(kb.astype(BF16), kbf, _NT) * decay, 0.0)
            rhs = jnp.concatenate([v * beta, kb * eg], axis=1)
            uw = rhs + _mm3(_unit_lower_inverse_minus_eye(lmat, sup, CHUNK), rhs)
            u_ref[rs, hc] = uw[:, 0:DV_B]
            w_ref[rs, hc] = uw[:, DV_B:DV_B + DK_B]
            qk = _dg(q.astype(BF16), kbf, _NT) * decay
            qkc = qk[:, 0:CHUNK]
            for i in range(1, sup // CHUNK):
                qkc = qkc + qk[:, i * CHUNK:(i + 1) * CHUNK]
            qk_ref[h, rs, :] = qkc
            qd_ref[rs, hc] = q * eg
            kd_ref[rs, hc] = k * jnp.exp(gtot - gcum)
            gt_ref[rs, hc] = jnp.exp(gtot)

    ng = ng_ref[...]

    def chunk(ci, carry):
        start = ci * CHUNK
        rows = pl.ds(start, CHUNK)
        for h in range(H_B):
            hc = slice(h * DK_B, (h + 1) * DK_B)
            ssp = _split(st_ref[h])
            v_new = u_ref[rows, hc] - _mm3s(_split(w_ref[rows, hc]), ssp)
            vsp = _split(v_new)
            o = _dg(qd_ref[rows, hc].astype(BF16), ssp[0], _NN) + _dg(qk_ref[h, rows, :].astype(BF16), vsp[0], _NN)
            st_ref[h] = st_ref[h] * gt_ref[pl.ds(start, 1), hc] + _mm3s(_split(kd_ref[rows, hc]), vsp, _TN)
            y = o * lax.rsqrt(jnp.mean(o * o, axis=-1, keepdims=True) + 1e-6) * ng
            o_ref[0, rows, hc] = (y * _silu(z_ref[0, rows, hc])).astype(o_ref.dtype)
        return carry

    for ci in range(tl // CHUNK):
        chunk(ci, 0)
    sn_ref[0] = st_ref[...]


def _gated_deltanet_long(qkvb, ba, z, conv_buf, s0, conv_w, head_params, norm_g):
    nb, seq, width = qkvb.shape
    tl = GDN_TILE
    assert seq % tl == 0 and tl % GDN_SUPER == 0
    row = lambda w: pl.BlockSpec((1, tl, w), lambda b, t: (b, t, 0))
    st = pl.BlockSpec((1, H_B, DK_B, DV_B), lambda b, t: (b, 0, 0, 0))
    scr = lambda n, w: pltpu.VMEM((n, w), F32)
    return pl.pallas_call(
        functools.partial(_gdn_long_kernel, tl=tl),
        grid=(nb, seq // tl),
        in_specs=[row(width),
                  pl.BlockSpec((1, CONV_W - 1, width), lambda b, t: (b, 0, 0)),
                  pl.BlockSpec((CONV_W, width), lambda b, t: (0, 0)),
                  row(LANES), row(D_B), st,
                  pl.BlockSpec((H_B, 2, LANES), lambda b, t: (0, 0, 0)),
                  pl.BlockSpec((1, DV_B), lambda b, t: (0, 0))],
        out_specs=[row(D_B), st],
        out_shape=[jax.ShapeDtypeStruct((nb, seq, D_B), BF16),
                   jax.ShapeDtypeStruct((nb, H_B, DK_B, DV_B), F32)],
        scratch_shapes=[scr(tl + SUBLANES, width), scr(tl, width), pltpu.VMEM((H_B, DK_B, DV_B), F32),
                        scr(tl, D_B), scr(tl, D_B), scr(tl, D_B), scr(tl, D_B), scr(tl, D_B),
                        pltpu.VMEM((H_B, tl, CHUNK), F32)],
        compiler_params=_cparams(("arbitrary", "arbitrary")),
        name="gated_deltanet_long",
    )(qkvb, conv_buf, conv_w, ba, z, s0, head_params, norm_g.reshape(1, DV_B))


def _outproj_kernel(oa_ref, ob_ref, x_ref, w_ref, g1_ref, lg_ref, lb_ref, sh_ref, sc_ref, *rest, tb, tl, alpha, region):
    router = region is not None
    if router:
        wr_ref, br_ref, x1_ref, h2_ref, pair_ref, cnt_ref, run_ref = rest
    else:
        x1_ref, h2_ref = rest
    rows = tb * tl
    oa = oa_ref[...].reshape(rows, D_A)
    ob = ob_ref[...].reshape(rows, D_B)
    mix = (jnp.dot(oa, w_ref[0:D_A, :], preferred_element_type=F32)
           + jnp.dot(ob, w_ref[D_A:D_A + D_B, :], preferred_element_type=F32))
    y = alpha * x_ref[...] + g1_ref[...] * mix.reshape(tb, tl, D_MODEL)
    x1 = _ln(y) * lg_ref[...] + lb_ref[...]
    x1_ref[...] = x1
    h2 = (_ln(x1) * (1.0 + sc_ref[...]) + sh_ref[...]).astype(BF16)
    h2_ref[...] = h2
    if router:
        @pl.when((pl.program_id(0) == 0) & (pl.program_id(1) == 0))
        def _():
            run_ref[...] = jnp.zeros_like(run_ref)

        logits = jnp.dot(h2.reshape(rows, D_MODEL), wr_ref[...], preferred_element_type=F32) + br_ref[...]
        lane = lax.broadcasted_iota(jnp.int32, (rows, LANES), 1)
        logits = jnp.where(lane < N_EXPERTS, logits, NEG_INF)
        v1 = jnp.max(logits, axis=-1, keepdims=True)
        i1 = jnp.min(jnp.where(logits == v1, lane, LANES), axis=-1, keepdims=True)
        rest_l = jnp.where(lane == i1, NEG_INF, logits)
        v2 = jnp.max(rest_l, axis=-1, keepdims=True)
        i2 = jnp.min(jnp.where(rest_l == v2, lane, LANES), axis=-1, keepdims=True)
        e2 = jnp.exp(v2 - v1)
        gate1 = 1.0 / (1.0 + e2)
        gate2 = e2 / (1.0 + e2)
        sel = jnp.where((lane == i1) | (lane == i2), 1.0, 0.0)
        r = lax.broadcasted_iota(jnp.int32, (rows, rows), 0)
        c = lax.broadcasted_iota(jnp.int32, (rows, rows), 1)
        earlier = jnp.where(r > c, 1.0, 0.0).astype(BF16)
        rank = jnp.dot(earlier, sel.astype(BF16), preferred_element_type=F32) + run_ref[...]
        run_ref[...] = run_ref[...] + jnp.sum(sel, axis=0, keepdims=True)
        slot = lane.astype(F32) * float(region) + rank
        first, second = jnp.minimum(i1, i2), jnp.maximum(i1, i2)
        slot_a = jnp.sum(jnp.where(lane == first, slot, 0.0), axis=-1, keepdims=True)
        slot_b = jnp.sum(jnp.where(lane == second, slot, 0.0), axis=-1, keepdims=True)
        gate_a = jnp.where(first == i1, gate1, gate2)
        gate_b = jnp.where(first == i1, gate2, gate1)
        pair = (jnp.where(lane == 0, slot_a, 0.0) + jnp.where(lane == 1, slot_b, 0.0)
                + jnp.where(lane == 2, gate_a, 0.0) + jnp.where(lane == 3, gate_b, 0.0))
        pair_ref[...] = pair.reshape(tb, tl, LANES)
        cnt_ref[...] = run_ref[...]


def _out_proj(oa, ob, x, w_out_b, g1, ln_g, ln_b, sh2, sc2, alpha, tm, router_w=None, router_b=None, region=None):
    nb, seq, _ = x.shape
    tb, tl = _row_tile(nb, seq, tm)
    grid = (nb // tb, seq // tl)
    row = lambda w: pl.BlockSpec((tb, tl, w), lambda i, j: (i, j, 0))
    mod = pl.BlockSpec((tb, 1, D_MODEL), lambda i, j: (i, 0, 0))
    vec = pl.BlockSpec((1, 1, D_MODEL), lambda i, j: (0, 0, 0))
    in_specs = [row(D_A), row(D_B), row(D_MODEL), pl.BlockSpec((D_MODEL, D_MODEL), lambda i, j: (0, 0)),
                mod, vec, vec, mod, mod]
    args = [oa, ob, x, w_out_b, g1, ln_g.reshape(1, 1, D_MODEL), ln_b.reshape(1, 1, D_MODEL), sh2, sc2]
    out_specs = [row(D_MODEL), row(D_MODEL)]
    out_shape = [jax.ShapeDtypeStruct((nb, seq, D_MODEL), F32), jax.ShapeDtypeStruct((nb, seq, D_MODEL), BF16)]
    scratch = []
    if region is not None:
        in_specs += [pl.BlockSpec((D_MODEL, LANES), lambda i, j: (0, 0)), pl.BlockSpec((1, LANES), lambda i, j: (0, 0))]
        args += [router_w, router_b]
        out_specs += [row(LANES), pl.BlockSpec((1, LANES), lambda i, j: (0, 0))]
        out_shape += [jax.ShapeDtypeStruct((nb, seq, LANES), F32), jax.ShapeDtypeStruct((1, LANES), F32)]
        scratch = [pltpu.VMEM((1, LANES), F32)]
    return pl.pallas_call(
        functools.partial(_outproj_kernel, tb=tb, tl=tl, alpha=alpha, region=region),
        grid=grid, in_specs=in_specs, out_specs=out_specs, out_shape=out_shape, scratch_shapes=scratch,
        compiler_params=_cparams(("arbitrary", "arbitrary")),
        name="out_proj",
    )(*args)


def _ffn_kernel(h_ref, x_ref, wg_ref, wu_ref, wd_ref, g2_ref, lg_ref, lb_ref, o_ref, acc_ref, *, tb, tl, alpha):
    f = pl.program_id(2)
    rows = tb * tl

    @pl.when(f == 0)
    def _():
        acc_ref[...] = jnp.zeros_like(acc_ref)

    hb = h_ref[...].reshape(rows, D_MODEL)
    gt = jnp.dot(hb, wg_ref[...], preferred_element_type=F32)
    up = jnp.dot(hb, wu_ref[...], preferred_element_type=F32)
    a = (_silu(gt) * up).astype(BF16)
    acc_ref[...] += jnp.dot(a, wd_ref[...], preferred_element_type=F32)

    @pl.when(f == pl.num_programs(2) - 1)
    def _():
        y = alpha * x_ref[...] + g2_ref[...] * acc_ref[...].reshape(tb, tl, D_MODEL)
        o_ref[...] = _ln(y) * lg_ref[...] + lb_ref[...]


def _ffn(h2, x1, w_up_b, w_down_b, g2, ln_g, ln_b, alpha, tm, tf):
    nb, seq, _ = x1.shape
    tb, tl = _row_tile(nb, seq, tm)
    nf = D_FF // tf
    grid = (nb // tb, seq // tl, nf)
    row = lambda w: pl.BlockSpec((tb, tl, w), lambda i, j, f: (i, j, 0))
    return pl.pallas_call(
        functools.partial(_ffn_kernel, tb=tb, tl=tl, alpha=alpha),
        grid=grid,
        in_specs=[row(D_MODEL), row(D_MODEL),
                  pl.BlockSpec((D_MODEL, tf), lambda i, j, f: (0, f)),
                  pl.BlockSpec((D_MODEL, tf), lambda i, j, f: (0, nf + f)),
                  pl.BlockSpec((tf, D_MODEL), lambda i, j, f: (f, 0)),
                  pl.BlockSpec((tb, 1, D_MODEL), lambda i, j, f: (i, 0, 0)),
                  pl.BlockSpec((1, 1, D_MODEL), lambda i, j, f: (0, 0, 0)),
                  pl.BlockSpec((1, 1, D_MODEL), lambda i, j, f: (0, 0, 0))],
        out_specs=row(D_MODEL),
        out_shape=jax.ShapeDtypeStruct((nb, seq, D_MODEL), F32),
        scratch_shapes=[pltpu.VMEM((tb * tl, D_MODEL), F32)],
        compiler_params=_cparams(("arbitrary", "arbitrary", "arbitrary")),
        name="ffn",
    )(h2, x1, w_up_b, w_up_b, w_down_b, g2, ln_g.reshape(1, 1, D_MODEL), ln_b.reshape(1, 1, D_MODEL))


MOE_ROWS = 512
ROW_TILE = (SUBLANES, LANES)
ISSUE_UNROLL = 8


def _region_rows(m):
    return -(-m // MOE_ROWS) * MOE_ROWS


def _dispatch_kernel(slot_ref, pad_ref, h_ref, xs_ref, src_ref, zero_ref, sem, *, rows, npad):
    @pl.when((pl.program_id(0) == 0) & (pl.program_id(1) == 0))
    def _():
        zero_ref[...] = jnp.zeros_like(zero_ref)

        def pad_copy(k):
            return pltpu.make_async_copy(zero_ref.at[0], xs_ref.at[pad_ref[0, k]], sem)

        def issue_pad(k, c):
            pad_copy(k).start()
            return c

        def drain_pad(k, c):
            pad_copy(0).wait()
            return c

        lax.fori_loop(0, npad, issue_pad, 0)
        lax.fori_loop(0, npad, drain_pad, 0)

    src_ref[...] = h_ref[...].reshape(rows, D_MODEL).astype(F32).reshape(rows, *ROW_TILE)

    def row_copy(t, k):
        return pltpu.make_async_copy(src_ref.at[t], xs_ref.at[slot_ref[k, t]], sem)

    def issue(i, c):
        for u in range(ISSUE_UNROLL):
            t = i * ISSUE_UNROLL + u
            row_copy(t, 0).start()
            row_copy(t, 1).start()
        return c

    lax.fori_loop(0, rows // ISSUE_UNROLL, issue, 0)

    def drain(i, c):
        row_copy(0, 0).wait()
        row_copy(0, 1).wait()
        return c

    lax.fori_loop(0, rows, drain, 0)


def _dispatch(h2, slots, pad_slots, total_rows):
    nb, seq, _ = h2.shape
    tb, tl = _row_tile(nb, seq, MOE_ROWS)
    rows = tb * tl
    nj = seq // tl
    npad = pad_slots.shape[1]
    return pl.pallas_call(
        functools.partial(_dispatch_kernel, rows=rows, npad=npad),
        grid=(nb // tb, nj),
        in_specs=[pl.BlockSpec((2, rows), lambda i, j: (0, i * nj + j), memory_space=pltpu.SMEM),
                  pl.BlockSpec((1, npad), lambda i, j: (0, 0), memory_space=pltpu.SMEM),
                  pl.BlockSpec((tb, tl, D_MODEL), lambda i, j: (i, j, 0))],
        out_specs=pl.BlockSpec(memory_space=pl.ANY),
        out_shape=jax.ShapeDtypeStruct((total_rows, *ROW_TILE), F32),
        scratch_shapes=[pltpu.VMEM((rows, *ROW_TILE), F32), pltpu.VMEM((1, *ROW_TILE), F32),
                        pltpu.SemaphoreType.DMA(())],
        compiler_params=_cparams(("arbitrary", "arbitrary")),
        name="moe_dispatch",
    )(slots, pad_slots, h2)


def _experts_kernel(blk_e_ref, blk_rows_ref, nused_ref, x_ref, wg_ref, wu_ref, wd_ref, o_ref, xb_ref, acc_ref):
    i, f = pl.program_id(0), pl.program_id(1)
    nf = pl.num_programs(1)
    nrows = blk_rows_ref[i]
    active = nrows > 0

    @pl.when(active & (f == 0))
    def _():
        x = x_ref[...].reshape(MOE_ROWS, D_MODEL)
        row = lax.broadcasted_iota(jnp.int32, (MOE_ROWS, 1), 0)
        xb_ref[...] = jnp.where(row < nrows, x, 0.0).astype(BF16)
        acc_ref[...] = jnp.zeros_like(acc_ref)

    @pl.when(active)
    def _():
        xb = xb_ref[...]
        gt = jnp.dot(xb, wg_ref[0], preferred_element_type=F32)
        up = jnp.dot(xb, wu_ref[0], preferred_element_type=F32)
        a = (_silu(gt) * up).astype(BF16)
        acc_ref[...] += jnp.dot(a, wd_ref[0], preferred_element_type=F32)

    @pl.when(active & (f == nf - 1))
    def _():
        o_ref[...] = acc_ref[...].reshape(MOE_ROWS, *ROW_TILE)

    @pl.when(jnp.logical_not(active) & (f == nf - 1))
    def _():
        o_ref[...] = jnp.zeros_like(o_ref)


def _experts(xs, blk_e, blk_rows, nused, w_up_b, w_down_b, tf):
    nblk = xs.shape[0] // MOE_ROWS
    nf = D_FF // tf

    def xi(i, f, be, br, nu):
        return jnp.minimum(i, nu[0] - 1)

    def ei(i, f, be, br, nu):
        return be[jnp.minimum(i, nu[0] - 1)]

    def fi(i, f, be, br, nu):
        return jnp.where(br[i] > 0, f, nf - 1)

    grid_spec = pltpu.PrefetchScalarGridSpec(
        num_scalar_prefetch=3,
        grid=(nblk, nf),
        in_specs=[pl.BlockSpec((MOE_ROWS, *ROW_TILE), lambda *a: (xi(*a), 0, 0)),
                  pl.BlockSpec((1, D_MODEL, tf), lambda *a: (ei(*a), 0, fi(*a))),
                  pl.BlockSpec((1, D_MODEL, tf), lambda *a: (ei(*a), 0, nf + fi(*a))),
                  pl.BlockSpec((1, tf, D_MODEL), lambda *a: (ei(*a), fi(*a), 0))],
        out_specs=pl.BlockSpec((MOE_ROWS, *ROW_TILE), lambda i, f, be, br, nu: (i, 0, 0)),
        scratch_shapes=[pltpu.VMEM((MOE_ROWS, D_MODEL), BF16), pltpu.VMEM((MOE_ROWS, D_MODEL), F32)],
    )
    return pl.pallas_call(
        _experts_kernel, grid_spec=grid_spec,
        out_shape=jax.ShapeDtypeStruct(xs.shape, F32),
        compiler_params=_cparams(("arbitrary", "arbitrary")),
        name="moe_experts",
    )(blk_e, blk_rows, nused, xs, w_up_b, w_up_b, w_down_b)


def _moe_layout(pair, cnt, m, region):
    raw = pair[..., 0:2].reshape(m, 2).astype(jnp.int32)
    e = raw // region
    counts = cnt[0, :N_EXPERTS].astype(jnp.int32)
    padded = (counts + MOE_ROWS - 1) // MOE_ROWS * MOE_ROWS
    off = jnp.cumsum(padded) - padded
    slots = (off[e] + (raw - e * region)).T
    nblk = (2 * m + MOE_ROWS - 1) // MOE_ROWS + N_EXPERTS
    bstart = jnp.arange(nblk, dtype=jnp.int32) * MOE_ROWS
    blk_e = jnp.minimum(jnp.sum((bstart[:, None] >= (off + padded)[None, :]).astype(jnp.int32), axis=1), N_EXPERTS - 1)
    blk_rows = jnp.clip(counts[blk_e] - (bstart - off[blk_e]), 0, MOE_ROWS)
    nused = (jnp.sum(padded) // MOE_ROWS).reshape(1)
    npad = nblk * MOE_ROWS - 2 * m
    run_start = jnp.concatenate([off + counts, jnp.sum(padded).reshape(1)])
    run_len = jnp.concatenate([padded - counts, (nblk * MOE_ROWS - jnp.sum(padded)).reshape(1)])
    run_end = jnp.cumsum(run_len)
    k = jnp.arange(npad, dtype=jnp.int32)[:, None]
    in_run = (k >= (run_end - run_len)[None, :]) & (k < run_end[None, :])
    pad_slots = jnp.sum(jnp.where(in_run, run_start[None, :] + k - (run_end - run_len)[None, :], 0), axis=1)
    return slots, pad_slots.astype(jnp.int32).reshape(1, npad), blk_e, blk_rows, nused


def _combine_kernel(slot_ref, ys_ref, pair_ref, x_ref, g2_ref, lg_ref, lb_ref, o_ref, buf0, buf1, sem, *, tb, tl, alpha):
    rows = tb * tl

    def row_copy(t, k):
        return pltpu.make_async_copy(ys_ref.at[slot_ref[k, t]], (buf0 if k == 0 else buf1).at[t], sem)

    def issue(i, c):
        for u in range(ISSUE_UNROLL):
            t = i * ISSUE_UNROLL + u
            row_copy(t, 0).start()
            row_copy(t, 1).start()
        return c

    lax.fori_loop(0, rows // ISSUE_UNROLL, issue, 0)

    def drain(i, c):
        row_copy(0, 0).wait()
        row_copy(0, 1).wait()
        return c

    lax.fori_loop(0, rows, drain, 0)
    pair = pair_ref[...].reshape(rows, LANES)
    ga, gb = pair[:, 2:3], pair[:, 3:4]
    f = ga * buf0[...].reshape(rows, D_MODEL) + gb * buf1[...].reshape(rows, D_MODEL)
    y = alpha * x_ref[...] + g2_ref[...] * f.reshape(tb, tl, D_MODEL)
    o_ref[...] = _ln(y) * lg_ref[...] + lb_ref[...]


def _combine(ys, slots, pair, x1, g2, ln_g, ln_b, alpha):
    nb, seq, _ = x1.shape
    tb, tl = _row_tile(nb, seq, MOE_ROWS)
    rows = tb * tl
    nj = seq // tl
    row = lambda w: pl.BlockSpec((tb, tl, w), lambda i, j: (i, j, 0))
    return pl.pallas_call(
        functools.partial(_combine_kernel, tb=tb, tl=tl, alpha=alpha),
        grid=(nb // tb, nj),
        in_specs=[pl.BlockSpec((2, rows), lambda i, j: (0, i * nj + j), memory_space=pltpu.SMEM),
                  pl.BlockSpec(memory_space=pl.ANY), row(LANES), row(D_MODEL),
                  pl.BlockSpec((tb, 1, D_MODEL), lambda i, j: (i, 0, 0)),
                  pl.BlockSpec((1, 1, D_MODEL), lambda i, j: (0, 0, 0)),
                  pl.BlockSpec((1, 1, D_MODEL), lambda i, j: (0, 0, 0))],
        out_specs=row(D_MODEL),
        out_shape=jax.ShapeDtypeStruct((nb, seq, D_MODEL), F32),
        scratch_shapes=[pltpu.VMEM((rows, *ROW_TILE), F32), pltpu.VMEM((rows, *ROW_TILE), F32),
                        pltpu.SemaphoreType.DMA(())],
        compiler_params=_cparams(("arbitrary", "arbitrary")),
        name="moe_combine",
    )(slots, ys, pair, x1, g2, ln_g.reshape(1, 1, D_MODEL), ln_b.reshape(1, 1, D_MODEL))


def _trunk(x, mod, conv_bufs, delta_states, past, weights, tm):
    depth = weights["w_in"].shape[0]
    alpha = (2 * depth) ** 0.25
    nb, seq, _ = x.shape
    new_k, new_v, new_conv, new_delta = [], [], [], []
    slopes = jnp.broadcast_to(
        jnp.asarray([2.0 ** (-8.0 * (h + 1) / H_A) for h in range(H_A)], F32)[:, None, None], (H_A, 1, LANES))
    for l in range(depth):
        sh1, sc1, g1, sh2, sc2, g2 = [mod[l, s] for s in range(6)]
        lam_init = 0.8 - 0.6 * math.exp(-0.3 * l)
        q, k, v, qkvb, z, ba, k_heads, v_heads = _in_proj(x, sh1, sc1, weights["w_in"][l], tm)
        new_k.append(k_heads)
        new_v.append(v_heads)
        if past is None:
            o_a = _attn_prompt(q, k, v, slopes, weights["lam_params"][l], weights["subln_a"][l], lam_init, 512)
        else:
            cache_k, cache_v, pool, page_table = past
            o_a = _attn_decode(q, k, v, cache_k, cache_v, l, pool, page_table,
                               weights["lam_params"][l], weights["subln_a"][l], lam_init)
        gdn = _gated_deltanet_long if seq % GDN_TILE == 0 else _gated_deltanet_step
        o_b, s_new = gdn(qkvb, ba, z, conv_bufs[l], delta_states[l], weights["conv_w"][l],
                         weights["head_params"][l], weights["norm_b"][l])
        new_conv.append(qkvb[:, seq - (CONV_W - 1):, :] if seq >= CONV_W - 1 else
                        jnp.concatenate([conv_bufs[l], qkvb], axis=1)[:, -(CONV_W - 1):, :])
        new_delta.append(s_new)
        j = l // 2
        if l % 2 == 0:
            x1, h2 = _out_proj(o_a, o_b, x, weights["w_out"][l], g1, weights["ln1_g"][l], weights["ln1_b"][l],
                               sh2, sc2, alpha, tm)
            x = _ffn(h2, x1, weights["dense_w_up"][j], weights["dense_w_down"][j], g2,
                     weights["ln2_g"][l], weights["ln2_b"][l], alpha, tm, D_FF // 2)
        else:
            region = _region_rows(nb * seq)
            x1, h2, pair, cnt = _out_proj(o_a, o_b, x, weights["w_out"][l], g1, weights["ln1_g"][l],
                                          weights["ln1_b"][l], sh2, sc2, alpha, MOE_ROWS,
                                          weights["router_w"][j], weights["router_b"][j], region)
            slots, pad_slots, blk_e, blk_rows, nused = _moe_layout(pair, cnt, nb * seq, region)
            xs = _dispatch(h2, slots, pad_slots, blk_e.shape[0] * MOE_ROWS)
            ys = _experts(xs, blk_e, blk_rows, nused, weights["moe_w_up"][j], weights["moe_w_down"][j], D_FF // 2)
            x = _combine(ys, slots, pair, x1, g2, weights["ln2_g"][l], weights["ln2_b"][l], alpha)
    return x, jnp.stack(new_k), jnp.stack(new_v), jnp.stack(new_conv), jnp.stack(new_delta)


def kernel(x_prompt, x_sample, cache_k, cache_v, state_conv, state_delta, page_table, c_prompt, c_sample,
           w_in, w_out, lam_params, subln_a, conv_w, a_log, dt_bias, norm_b, ada_w, ada_b,
           ln1_g, ln1_b, ln2_g, ln2_b, dense_w_up, dense_w_down, moe_router, moe_router_b, moe_w_up, moe_w_down):
    depth = w_in.shape[0]
    nbp, nbs = x_prompt.shape[0], x_sample.shape[0]
    a_q, a_k, a_v, b_qkv = 2 * H_A * DQK_A, 2 * H_A * DQK_A, H_A * DV_A, 3 * D_B
    n0 = a_q + a_k + a_v + b_qkv
    w_perm = jnp.concatenate(
        [w_in[:, :, :n0], w_in[:, :, n0 + 2 * H_B:], w_in[:, :, n0:n0 + 2 * H_B],
         jnp.zeros((depth, D_MODEL, LANES - 2 * H_B), w_in.dtype)], axis=-1).astype(BF16)
    n_moe = moe_router.shape[0]
    router_w = jnp.concatenate([moe_router, jnp.zeros((n_moe, D_MODEL, LANES - N_EXPERTS), moe_router.dtype)],
                               axis=-1).astype(BF16)
    router_b = jnp.concatenate([moe_router_b, jnp.zeros((n_moe, LANES - N_EXPERTS), moe_router_b.dtype)],
                               axis=-1).reshape(n_moe, 1, LANES)
    head_params = jnp.broadcast_to(jnp.stack([a_log, dt_bias], axis=-1)[..., None], (depth, H_B, 2, LANES))
    weights = dict(w_in=w_perm, w_out=w_out.astype(BF16), lam_params=lam_params, subln_a=subln_a, conv_w=conv_w,
                   head_params=head_params, norm_b=norm_b, ln1_g=ln1_g, ln1_b=ln1_b, ln2_g=ln2_g, ln2_b=ln2_b,
                   dense_w_up=dense_w_up.astype(BF16), dense_w_down=dense_w_down.astype(BF16),
                   router_w=router_w, router_b=router_b,
                   moe_w_up=moe_w_up.astype(BF16), moe_w_down=moe_w_down.astype(BF16))

    mod = _ada_mod(jnp.concatenate([c_prompt, c_sample], axis=0), ada_w, ada_b)
    mod = mod[:, :, :, None, :]
    conv0 = jnp.zeros((depth, nbp, CONV_W - 1, 3 * D_B), x_prompt.dtype)
    delta0 = jnp.zeros((depth, nbp, H_B, DK_B, DV_B), x_prompt.dtype)
    y_p, k_p, v_p, conv_p, delta_p = _trunk(x_prompt, mod[:, :, :nbp], conv0, delta0, None, weights, 512)

    n_pool = cache_k.shape[1]
    ck = cache_k.reshape(depth * n_pool * PAGE * H_A, 2 * DQK_A)
    cv = cache_v.reshape(depth * n_pool * PAGE * H_A, DV_A)
    y_s, k_s, v_s, conv_s, delta_s = _trunk(x_sample, mod[:, :, nbp:], state_conv, state_delta,
                                            (ck, cv, n_pool, page_table), weights, 512)
    return (y_p, y_s, k_p, v_p, conv_p, delta_p, k_s, v_s, conv_s, delta_s)
```

```python
import functools
import math

import jax
import jax.numpy as jnp
from jax import lax
from jax.experimental import pallas as pl
from jax.experimental.pallas import tpu as pltpu

F32 = jnp.float32
BF16 = jnp.bfloat16

D_MODEL = 1024
H_A = 4
DV_A = 128
DQK_A = 64
D_A = H_A * DV_A
H_B = 4
DK_B = 128
DV_B = 128
D_B = H_B * DV_B
CONV_W = 4
CHUNK = 64
D_FF = 2816
N_EXPERTS = 8
LN_EPS = 1e-5
PAGE = 128
LANES = 128
SUBLANES = 8
VMEM_LIMIT = 56 * 1024 * 1024
NEG_INF = float("-inf")

IN_SEGS = (("q", 0, 512), ("k", 512, 1024), ("v", 1024, 1536), ("qkvb", 1536, 3072),
           ("z", 3072, 3584), ("ba", 3584, 3712))
D_IN_PAD = 3712


def _cparams(sem):
    return pltpu.CompilerParams(dimension_semantics=sem, vmem_limit_bytes=VMEM_LIMIT)


def _ln(x):
    mu = jnp.mean(x, axis=-1, keepdims=True)
    xc = x - mu
    var = jnp.mean(xc * xc, axis=-1, keepdims=True)
    return xc * lax.rsqrt(var + LN_EPS)


def _silu(x):
    return x * jax.nn.sigmoid(x)


def _row_tile(nb, seq, target):
    if seq >= target:
        assert seq % target == 0
        return 1, target
    tb = max(1, min(nb, target // seq))
    while nb % tb:
        tb -= 1
    return tb, seq


def _ada_kernel(c_ref, w_ref, b_ref, o_ref):
    a = _silu(c_ref[...]).astype(BF16)
    o_ref[0, 0] = jnp.dot(a, w_ref[0].astype(BF16), preferred_element_type=F32) + b_ref[0, 0]


def _ada_mod(c_all, ada_w, ada_b):
    depth = ada_w.shape[0]
    nb = c_all.shape[0]
    return pl.pallas_call(
        _ada_kernel,
        grid=(depth, 6),
        in_specs=[pl.BlockSpec((nb, D_MODEL), lambda l, s: (0, 0)),
                  pl.BlockSpec((1, D_MODEL, D_MODEL), lambda l, s: (l, 0, s)),
                  pl.BlockSpec((1, 1, 1, D_MODEL), lambda l, s: (l, s, 0, 0))],
        out_specs=pl.BlockSpec((1, 1, nb, D_MODEL), lambda l, s: (l, s, 0, 0)),
        out_shape=jax.ShapeDtypeStruct((depth, 6, nb, D_MODEL), F32),
        compiler_params=_cparams(("arbitrary", "arbitrary")),
        name="ada_mod",
    )(c_all, ada_w, ada_b.reshape(depth, 6, 1, D_MODEL))


def _inproj_kernel(x_ref, sh_ref, sc_ref, w_ref, q_ref, k_ref, v_ref, qkvb_ref, z_ref, ba_ref, kh_ref, vh_ref, *, tb, tl):
    h = _ln(x_ref[...]) * (1.0 + sc_ref[...]) + sh_ref[...]
    hb = h.reshape(tb * tl, D_MODEL).astype(BF16)
    outs = dict(q=q_ref, k=k_ref, v=v_ref, qkvb=qkvb_ref, z=z_ref, ba=ba_ref)
    heads = dict(k=kh_ref, v=vh_ref)
    for name, lo, hi in IN_SEGS:
        r = jnp.dot(hb, w_ref[:, lo:hi], preferred_element_type=F32)
        outs[name][...] = r.reshape(tb, tl, hi - lo).astype(outs[name].dtype)
        if name in heads:
            heads[name][...] = r.reshape(tb, tl, H_A, DV_A)


def _in_proj(x, sh, sc, w_perm, tm):
    nb, seq, _ = x.shape
    tb, tl = _row_tile(nb, seq, tm)
    grid = (nb // tb, seq // tl)
    row = lambda w: pl.BlockSpec((tb, tl, w), lambda i, j: (i, j, 0))
    per_head = pl.BlockSpec((tb, tl, H_A, DV_A), lambda i, j: (i, j, 0, 0))
    mod = pl.BlockSpec((tb, 1, D_MODEL), lambda i, j: (i, 0, 0))
    widths = [hi - lo for _, lo, hi in IN_SEGS]
    dtypes = [BF16, F32, F32, F32, F32, F32]
    return pl.pallas_call(
        functools.partial(_inproj_kernel, tb=tb, tl=tl),
        grid=grid,
        in_specs=[row(D_MODEL), mod, mod, pl.BlockSpec((D_MODEL, D_IN_PAD), lambda i, j: (0, 0))],
        out_specs=[row(w) for w in widths] + [per_head, per_head],
        out_shape=[jax.ShapeDtypeStruct((nb, seq, w), dt) for w, dt in zip(widths, dtypes)]
                  + [jax.ShapeDtypeStruct((nb, seq, H_A, DV_A), F32)] * 2,
        compiler_params=_cparams(("arbitrary", "arbitrary")),
        name="in_proj",
    )(x, sh, sc, w_perm)


def _lam_value(lam_ref, lam_init):
    lp = lam_ref[...]
    a = jnp.sum(lp[0:1] * lp[1:2], axis=-1, keepdims=True)
    b = jnp.sum(lp[2:3] * lp[3:4], axis=-1, keepdims=True)
    return jnp.exp(a) - jnp.exp(b) + lam_init


def _sub_ln(o, g, lam_init):
    y = o * lax.rsqrt(jnp.mean(o * o, axis=-1, keepdims=True) + 1e-6)
    return (y * g) * (1.0 - lam_init)


def _attn_kernel(q_ref, k_ref, v_ref, sl_ref, lam_ref, g_ref, o_ref, kb_ref, vt_ref, qt_ref, *acc_refs, tq, lam_init):
    qi = pl.program_id(2)
    ngrp = 2 * tq // LANES
    nblk = k_ref.shape[1] // tq

    @pl.when(qi == 0)
    def _():
        kb_ref[...] = k_ref[0].astype(BF16)
        for j in range(nblk):
            vt_ref[j] = v_ref[0, j * tq:(j + 1) * tq, :].T.astype(BF16)

    qs = q_ref[0].astype(F32) * (DQK_A ** -0.5)
    lane = lax.broadcasted_iota(jnp.int32, (tq, 2 * DQK_A), 1)
    q1 = jnp.where(lane < DQK_A, qs, 0.0)
    q2 = jnp.where(lane >= DQK_A, qs, 0.0)
    for g in range(ngrp // 2):
        rows = slice(g * LANES, (g + 1) * LANES)
        qt_ref[:, g * LANES:(g + 1) * LANES] = q1[rows, :].T.astype(BF16)
        qt_ref[:, (ngrp // 2 + g) * LANES:(ngrp // 2 + g + 1) * LANES] = q2[rows, :].T.astype(BF16)
    for g in range(ngrp):
        acc_refs[g][...] = jnp.zeros_like(acc_refs[g])

    slope = sl_ref[0]
    krow = lax.broadcasted_iota(jnp.int32, (tq, LANES), 0)
    qcol = lax.broadcasted_iota(jnp.int32, (tq, LANES), 1)
    qt = qt_ref[...]

    def step(j, stats, masked):
        start = pl.multiple_of(j * tq, tq)
        st = lax.dot_general(kb_ref[pl.ds(start, tq), :], qt, (((1,), (0,)), ((), ())),
                             preferred_element_type=F32)
        vt = vt_ref[j]
        bias = slope * (krow + (j - qi) * tq).astype(F32)
        new_stats = []
        for g in range(ngrp):
            m_prev, l_prev = stats[g]
            s = st[:, g * LANES:(g + 1) * LANES] + bias
            if masked:
                s = jnp.where(krow <= qcol + (g * LANES) % tq, s, NEG_INF)
            m_new = jnp.maximum(m_prev, jnp.max(s, axis=0, keepdims=True))
            a = jnp.exp(m_prev - m_new)
            p = jnp.exp(s - m_new)
            l_new = a * l_prev + jnp.sum(p, axis=0, keepdims=True)
            acc_refs[g][...] = a * acc_refs[g][...] + jnp.dot(vt, p.astype(BF16), preferred_element_type=F32)
            new_stats.append((m_new, l_new))
        return tuple(new_stats)

    init = tuple((jnp.full((1, LANES), NEG_INF, F32), jnp.zeros((1, LANES), F32)) for _ in range(ngrp))
    stats = lax.fori_loop(0, qi, lambda j, c: step(j, c, False), init)
    stats = step(qi, stats, True)

    lam = _lam_value(lam_ref, lam_init)
    half = ngrp // 2
    for g in range(half):
        o1 = acc_refs[g][...] / stats[g][1]
        o2 = acc_refs[half + g][...] / stats[half + g][1]
        od = o1 - lam * o2
        y = od * lax.rsqrt(jnp.mean(od * od, axis=0, keepdims=True) + 1e-6) * g_ref[...] * (1.0 - lam_init)
        o_ref[0, g * LANES:(g + 1) * LANES, :] = y.T.astype(o_ref.dtype)


def _attn_prompt(q, k, v, slopes, lam_p, subln_g, lam_init, tq):
    nb, seq, _ = q.shape
    tq = min(tq, seq)
    assert seq % tq == 0 and tq % LANES == 0 and DV_A == LANES
    grid = (nb, H_A, seq // tq)
    g_cols = jnp.broadcast_to(subln_g.reshape(DV_A, 1), (DV_A, LANES))
    return pl.pallas_call(
        functools.partial(_attn_kernel, tq=tq, lam_init=lam_init),
        grid=grid,
        in_specs=[pl.BlockSpec((1, tq, DV_A), lambda b, h, i: (b, i, h)),
                  pl.BlockSpec((1, seq, DV_A), lambda b, h, i: (b, 0, h)),
                  pl.BlockSpec((1, seq, DV_A), lambda b, h, i: (b, 0, h)),
                  pl.BlockSpec((1, 1, LANES), lambda b, h, i: (h, 0, 0)),
                  pl.BlockSpec((4, DQK_A), lambda b, h, i: (0, 0)),
                  pl.BlockSpec((DV_A, LANES), lambda b, h, i: (0, 0))],
        out_specs=pl.BlockSpec((1, tq, DV_A), lambda b, h, i: (b, i, h)),
        out_shape=jax.ShapeDtypeStruct((nb, seq, D_A), BF16),
        scratch_shapes=[pltpu.VMEM((seq, DV_A), BF16), pltpu.VMEM((seq // tq, DV_A, tq), BF16),
                        pltpu.VMEM((2 * DQK_A, 2 * tq), BF16)]
                       + [pltpu.VMEM((DV_A, LANES), F32)] * (2 * tq // LANES),
        compiler_params=_cparams(("arbitrary", "arbitrary", "arbitrary")),
        name="attn_prompt",
    )(q, k, v, slopes, lam_p, g_cols)


PAGES_PER_STEP = 16
PAGE_ROWS = PAGE * H_A
HEAD_SHIFT = 2


def _dg_nt(a, b):
    return lax.dot_general(a, b, (((1,), (1,)), ((), ())), preferred_element_type=F32)


def _decode_kernel(pt_ref, q_ref, kn_ref, vn_ref, lam_ref, g_ref, *rest, nq, past_len, lam_init, ngroups):
    kp_refs = rest[:PAGES_PER_STEP]
    vp_refs = rest[PAGES_PER_STEP:2 * PAGES_PER_STEP]
    o_ref = rest[2 * PAGES_PER_STEP]
    qr_ref, m_ref, l_ref, acc_ref, pad_k_ref, pad_v_ref = rest[2 * PAGES_PER_STEP + 1:]
    g = pl.program_id(1)
    rows = 2 * H_A * nq

    row = lax.broadcasted_iota(jnp.int32, (rows, 1), 0)
    hm = jnp.zeros_like(row)
    for t in range(1, 2 * H_A):
        hm = hm + (row >= t * nq).astype(jnp.int32)
    qpos = past_len + (row - nq * hm)
    rhead = jnp.zeros_like(row)
    for h in range(1, H_A):
        rhead = rhead + (row >= 2 * h * nq).astype(jnp.int32)
    slope = jnp.zeros((rows, 1), F32)
    for h in range(H_A):
        slope = jnp.where(rhead == h, 2.0 ** (-8.0 * (h + 1) / H_A), slope)

    @pl.when(g == 0)
    def _():
        lane = lax.broadcasted_iota(jnp.int32, (nq, 2 * DQK_A), 1)
        for h in range(H_A):
            qh = (q_ref[0, :, h * DV_A:(h + 1) * DV_A].astype(F32) * (DQK_A ** -0.5)).astype(BF16)
            zero = jnp.zeros_like(qh)
            qr_ref[(2 * h) * nq:(2 * h + 1) * nq, :] = jnp.where(lane < DQK_A, qh, zero)
            qr_ref[(2 * h + 1) * nq:(2 * h + 2) * nq, :] = jnp.where(lane >= DQK_A, qh, zero)
        m_ref[...] = jnp.full_like(m_ref, NEG_INF)
        l_ref[...] = jnp.zeros_like(l_ref)
        acc_ref[...] = jnp.zeros_like(acc_ref)

    qr = qr_ref[...]

    def update(kbs, vbs, base_pos, causal):
        n = len(kbs)
        ncol = n * PAGE_ROWS
        s = jnp.concatenate([_dg_nt(qr, kb) for kb in kbs], axis=1)
        col = lax.broadcasted_iota(jnp.int32, (1, ncol), 1)
        ctok = col >> HEAD_SHIFT
        kpos = base_pos + ctok
        valid = (col - (ctok << HEAD_SHIFT)) == rhead
        if causal:
            valid = valid & (kpos <= qpos) & (ctok < nq)
        s = jnp.where(valid, s - slope * (qpos - kpos).astype(F32), NEG_INF)
        m_prev = m_ref[...]
        m_new = jnp.maximum(m_prev, jnp.max(s, axis=-1, keepdims=True))
        a = jnp.exp(m_prev - m_new)
        p = jnp.exp(s - jnp.concatenate([m_new] * (ncol // LANES), axis=1)).astype(BF16)
        l_ref[...] = a * l_ref[...] + jnp.sum(p.astype(F32), axis=-1, keepdims=True)
        pv = jnp.dot(p[:, 0:PAGE_ROWS], vbs[0], preferred_element_type=F32)
        for i in range(1, n):
            pv = pv + jnp.dot(p[:, i * PAGE_ROWS:(i + 1) * PAGE_ROWS], vbs[i], preferred_element_type=F32)
        acc_ref[...] = a * acc_ref[...] + pv
        m_ref[...] = m_new

    update([r[...].astype(BF16) for r in kp_refs], [r[...].astype(BF16) for r in vp_refs],
           g * (PAGES_PER_STEP * PAGE), False)

    @pl.when(g == ngroups - 1)
    def _():
        pad_k_ref[...] = jnp.zeros_like(pad_k_ref)
        pad_v_ref[...] = jnp.zeros_like(pad_v_ref)
        pad_k_ref[0:nq * H_A, :] = kn_ref[0]
        pad_v_ref[0:nq * H_A, :] = vn_ref[0]
        update([pad_k_ref[...].astype(BF16)], [pad_v_ref[...].astype(BF16)], past_len, True)
        o = acc_ref[...] / l_ref[...]
        lam = _lam_value(lam_ref, lam_init)
        for h in range(H_A):
            o1 = o[(2 * h) * nq:(2 * h + 1) * nq, :]
            o2 = o[(2 * h + 1) * nq:(2 * h + 2) * nq, :]
            y = _sub_ln(o1 - lam * o2, g_ref[...], lam_init)
            o_ref[0, :, h * DV_A:(h + 1) * DV_A] = y.astype(o_ref.dtype)


def _attn_decode(q, k_new, v_new, cache_k, cache_v, layer, pool, page_table, lam_p, subln_g, lam_init):
    nb, nq, _ = q.shape
    n_pages = page_table.shape[1]
    assert n_pages % PAGES_PER_STEP == 0 and nq * H_A <= PAGE_ROWS and H_A == 1 << HEAD_SHIFT
    ngroups = n_pages // PAGES_PER_STEP
    past_len = n_pages * PAGE
    rows = 2 * H_A * nq

    def page_spec(i):
        return pl.BlockSpec((PAGE_ROWS, DV_A),
                            lambda b, g, pt: (layer * pool + pt[b, g * PAGES_PER_STEP + i], 0))

    per_b = lambda r, w: pl.BlockSpec((1, r, w), lambda b, g, pt: (b, 0, 0))
    grid_spec = pltpu.PrefetchScalarGridSpec(
        num_scalar_prefetch=1,
        grid=(nb, ngroups),
        in_specs=[per_b(nq, D_A), per_b(nq * H_A, DV_A), per_b(nq * H_A, DV_A),
                  pl.BlockSpec((4, DQK_A), lambda b, g, pt: (0, 0)),
                  pl.BlockSpec((1, DV_A), lambda b, g, pt: (0, 0))]
                 + [page_spec(i) for i in range(PAGES_PER_STEP)]
                 + [page_spec(i) for i in range(PAGES_PER_STEP)],
        out_specs=per_b(nq, D_A),
        scratch_shapes=[pltpu.VMEM((rows, DV_A), BF16), pltpu.VMEM((rows, LANES), F32), pltpu.VMEM((rows, LANES), F32),
                        pltpu.VMEM((rows, DV_A), F32), pltpu.VMEM((PAGE_ROWS, DV_A), F32),
                        pltpu.VMEM((PAGE_ROWS, DV_A), F32)],
    )
    return pl.pallas_call(
        functools.partial(_decode_kernel, nq=nq, past_len=past_len, lam_init=lam_init, ngroups=ngroups),
        grid_spec=grid_spec,
        out_shape=jax.ShapeDtypeStruct((nb, nq, D_A), BF16),
        compiler_params=_cparams(("arbitrary", "arbitrary")),
        name="attn_decode",
    )(page_table, q, k_new.reshape(nb, nq * H_A, DV_A), v_new.reshape(nb, nq * H_A, DV_A), lam_p,
      subln_g.reshape(1, DV_A), *([cache_k] * PAGES_PER_STEP), *([cache_v] * PAGES_PER_STEP))


def _gdn_step_kernel(x_ref, cb_ref, cw_ref, ba_ref, z_ref, s0_ref, hp_ref, ng_ref, o_ref, sn_ref, xx_ref, *, seq):
    width = 3 * D_B
    hist = SUBLANES
    xx_ref[0:hist, :] = jnp.zeros((hist, width), F32)
    xx_ref[hist - (CONV_W - 1):hist, :] = cb_ref[0]
    xx_ref[hist:hist + seq, :] = x_ref[0]
    cw = cw_ref[...]
    y = cw[CONV_W - 1:CONV_W] * xx_ref[hist:hist + seq, :]
    for i in range(CONV_W - 1):
        off = hist - (CONV_W - 1) + i
        y = y + cw[i:i + 1] * xx_ref[off:off + seq, :]
    y = _silu(y)

    def l2n(v):
        return v * lax.rsqrt(jnp.sum(v * v, axis=-1, keepdims=True) + 1e-6)

    ba = ba_ref[0]
    ng = ng_ref[...]
    for h in range(H_B):
        hc = slice(h * DK_B, (h + 1) * DK_B)
        q = l2n(y[:, h * DK_B:(h + 1) * DK_B]) * (DK_B ** -0.5)
        k = l2n(y[:, D_B + h * DK_B:D_B + (h + 1) * DK_B])
        v = y[:, 2 * D_B + h * DV_B:2 * D_B + (h + 1) * DV_B]
        beta = jax.nn.sigmoid(ba[:, h:h + 1])
        xsp = ba[:, H_B + h:H_B + h + 1] + hp_ref[h, 1:2, :]
        decay = jnp.exp(-jnp.exp(hp_ref[h, 0:1, :]) * (jnp.maximum(xsp, 0.0) + jnp.log1p(jnp.exp(-jnp.abs(xsp)))))
        kt = k.T
        qt = q.T
        bv = beta * v
        state = s0_ref[0, h]
        outs = []
        for t in range(seq):
            kcol = kt[:, t:t + 1]
            state = state * decay[t:t + 1, :]
            r = jnp.sum(kcol * state, axis=0, keepdims=True)
            state = state + kcol * (bv[t:t + 1, :] - beta[t:t + 1, :] * r)
            outs.append(jnp.sum(qt[:, t:t + 1] * state, axis=0, keepdims=True))
        o = jnp.concatenate(outs, axis=0)
        sn_ref[0, h] = state
        yo = o * lax.rsqrt(jnp.mean(o * o, axis=-1, keepdims=True) + 1e-6) * ng
        o_ref[0, :, hc] = (yo * _silu(z_ref[0, :, hc])).astype(o_ref.dtype)


def _gated_deltanet_step(qkvb, ba, z, conv_buf, s0, conv_w, head_params, norm_g):
    nb, seq, width = qkvb.shape
    row = lambda w: pl.BlockSpec((1, seq, w), lambda b: (b, 0, 0))
    st = pl.BlockSpec((1, H_B, DK_B, DV_B), lambda b: (b, 0, 0, 0))
    return pl.pallas_call(
        functools.partial(_gdn_step_kernel, seq=seq),
        grid=(nb,),
        in_specs=[row(width),
                  pl.BlockSpec((1, CONV_W - 1, width), lambda b: (b, 0, 0)),
                  pl.BlockSpec((CONV_W, width), lambda b: (0, 0)),
                  row(LANES), row(D_B), st,
                  pl.BlockSpec((H_B, 2, LANES), lambda b: (0, 0, 0)),
                  pl.BlockSpec((1, DV_B), lambda b: (0, 0))],
        out_specs=[row(D_B), st],
        out_shape=[jax.ShapeDtypeStruct((nb, seq, D_B), BF16),
                   jax.ShapeDtypeStruct((nb, H_B, DK_B, DV_B), F32)],
        scratch_shapes=[pltpu.VMEM((seq + SUBLANES, width), F32)],
        compiler_params=_cparams(("arbitrary",)),
        name="gated_deltanet_step",
    )(qkvb, conv_buf, conv_w, ba, z, s0, head_params, norm_g.reshape(1, DV_B))


GDN_TILE = 512
GDN_SUPER = 256


def _dg(a, b, dims):
    return lax.dot_general(a, b, (dims, ((), ())), preferred_element_type=F32)


_NN = ((1,), (0,))
_NT = ((1,), (1,))
_TN = ((0,), (0,))


def _mm_exact_lhs(t, b):
    b1 = b.astype(BF16)
    r1 = b - b1.astype(F32)
    b2 = r1.astype(BF16)
    b3 = (r1 - b2.astype(F32)).astype(BF16)
    return _dg(t, b1, _NN) + (_dg(t, b2, _NN) + _dg(t, b3, _NN))


def _prod(a, b):
    return _dg(a.astype(BF16), b.astype(BF16), _NN)


def _unit_lower_inverse_minus_eye(lmat, size, csz):
    r = lax.broadcasted_iota(jnp.int32, (size, size), 0)
    c = lax.broadcasted_iota(jnp.int32, (size, size), 1)
    ld = jnp.where((r // SUBLANES) == (c // SUBLANES), lmat, 0.0)
    p2 = _prod(ld, ld)
    d = (p2 - ld) - _prod(ld, p2)
    p4 = _prod(p2, p2)
    d = d + p4 + _prod(d, p4)
    e = lmat - ld
    n = e + _prod(d, e)
    out = -n
    pw = n
    k = 2
    while k < csz // SUBLANES:
        pw = _prod(pw, pw)
        out = out + pw + _prod(out, pw)
        k *= 2
    return out + d + _prod(out, d)


def _gdn_long_kernel(x_ref, cb_ref, cw_ref, ba_ref, z_ref, s0_ref, hp_ref, ng_ref, o_ref, sn_ref,
                     xx_ref, y_ref, st_ref, u_ref, w_ref, qd_ref, kd_ref, gt_ref, qk_ref, *, tl):
    t = pl.program_id(1)
    width = 3 * D_B
    hist = SUBLANES

    @pl.when(t == 0)
    def _():
        xx_ref[0:hist, :] = jnp.zeros((hist, width), F32)
        xx_ref[hist - (CONV_W - 1):hist, :] = cb_ref[0]
        st_ref[...] = s0_ref[0]

    @pl.when(t > 0)
    def _():
        xx_ref[0:hist, :] = xx_ref[tl:tl + hist, :]

    xx_ref[hist:hist + tl, :] = x_ref[0]

    cw = cw_ref[...]
    blk = 256
    for r0 in range(0, tl, blk):
        y = cw[CONV_W - 1:CONV_W] * xx_ref[hist + r0:hist + r0 + blk, :]
        for i in range(CONV_W - 1):
            off = hist - (CONV_W - 1) + i + r0
            y = y + cw[i:i + 1] * xx_ref[off:off + blk, :]
        y_ref[r0:r0 + blk, :] = _silu(y)

    sup = GDN_SUPER
    r = lax.broadcasted_iota(jnp.int32, (sup, sup), 0)
    c = lax.broadcasted_iota(jnp.int32, (sup, sup), 1)
    same_chunk = (r // CHUNK) == (c // CHUNK)
    tril = same_chunk & (r >= c)
    strict = same_chunk & (r > c)
    diag = r == c
    sum_mat = jnp.concatenate([tril.astype(BF16), same_chunk.astype(BF16)], axis=0)

    def l2n(v):
        return v * lax.rsqrt(jnp.sum(v * v, axis=-1, keepdims=True) + 1e-6)

    for sc in range(tl // sup):
        rs = slice(sc * sup, (sc + 1) * sup)
        ba = ba_ref[0, rs, :]
        for h in range(H_B):
            hc = slice(h * DK_B, (h + 1) * DK_B)
            q = l2n(y_ref[rs, h * DK_B:(h + 1) * DK_B]) * (DK_B ** -0.5)
            k = l2n(y_ref[rs, D_B + h * DK_B:D_B + (h + 1) * DK_B])
            v = y_ref[rs, 2 * D_B + h * DV_B:2 * D_B + (h + 1) * DV_B]
            beta = jax.nn.sigmoid(ba[:, h:h + 1])
            xsp = ba[:, H_B + h:H_B + h + 1] + hp_ref[h, 1:2, :]
            gl = -jnp.exp(hp_ref[h, 0:1, :]) * (jnp.maximum(xsp, 0.0) + jnp.log1p(jnp.exp(-jnp.abs(xsp))))
            sums = _mm_exact_lhs(sum_mat, gl)
            gcum, gtot = sums[0:sup], sums[sup:2 * sup]
            gi = jnp.concatenate([gcum] * (sup // LANES), axis=1)
            gj = jnp.sum(jnp.where(diag, gi, 0.0), axis=0, keepdims=True)
            decay = jnp.where(tril, jnp.exp(jnp.where(tril, gi - gj, 0.0)), 0.0)
            eg = jnp.exp(gcum)
            kb = k * beta
            kbf = k.astype(BF16)
            lmat = jnp.where(strict, _dg(kb.astype(BF16), kbf, _NT) * decay, 0.0)
            rhs = jnp.concatenate([v * beta, kb * eg], axis=1)
            uw = rhs + _prod(_unit_lower_inverse_minus_eye(lmat, sup, CHUNK), rhs)
            u_ref[rs, hc] = uw[:, 0:DV_B]
            w_ref[rs, hc] = uw[:, DV_B:DV_B + DK_B]
            qk = _dg(q.astype(BF16), kbf, _NT) * decay
            qkc = qk[:, 0:CHUNK]
            for i in range(1, sup // CHUNK):
                qkc = qkc + qk[:, i * CHUNK:(i + 1) * CHUNK]
            qk_ref[h, rs, :] = qkc
            qd_ref[rs, hc] = q * eg
            kd_ref[rs, hc] = k * jnp.exp(gtot - gcum)
            gt_ref[rs, hc] = jnp.exp(gtot)

    ng = ng_ref[...]

    def chunk(ci, carry):
        start = ci * CHUNK
        rows = pl.ds(start, CHUNK)
        for h in range(H_B):
            hc = slice(h * DK_B, (h + 1) * DK_B)
            sb = st_ref[h].astype(BF16)
            v_new = u_ref[rows, hc] - _dg(w_ref[rows, hc].astype(BF16), sb, _NN)
            vb = v_new.astype(BF16)
            o = _dg(qd_ref[rows, hc].astype(BF16), sb, _NN) + _dg(qk_ref[h, rows, :].astype(BF16), vb, _NN)
            st_ref[h] = st_ref[h] * gt_ref[pl.ds(start, 1), hc] + _dg(kd_ref[rows, hc].astype(BF16), vb, _TN)
            y = o * lax.rsqrt(jnp.mean(o * o, axis=-1, keepdims=True) + 1e-6) * ng
            o_ref[0, rows, hc] = (y * _silu(z_ref[0, rows, hc])).astype(o_ref.dtype)
        return carry

    for ci in range(tl // CHUNK):
        chunk(ci, 0)
    sn_ref[0] = st_ref[...]


def _gated_deltanet_long(qkvb, ba, z, conv_buf, s0, conv_w, head_params, norm_g):
    nb, seq, width = qkvb.shape
    tl = GDN_TILE
    assert seq % tl == 0 and tl % GDN_SUPER == 0
    row = lambda w: pl.BlockSpec((1, tl, w), lambda b, t: (b, t, 0))
    st = pl.BlockSpec((1, H_B, DK_B, DV_B), lambda b, t: (b, 0, 0, 0))
    scr = lambda n, w: pltpu.VMEM((n, w), F32)
    return pl.pallas_call(
        functools.partial(_gdn_long_kernel, tl=tl),
        grid=(nb, seq // tl),
        in_specs=[row(width),
                  pl.BlockSpec((1, CONV_W - 1, width), lambda b, t: (b, 0, 0)),
                  pl.BlockSpec((CONV_W, width), lambda b, t: (0, 0)),
                  row(LANES), row(D_B), st,
                  pl.BlockSpec((H_B, 2, LANES), lambda b, t: (0, 0, 0)),
                  pl.BlockSpec((1, DV_B), lambda b, t: (0, 0))],
        out_specs=[row(D_B), st],
        out_shape=[jax.ShapeDtypeStruct((nb, seq, D_B), BF16),
                   jax.ShapeDtypeStruct((nb, H_B, DK_B, DV_B), F32)],
        scratch_shapes=[scr(tl + SUBLANES, width), scr(tl, width), pltpu.VMEM((H_B, DK_B, DV_B), F32),
                        scr(tl, D_B), scr(tl, D_B), scr(tl, D_B), scr(tl, D_B), scr(tl, D_B),
                        pltpu.VMEM((H_B, tl, CHUNK), F32)],
        compiler_params=_cparams(("arbitrary", "arbitrary")),
        name="gated_deltanet_long",
    )(qkvb, conv_buf, conv_w, ba, z, s0, head_params, norm_g.reshape(1, DV_B))


def _outproj_kernel(oa_ref, ob_ref, x_ref, w_ref, g1_ref, lg_ref, lb_ref, sh_ref, sc_ref, *rest, tb, tl, alpha, region):
    router = region is not None
    if router:
        wr_ref, br_ref, x1_ref, h2_ref, pair_ref, cnt_ref, run_ref = rest
    else:
        x1_ref, h2_ref = rest
    rows = tb * tl
    oa = oa_ref[...].reshape(rows, D_A)
    ob = ob_ref[...].reshape(rows, D_B)
    mix = (jnp.dot(oa, w_ref[0:D_A, :], preferred_element_type=F32)
           + jnp.dot(ob, w_ref[D_A:D_A + D_B, :], preferred_element_type=F32))
    y = alpha * x_ref[...] + g1_ref[...] * mix.reshape(tb, tl, D_MODEL)
    x1 = _ln(y) * lg_ref[...] + lb_ref[...]
    x1_ref[...] = x1
    h2 = (_ln(x1) * (1.0 + sc_ref[...]) + sh_ref[...]).astype(BF16)
    h2_ref[...] = h2
    if router:
        @pl.when((pl.program_id(0) == 0) & (pl.program_id(1) == 0))
        def _():
            run_ref[...] = jnp.zeros_like(run_ref)

        logits = jnp.dot(h2.reshape(rows, D_MODEL), wr_ref[...], preferred_element_type=F32) + br_ref[...]
        lane = lax.broadcasted_iota(jnp.int32, (rows, LANES), 1)
        logits = jnp.where(lane < N_EXPERTS, logits, NEG_INF)
        v1 = jnp.max(logits, axis=-1, keepdims=True)
        i1 = jnp.min(jnp.where(logits == v1, lane, LANES), axis=-1, keepdims=True)
        rest_l = jnp.where(lane == i1, NEG_INF, logits)
        v2 = jnp.max(rest_l, axis=-1, keepdims=True)
        i2 = jnp.min(jnp.where(rest_l == v2, lane, LANES), axis=-1, keepdims=True)
        e2 = jnp.exp(v2 - v1)
        gate1 = 1.0 / (1.0 + e2)
        gate2 = e2 / (1.0 + e2)
        sel = jnp.where((lane == i1) | (lane == i2), 1.0, 0.0)
        r = lax.broadcasted_iota(jnp.int32, (rows, rows), 0)
        c = lax.broadcasted_iota(jnp.int32, (rows, rows), 1)
        earlier = jnp.where(r > c, 1.0, 0.0).astype(BF16)
        rank = jnp.dot(earlier, sel.astype(BF16), preferred_element_type=F32) + run_ref[...]
        run_ref[...] = run_ref[...] + jnp.sum(sel, axis=0, keepdims=True)
        slot = lane.astype(F32) * float(region) + rank
        first, second = jnp.minimum(i1, i2), jnp.maximum(i1, i2)
        slot_a = jnp.sum(jnp.where(lane == first, slot, 0.0), axis=-1, keepdims=True)
        slot_b = jnp.sum(jnp.where(lane == second, slot, 0.0), axis=-1, keepdims=True)
        gate_a = jnp.where(first == i1, gate1, gate2)
        gate_b = jnp.where(first == i1, gate2, gate1)
        pair = (jnp.where(lane == 0, slot_a, 0.0) + jnp.where(lane == 1, slot_b, 0.0)
                + jnp.where(lane == 2, gate_a, 0.0) + jnp.where(lane == 3, gate_b, 0.0))
        pair_ref[...] = pair.reshape(tb, tl, LANES)
        cnt_ref[...] = run_ref[...]


def _out_proj(oa, ob, x, w_out_b, g1, ln_g, ln_b, sh2, sc2, alpha, tm, router_w=None, router_b=None, region=None):
    nb, seq, _ = x.shape
    tb, tl = _row_tile(nb, seq, tm)
    grid = (nb // tb, seq // tl)
    row = lambda w: pl.BlockSpec((tb, tl, w), lambda i, j: (i, j, 0))
    mod = pl.BlockSpec((tb, 1, D_MODEL), lambda i, j: (i, 0, 0))
    vec = pl.BlockSpec((1, 1, D_MODEL), lambda i, j: (0, 0, 0))
    in_specs = [row(D_A), row(D_B), row(D_MODEL), pl.BlockSpec((D_MODEL, D_MODEL), lambda i, j: (0, 0)),
                mod, vec, vec, mod, mod]
    args = [oa, ob, x, w_out_b, g1, ln_g.reshape(1, 1, D_MODEL), ln_b.reshape(1, 1, D_MODEL), sh2, sc2]
    out_specs = [row(D_MODEL), row(D_MODEL)]
    out_shape = [jax.ShapeDtypeStruct((nb, seq, D_MODEL), F32), jax.ShapeDtypeStruct((nb, seq, D_MODEL), BF16)]
    scratch = []
    if region is not None:
        in_specs += [pl.BlockSpec((D_MODEL, LANES), lambda i, j: (0, 0)), pl.BlockSpec((1, LANES), lambda i, j: (0, 0))]
        args += [router_w, router_b]
        out_specs += [row(LANES), pl.BlockSpec((1, LANES), lambda i, j: (0, 0))]
        out_shape += [jax.ShapeDtypeStruct((nb, seq, LANES), F32), jax.ShapeDtypeStruct((1, LANES), F32)]
        scratch = [pltpu.VMEM((1, LANES), F32)]
    return pl.pallas_call(
        functools.partial(_outproj_kernel, tb=tb, tl=tl, alpha=alpha, region=region),
        grid=grid, in_specs=in_specs, out_specs=out_specs, out_shape=out_shape, scratch_shapes=scratch,
        compiler_params=_cparams(("arbitrary", "arbitrary")),
        name="out_proj",
    )(*args)


def _ffn_kernel(h_ref, x_ref, wg_ref, wu_ref, wd_ref, g2_ref, lg_ref, lb_ref, o_ref, acc_ref, *, tb, tl, alpha):
    f = pl.program_id(2)
    rows = tb * tl

    @pl.when(f == 0)
    def _():
        acc_ref[...] = jnp.zeros_like(acc_ref)

    hb = h_ref[...].reshape(rows, D_MODEL)
    gt = jnp.dot(hb, wg_ref[...], preferred_element_type=F32)
    up = jnp.dot(hb, wu_ref[...], preferred_element_type=F32)
    a = (_silu(gt) * up).astype(BF16)
    acc_ref[...] += jnp.dot(a, wd_ref[...], preferred_element_type=F32)

    @pl.when(f == pl.num_programs(2) - 1)
    def _():
        y = alpha * x_ref[...] + g2_ref[...] * acc_ref[...].reshape(tb, tl, D_MODEL)
        o_ref[...] = _ln(y) * lg_ref[...] + lb_ref[...]


def _ffn(h2, x1, w_up_b, w_down_b, g2, ln_g, ln_b, alpha, tm, tf):
    nb, seq, _ = x1.shape
    tb, tl = _row_tile(nb, seq, tm)
    nf = D_FF // tf
    grid = (nb // tb, seq // tl, nf)
    row = lambda w: pl.BlockSpec((tb, tl, w), lambda i, j, f: (i, j, 0))
    return pl.pallas_call(
        functools.partial(_ffn_kernel, tb=tb, tl=tl, alpha=alpha),
        grid=grid,
        in_specs=[row(D_MODEL), row(D_MODEL),
                  pl.BlockSpec((D_MODEL, tf), lambda i, j, f: (0, f)),
                  pl.BlockSpec((D_MODEL, tf), lambda i, j, f: (0, nf + f)),
                  pl.BlockSpec((tf, D_MODEL), lambda i, j, f: (f, 0)),
                  pl.BlockSpec((tb, 1, D_MODEL), lambda i, j, f: (i, 0, 0)),
                  pl.BlockSpec((1, 1, D_MODEL), lambda i, j, f: (0, 0, 0)),
                  pl.BlockSpec((1, 1, D_MODEL), lambda i, j, f: (0, 0, 0))],
        out_specs=row(D_MODEL),
        out_shape=jax.ShapeDtypeStruct((nb, seq, D_MODEL), F32),
        scratch_shapes=[pltpu.VMEM((tb * tl, D_MODEL), F32)],
        compiler_params=_cparams(("arbitrary", "arbitrary", "arbitrary")),
        name="ffn",
    )(h2, x1, w_up_b, w_up_b, w_down_b, g2, ln_g.reshape(1, 1, D_MODEL), ln_b.reshape(1, 1, D_MODEL))


MOE_ROWS = 512
ROW_TILE = (SUBLANES, LANES)
ISSUE_UNROLL = 8


def _region_rows(m):
    return -(-m // MOE_ROWS) * MOE_ROWS


def _dispatch_kernel(slot_ref, pad_ref, h_ref, xs_ref, src_ref, zero_ref, sem, *, rows, npad):
    @pl.when((pl.program_id(0) == 0) & (pl.program_id(1) == 0))
    def _():
        zero_ref[...] = jnp.zeros_like(zero_ref)

        def pad_copy(k):
            return pltpu.make_async_copy(zero_ref.at[0], xs_ref.at[pad_ref[0, k]], sem)

        def issue_pad(k, c):
            pad_copy(k).start()
            return c

        def drain_pad(k, c):
            pad_copy(0).wait()
            return c

        lax.fori_loop(0, npad, issue_pad, 0)
        lax.fori_loop(0, npad, drain_pad, 0)

    src_ref[...] = h_ref[...].reshape(rows, D_MODEL).astype(F32).reshape(rows, *ROW_TILE)

    def row_copy(t, k):
        return pltpu.make_async_copy(src_ref.at[t], xs_ref.at[slot_ref[k, t]], sem)

    def issue(i, c):
        for u in range(ISSUE_UNROLL):
            t = i * ISSUE_UNROLL + u
            row_copy(t, 0).start()
            row_copy(t, 1).start()
        return c

    lax.fori_loop(0, rows // ISSUE_UNROLL, issue, 0)

    def drain(i, c):
        row_copy(0, 0).wait()
        row_copy(0, 1).wait()
        return c

    lax.fori_loop(0, rows, drain, 0)


def _dispatch(h2, slots, pad_slots, total_rows):
    nb, seq, _ = h2.shape
    tb, tl = _row_tile(nb, seq, MOE_ROWS)
    rows = tb * tl
    nj = seq // tl
    npad = pad_slots.shape[1]
    return pl.pallas_call(
        functools.partial(_dispatch_kernel, rows=rows, npad=npad),
        grid=(nb // tb, nj),
        in_specs=[pl.BlockSpec((2, rows), lambda i, j: (0, i * nj + j), memory_space=pltpu.SMEM),
                  pl.BlockSpec((1, npad), lambda i, j: (0, 0), memory_space=pltpu.SMEM),
                  pl.BlockSpec((tb, tl, D_MODEL), lambda i, j: (i, j, 0))],
        out_specs=pl.BlockSpec(memory_space=pl.ANY),
        out_shape=jax.ShapeDtypeStruct((total_rows, *ROW_TILE), F32),
        scratch_shapes=[pltpu.VMEM((rows, *ROW_TILE), F32), pltpu.VMEM((1, *ROW_TILE), F32),
                        pltpu.SemaphoreType.DMA(())],
        compiler_params=_cparams(("arbitrary", "arbitrary")),
        name="moe_dispatch",
    )(slots, pad_slots, h2)


def _experts_kernel(blk_e_ref, blk_rows_ref, nused_ref, x_ref, wg_ref, wu_ref, wd_ref, o_ref, xb_ref, acc_ref):
    i, f = pl.program_id(0), pl.program_id(1)
    nf = pl.num_programs(1)
    nrows = blk_rows_ref[i]
    active = nrows > 0

    @pl.when(active & (f == 0))
    def _():
        x = x_ref[...].reshape(MOE_ROWS, D_MODEL)
        row = lax.broadcasted_iota(jnp.int32, (MOE_ROWS, 1), 0)
        xb_ref[...] = jnp.where(row < nrows, x, 0.0).astype(BF16)
        acc_ref[...] = jnp.zeros_like(acc_ref)

    @pl.when(active)
    def _():
        xb = xb_ref[...]
        gt = jnp.dot(xb, wg_ref[0], preferred_element_type=F32)
        up = jnp.dot(xb, wu_ref[0], preferred_element_type=F32)
        a = (_silu(gt) * up).astype(BF16)
        acc_ref[...] += jnp.dot(a, wd_ref[0], preferred_element_type=F32)

    @pl.when(active & (f == nf - 1))
    def _():
        o_ref[...] = acc_ref[...].reshape(MOE_ROWS, *ROW_TILE)

    @pl.when(jnp.logical_not(active) & (f == nf - 1))
    def _():
        o_ref[...] = jnp.zeros_like(o_ref)


def _experts(xs, blk_e, blk_rows, nused, w_up_b, w_down_b, tf):
    nblk = xs.shape[0] // MOE_ROWS
    nf = D_FF // tf

    def xi(i, f, be, br, nu):
        return jnp.minimum(i, nu[0] - 1)

    def ei(i, f, be, br, nu):
        return be[jnp.minimum(i, nu[0] - 1)]

    def fi(i, f, be, br, nu):
        return jnp.where(br[i] > 0, f, nf - 1)

    grid_spec = pltpu.PrefetchScalarGridSpec(
        num_scalar_prefetch=3,
        grid=(nblk, nf),
        in_specs=[pl.BlockSpec((MOE_ROWS, *ROW_TILE), lambda *a: (xi(*a), 0, 0)),
                  pl.BlockSpec((1, D_MODEL, tf), lambda *a: (ei(*a), 0, fi(*a))),
                  pl.BlockSpec((1, D_MODEL, tf), lambda *a: (ei(*a), 0, nf + fi(*a))),
                  pl.BlockSpec((1, tf, D_MODEL), lambda *a: (ei(*a), fi(*a), 0))],
        out_specs=pl.BlockSpec((MOE_ROWS, *ROW_TILE), lambda i, f, be, br, nu: (i, 0, 0)),
        scratch_shapes=[pltpu.VMEM((MOE_ROWS, D_MODEL), BF16), pltpu.VMEM((MOE_ROWS, D_MODEL), F32)],
    )
    return pl.pallas_call(
        _experts_kernel, grid_spec=grid_spec,
        out_shape=jax.ShapeDtypeStruct(xs.shape, F32),
        compiler_params=_cparams(("arbitrary", "arbitrary")),
        name="moe_experts",
    )(blk_e, blk_rows, nused, xs, w_up_b, w_up_b, w_down_b)


def _moe_layout(pair, cnt, m, region):
    raw = pair[..., 0:2].reshape(m, 2).astype(jnp.int32)
    e = raw // region
    counts = cnt[0, :N_EXPERTS].astype(jnp.int32)
    padded = (counts + MOE_ROWS - 1) // MOE_ROWS * MOE_ROWS
    off = jnp.cumsum(padded) - padded
    slots = (off[e] + (raw - e * region)).T
    nblk = (2 * m + MOE_ROWS - 1) // MOE_ROWS + N_EXPERTS
    bstart = jnp.arange(nblk, dtype=jnp.int32) * MOE_ROWS
    blk_e = jnp.minimum(jnp.sum((bstart[:, None] >= (off + padded)[None, :]).astype(jnp.int32), axis=1), N_EXPERTS - 1)
    blk_rows = jnp.clip(counts[blk_e] - (bstart - off[blk_e]), 0, MOE_ROWS)
    nused = (jnp.sum(padded) // MOE_ROWS).reshape(1)
    npad = nblk * MOE_ROWS - 2 * m
    run_start = jnp.concatenate([off + counts, jnp.sum(padded).reshape(1)])
    run_len = jnp.concatenate([padded - counts, (nblk * MOE_ROWS - jnp.sum(padded)).reshape(1)])
    run_end = jnp.cumsum(run_len)
    k = jnp.arange(npad, dtype=jnp.int32)[:, None]
    in_run = (k >= (run_end - run_len)[None, :]) & (k < run_end[None, :])
    pad_slots = jnp.sum(jnp.where(in_run, run_start[None, :] + k - (run_end - run_len)[None, :], 0), axis=1)
    return slots, pad_slots.astype(jnp.int32).reshape(1, npad), blk_e, blk_rows, nused


def _combine_kernel(slot_ref, ys_ref, pair_ref, x_ref, g2_ref, lg_ref, lb_ref, o_ref, buf0, buf1, sem, *, tb, tl, alpha):
    rows = tb * tl

    def row_copy(t, k):
        return pltpu.make_async_copy(ys_ref.at[slot_ref[k, t]], (buf0 if k == 0 else buf1).at[t], sem)

    def issue(i, c):
        for u in range(ISSUE_UNROLL):
            t = i * ISSUE_UNROLL + u
            row_copy(t, 0).start()
            row_copy(t, 1).start()
        return c

    lax.fori_loop(0, rows // ISSUE_UNROLL, issue, 0)

    def drain(i, c):
        row_copy(0, 0).wait()
        row_copy(0, 1).wait()
        return c

    lax.fori_loop(0, rows, drain, 0)
    pair = pair_ref[...].reshape(rows, LANES)
    ga, gb = pair[:, 2:3], pair[:, 3:4]
    f = ga * buf0[...].reshape(rows, D_MODEL) + gb * buf1[...].reshape(rows, D_MODEL)
    y = alpha * x_ref[...] + g2_ref[...] * f.reshape(tb, tl, D_MODEL)
    o_ref[...] = _ln(y) * lg_ref[...] + lb_ref[...]


def _combine(ys, slots, pair, x1, g2, ln_g, ln_b, alpha):
    nb, seq, _ = x1.shape
    tb, tl = _row_tile(nb, seq, MOE_ROWS)
    rows = tb * tl
    nj = seq // tl
    row = lambda w: pl.BlockSpec((tb, tl, w), lambda i, j: (i, j, 0))
    return pl.pallas_call(
        functools.partial(_combine_kernel, tb=tb, tl=tl, alpha=alpha),
        grid=(nb // tb, nj),
        in_specs=[pl.BlockSpec((2, rows), lambda i, j: (0, i * nj + j), memory_space=pltpu.SMEM),
                  pl.BlockSpec(memory_space=pl.ANY), row(LANES), row(D_MODEL),
                  pl.BlockSpec((tb, 1, D_MODEL), lambda i, j: (i, 0, 0)),
                  pl.BlockSpec((1, 1, D_MODEL), lambda i, j: (0, 0, 0)),
                  pl.BlockSpec((1, 1, D_MODEL), lambda i, j: (0, 0, 0))],
        out_specs=row(D_MODEL),
        out_shape=jax.ShapeDtypeStruct((nb, seq, D_MODEL), F32),
        scratch_shapes=[pltpu.VMEM((rows, *ROW_TILE), F32), pltpu.VMEM((rows, *ROW_TILE), F32),
                        pltpu.SemaphoreType.DMA(())],
        compiler_params=_cparams(("arbitrary", "arbitrary")),
        name="moe_combine",
    )(slots, ys, pair, x1, g2, ln_g.reshape(1, 1, D_MODEL), ln_b.reshape(1, 1, D_MODEL))


def _trunk(x, mod, conv_bufs, delta_states, past, weights, tm):
    depth = weights["w_in"].shape[0]
    alpha = (2 * depth) ** 0.25
    nb, seq, _ = x.shape
    new_k, new_v, new_conv, new_delta = [], [], [], []
    slopes = jnp.broadcast_to(
        jnp.asarray([2.0 ** (-8.0 * (h + 1) / H_A) for h in range(H_A)], F32)[:, None, None], (H_A, 1, LANES))
    for l in range(depth):
        sh1, sc1, g1, sh2, sc2, g2 = [mod[l, s] for s in range(6)]
        lam_init = 0.8 - 0.6 * math.exp(-0.3 * l)
        q, k, v, qkvb, z, ba, k_heads, v_heads = _in_proj(x, sh1, sc1, weights["w_in"][l], tm)
        new_k.append(k_heads)
        new_v.append(v_heads)
        if past is None:
            o_a = _attn_prompt(q, k, v, slopes, weights["lam_params"][l], weights["subln_a"][l], lam_init, 512)
        else:
            cache_k, cache_v, pool, page_table = past
            o_a = _attn_decode(q, k, v, cache_k, cache_v, l, pool, page_table,
                               weights["lam_params"][l], weights["subln_a"][l], lam_init)
        gdn = _gated_deltanet_long if seq % GDN_TILE == 0 else _gated_deltanet_step
        o_b, s_new = gdn(qkvb, ba, z, conv_bufs[l], delta_states[l], weights["conv_w"][l],
                         weights["head_params"][l], weights["norm_b"][l])
        new_conv.append(qkvb[:, seq - (CONV_W - 1):, :] if seq >= CONV_W - 1 else
                        jnp.concatenate([conv_bufs[l], qkvb], axis=1)[:, -(CONV_W - 1):, :])
        new_delta.append(s_new)
        j = l // 2
        if l % 2 == 0:
            x1, h2 = _out_proj(o_a, o_b, x, weights["w_out"][l], g1, weights["ln1_g"][l], weights["ln1_b"][l],
                               sh2, sc2, alpha, tm)
            x = _ffn(h2, x1, weights["dense_w_up"][j], weights["dense_w_down"][j], g2,
                     weights["ln2_g"][l], weights["ln2_b"][l], alpha, tm, D_FF // 2)
        else:
            region = _region_rows(nb * seq)
            x1, h2, pair, cnt = _out_proj(o_a, o_b, x, weights["w_out"][l], g1, weights["ln1_g"][l],
                                          weights["ln1_b"][l], sh2, sc2, alpha, MOE_ROWS,
                                          weights["router_w"][j], weights["router_b"][j], region)
            slots, pad_slots, blk_e, blk_rows, nused = _moe_layout(pair, cnt, nb * seq, region)
            xs = _dispatch(h2, slots, pad_slots, blk_e.shape[0] * MOE_ROWS)
            ys = _experts(xs, blk_e, blk_rows, nused, weights["moe_w_up"][j], weights["moe_w_down"][j], D_FF // 2)
            x = _combine(ys, slots, pair, x1, g2, weights["ln2_g"][l], weights["ln2_b"][l], alpha)
    return x, jnp.stack(new_k), jnp.stack(new_v), jnp.stack(new_conv), jnp.stack(new_delta)


def kernel(x_prompt, x_sample, cache_k, cache_v, state_conv, state_delta, page_table, c_prompt, c_sample,
           w_in, w_out, lam_params, subln_a, conv_w, a_log, dt_bias, norm_b, ada_w, ada_b,
           ln1_g, ln1_b, ln2_g, ln2_b, dense_w_up, dense_w_down, moe_router, moe_router_b, moe_w_up, moe_w_down):
    depth = w_in.shape[0]
    nbp, nbs = x_prompt.shape[0], x_sample.shape[0]
    a_q, a_k, a_v, b_qkv = 2 * H_A * DQK_A, 2 * H_A * DQK_A, H_A * DV_A, 3 * D_B
    n0 = a_q + a_k + a_v + b_qkv
    w_perm = jnp.concatenate(
        [w_in[:, :, :n0], w_in[:, :, n0 + 2 * H_B:], w_in[:, :, n0:n0 + 2 * H_B],
         jnp.zeros((depth, D_MODEL, LANES - 2 * H_B), w_in.dtype)], axis=-1).astype(BF16)
    n_moe = moe_router.shape[0]
    router_w = jnp.concatenate([moe_router, jnp.zeros((n_moe, D_MODEL, LANES - N_EXPERTS), moe_router.dtype)],
                               axis=-1).astype(BF16)
    router_b = jnp.concatenate([moe_router_b, jnp.zeros((n_moe, LANES - N_EXPERTS), moe_router_b.dtype)],
                               axis=-1).reshape(n_moe, 1, LANES)
    head_params = jnp.broadcast_to(jnp.stack([a_log, dt_bias], axis=-1)[..., None], (depth, H_B, 2, LANES))
    weights = dict(w_in=w_perm, w_out=w_out.astype(BF16), lam_params=lam_params, subln_a=subln_a, conv_w=conv_w,
                   head_params=head_params, norm_b=norm_b, ln1_g=ln1_g, ln1_b=ln1_b, ln2_g=ln2_g, ln2_b=ln2_b,
                   dense_w_up=dense_w_up.astype(BF16), dense_w_down=dense_w_down.astype(BF16),
                   router_w=router_w, router_b=router_b,
                   moe_w_up=moe_w_up.astype(BF16), moe_w_down=moe_w_down.astype(BF16))

    mod = _ada_mod(jnp.concatenate([c_prompt, c_sample], axis=0), ada_w, ada_b)
    mod = mod[:, :, :, None, :]
    conv0 = jnp.zeros((depth, nbp, CONV_W - 1, 3 * D_B), x_prompt.dtype)
    delta0 = jnp.zeros((depth, nbp, H_B, DK_B, DV_B), x_prompt.dtype)
    y_p, k_p, v_p, conv_p, delta_p = _trunk(x_prompt, mod[:, :, :nbp], conv0, delta0, None, weights, 512)

    n_pool = cache_k.shape[1]
    ck = cache_k.reshape(depth * n_pool * PAGE * H_A, 2 * DQK_A)
    cv = cache_v.reshape(depth * n_pool * PAGE * H_A, DV_A)
    y_s, k_s, v_s, conv_s, delta_s = _trunk(x_sample, mod[:, :, nbp:], state_conv, state_delta,
                                            (ck, cv, n_pool, page_table), weights, 512)
    return (y_p, y_s, k_p, v_p, conv_p, delta_p, k_s, v_s, conv_s, delta_s)
```

```python
import functools
import math

import jax
import jax.numpy as jnp
from jax import lax
from jax.experimental import pallas as pl
from jax.experimental.pallas import tpu as pltpu

F32 = jnp.float32
BF16 = jnp.bfloat16

D_MODEL = 1024
H_A = 4
DV_A = 128
DQK_A = 64
D_A = H_A * DV_A
H_B = 4
DK_B = 128
DV_B = 128
D_B = H_B * DV_B
CONV_W = 4
CHUNK = 64
D_FF = 2816
N_EXPERTS = 8
LN_EPS = 1e-5
PAGE = 128
LANES = 128
SUBLANES = 8
VMEM_LIMIT = 56 * 1024 * 1024
NEG_INF = float("-inf")

IN_SEGS = (("q", 0, 512), ("k", 512, 1024), ("v", 1024, 1536), ("qkvb", 1536, 3072),
           ("z", 3072, 3584), ("ba", 3584, 3712))
D_IN_PAD = 3712


def _cparams(sem):
    return pltpu.CompilerParams(dimension_semantics=sem, vmem_limit_bytes=VMEM_LIMIT)


def _ln(x):
    mu = jnp.mean(x, axis=-1, keepdims=True)
    xc = x - mu
    var = jnp.mean(xc * xc, axis=-1, keepdims=True)
    return xc * lax.rsqrt(var + LN_EPS)


def _silu(x):
    return x * jax.nn.sigmoid(x)


def _row_tile(nb, seq, target):
    if seq >= target:
        assert seq % target == 0
        return 1, target
    tb = max(1, min(nb, target // seq))
    while nb % tb:
        tb -= 1
    return tb, seq


def _ada_kernel(c_ref, w_ref, b_ref, o_ref):
    a = _silu(c_ref[...]).astype(BF16)
    o_ref[0, 0] = jnp.dot(a, w_ref[0].astype(BF16), preferred_element_type=F32) + b_ref[0, 0]


def _ada_mod(c_all, ada_w, ada_b):
    depth = ada_w.shape[0]
    nb = c_all.shape[0]
    return pl.pallas_call(
        _ada_kernel,
        grid=(depth, 6),
        in_specs=[pl.BlockSpec((nb, D_MODEL), lambda l, s: (0, 0)),
                  pl.BlockSpec((1, D_MODEL, D_MODEL), lambda l, s: (l, 0, s)),
                  pl.BlockSpec((1, 1, 1, D_MODEL), lambda l, s: (l, s, 0, 0))],
        out_specs=pl.BlockSpec((1, 1, nb, D_MODEL), lambda l, s: (l, s, 0, 0)),
        out_shape=jax.ShapeDtypeStruct((depth, 6, nb, D_MODEL), F32),
        compiler_params=_cparams(("arbitrary", "arbitrary")),
        name="ada_mod",
    )(c_all, ada_w, ada_b.reshape(depth, 6, 1, D_MODEL))


def _inproj_kernel(x_ref, sh_ref, sc_ref, w_ref, q_ref, k_ref, v_ref, qkvb_ref, z_ref, ba_ref, kh_ref, vh_ref, *, tb, tl):
    h = _ln(x_ref[...]) * (1.0 + sc_ref[...]) + sh_ref[...]
    hb = h.reshape(tb * tl, D_MODEL).astype(BF16)
    outs = dict(q=q_ref, k=k_ref, v=v_ref, qkvb=qkvb_ref, z=z_ref, ba=ba_ref)
    heads = dict(k=kh_ref, v=vh_ref)
    for name, lo, hi in IN_SEGS:
        r = jnp.dot(hb, w_ref[:, lo:hi], preferred_element_type=F32)
        outs[name][...] = r.reshape(tb, tl, hi - lo).astype(outs[name].dtype)
        if name in heads:
            heads[name][...] = r.reshape(tb, tl, H_A, DV_A)


def _in_proj(x, sh, sc, w_perm, tm):
    nb, seq, _ = x.shape
    tb, tl = _row_tile(nb, seq, tm)
    grid = (nb // tb, seq // tl)
    row = lambda w: pl.BlockSpec((tb, tl, w), lambda i, j: (i, j, 0))
    per_head = pl.BlockSpec((tb, tl, H_A, DV_A), lambda i, j: (i, j, 0, 0))
    mod = pl.BlockSpec((tb, 1, D_MODEL), lambda i, j: (i, 0, 0))
    widths = [hi - lo for _, lo, hi in IN_SEGS]
    dtypes = [BF16, F32, F32, F32, F32, F32]
    return pl.pallas_call(
        functools.partial(_inproj_kernel, tb=tb, tl=tl),
        grid=grid,
        in_specs=[row(D_MODEL), mod, mod, pl.BlockSpec((D_MODEL, D_IN_PAD), lambda i, j: (0, 0))],
        out_specs=[row(w) for w in widths] + [per_head, per_head],
        out_shape=[jax.ShapeDtypeStruct((nb, seq, w), dt) for w, dt in zip(widths, dtypes)]
                  + [jax.ShapeDtypeStruct((nb, seq, H_A, DV_A), F32)] * 2,
        compiler_params=_cparams(("arbitrary", "arbitrary")),
        name="in_proj",
    )(x, sh, sc, w_perm)


def _lam_value(lam_ref, lam_init):
    lp = lam_ref[...]
    a = jnp.sum(lp[0:1] * lp[1:2], axis=-1, keepdims=True)
    b = jnp.sum(lp[2:3] * lp[3:4], axis=-1, keepdims=True)
    return jnp.exp(a) - jnp.exp(b) + lam_init


def _sub_ln(o, g, lam_init):
    y = o * lax.rsqrt(jnp.mean(o * o, axis=-1, keepdims=True) + 1e-6)
    return (y * g) * (1.0 - lam_init)


def _attn_kernel(q_ref, k_ref, v_ref, sl_ref, lam_ref, g_ref, o_ref, kb_ref, vt_ref, qt_ref, *acc_refs, tq, lam_init):
    qi = pl.program_id(2)
    ngrp = 2 * tq // LANES
    nblk = k_ref.shape[1] // tq

    @pl.when(qi == 0)
    def _():
        kb_ref[...] = k_ref[0].astype(BF16)
        for j in range(nblk):
            vt_ref[j] = v_ref[0, j * tq:(j + 1) * tq, :].T.astype(BF16)

    qs = q_ref[0].astype(F32) * (DQK_A ** -0.5)
    lane = lax.broadcasted_iota(jnp.int32, (tq, 2 * DQK_A), 1)
    q1 = jnp.where(lane < DQK_A, qs, 0.0)
    q2 = jnp.where(lane >= DQK_A, qs, 0.0)
    for g in range(ngrp // 2):
        rows = slice(g * LANES, (g + 1) * LANES)
        qt_ref[:, g * LANES:(g + 1) * LANES] = q1[rows, :].T.astype(BF16)
        qt_ref[:, (ngrp // 2 + g) * LANES:(ngrp // 2 + g + 1) * LANES] = q2[rows, :].T.astype(BF16)
    for g in range(ngrp):
        acc_refs[g][...] = jnp.zeros_like(acc_refs[g])

    slope = sl_ref[0]
    krow = lax.broadcasted_iota(jnp.int32, (tq, LANES), 0)
    qcol = lax.broadcasted_iota(jnp.int32, (tq, LANES), 1)
    qt = qt_ref[...]

    def step(j, stats, masked):
        start = pl.multiple_of(j * tq, tq)
        st = lax.dot_general(kb_ref[pl.ds(start, tq), :], qt, (((1,), (0,)), ((), ())),
                             preferred_element_type=F32)
        vt = vt_ref[j]
        bias = slope * (krow + (j - qi) * tq).astype(F32)
        new_stats = []
        for g in range(ngrp):
            m_prev, l_prev = stats[g]
            s = st[:, g * LANES:(g + 1) * LANES] + bias
            if masked:
                s = jnp.where(krow <= qcol + (g * LANES) % tq, s, NEG_INF)
            m_new = jnp.maximum(m_prev, jnp.max(s, axis=0, keepdims=True))
            a = jnp.exp(m_prev - m_new)
            p = jnp.exp(s - m_new)
            l_new = a * l_prev + jnp.sum(p, axis=0, keepdims=True)
            acc_refs[g][...] = a * acc_refs[g][...] + jnp.dot(vt, p.astype(BF16), preferred_element_type=F32)
            new_stats.append((m_new, l_new))
        return tuple(new_stats)

    init = tuple((jnp.full((1, LANES), NEG_INF, F32), jnp.zeros((1, LANES), F32)) for _ in range(ngrp))
    stats = lax.fori_loop(0, qi, lambda j, c: step(j, c, False), init)
    stats = step(qi, stats, True)

    lam = _lam_value(lam_ref, lam_init)
    half = ngrp // 2
    for g in range(half):
        o1 = acc_refs[g][...] / stats[g][1]
        o2 = acc_refs[half + g][...] / stats[half + g][1]
        od = o1 - lam * o2
        y = od * lax.rsqrt(jnp.mean(od * od, axis=0, keepdims=True) + 1e-6) * g_ref[...] * (1.0 - lam_init)
        o_ref[0, g * LANES:(g + 1) * LANES, :] = y.T.astype(o_ref.dtype)


def _attn_prompt(q, k, v, slopes, lam_p, subln_g, lam_init, tq):
    nb, seq, _ = q.shape
    tq = min(tq, seq)
    assert seq % tq == 0 and tq % LANES == 0 and DV_A == LANES
    grid = (nb, H_A, seq // tq)
    g_cols = jnp.broadcast_to(subln_g.reshape(DV_A, 1), (DV_A, LANES))
    return pl.pallas_call(
        functools.partial(_attn_kernel, tq=tq, lam_init=lam_init),
        grid=grid,
        in_specs=[pl.BlockSpec((1, tq, DV_A), lambda b, h, i: (b, i, h)),
                  pl.BlockSpec((1, seq, DV_A), lambda b, h, i: (b, 0, h)),
                  pl.BlockSpec((1, seq, DV_A), lambda b, h, i: (b, 0, h)),
                  pl.BlockSpec((1, 1, LANES), lambda b, h, i: (h, 0, 0)),
                  pl.BlockSpec((4, DQK_A), lambda b, h, i: (0, 0)),
                  pl.BlockSpec((DV_A, LANES), lambda b, h, i: (0, 0))],
        out_specs=pl.BlockSpec((1, tq, DV_A), lambda b, h, i: (b, i, h)),
        out_shape=jax.ShapeDtypeStruct((nb, seq, D_A), BF16),
        scratch_shapes=[pltpu.VMEM((seq, DV_A), BF16), pltpu.VMEM((seq // tq, DV_A, tq), BF16),
                        pltpu.VMEM((2 * DQK_A, 2 * tq), BF16)]
                       + [pltpu.VMEM((DV_A, LANES), F32)] * (2 * tq // LANES),
        compiler_params=_cparams(("arbitrary", "arbitrary", "arbitrary")),
        name="attn_prompt",
    )(q, k, v, slopes, lam_p, g_cols)


PAGES_PER_STEP = 16
PAGE_ROWS = PAGE * H_A
HEAD_SHIFT = 2


def _dg_nt(a, b):
    return lax.dot_general(a, b, (((1,), (1,)), ((), ())), preferred_element_type=F32)


def _decode_kernel(pt_ref, q_ref, kn_ref, vn_ref, lam_ref, g_ref, *rest, nq, past_len, lam_init, ngroups):
    kp_refs = rest[:PAGES_PER_STEP]
    vp_refs = rest[PAGES_PER_STEP:2 * PAGES_PER_STEP]
    o_ref = rest[2 * PAGES_PER_STEP]
    qr_ref, m_ref, l_ref, acc_ref, pad_k_ref, pad_v_ref = rest[2 * PAGES_PER_STEP + 1:]
    g = pl.program_id(1)
    rows = 2 * H_A * nq

    row = lax.broadcasted_iota(jnp.int32, (rows, 1), 0)
    hm = jnp.zeros_like(row)
    for t in range(1, 2 * H_A):
        hm = hm + (row >= t * nq).astype(jnp.int32)
    qpos = past_len + (row - nq * hm)
    rhead = jnp.zeros_like(row)
    for h in range(1, H_A):
        rhead = rhead + (row >= 2 * h * nq).astype(jnp.int32)
    slope = jnp.zeros((rows, 1), F32)
    for h in range(H_A):
        slope = jnp.where(rhead == h, 2.0 ** (-8.0 * (h + 1) / H_A), slope)

    @pl.when(g == 0)
    def _():
        lane = lax.broadcasted_iota(jnp.int32, (nq, 2 * DQK_A), 1)
        for h in range(H_A):
            qh = (q_ref[0, :, h * DV_A:(h + 1) * DV_A].astype(F32) * (DQK_A ** -0.5)).astype(BF16)
            zero = jnp.zeros_like(qh)
            qr_ref[(2 * h) * nq:(2 * h + 1) * nq, :] = jnp.where(lane < DQK_A, qh, zero)
            qr_ref[(2 * h + 1) * nq:(2 * h + 2) * nq, :] = jnp.where(lane >= DQK_A, qh, zero)
        m_ref[...] = jnp.full_like(m_ref, NEG_INF)
        l_ref[...] = jnp.zeros_like(l_ref)
        acc_ref[...] = jnp.zeros_like(acc_ref)

    qr = qr_ref[...]

    def update(kbs, vbs, base_pos, causal):
        n = len(kbs)
        ncol = n * PAGE_ROWS
        s = jnp.concatenate([_dg_nt(qr, kb) for kb in kbs], axis=1)
        col = lax.broadcasted_iota(jnp.int32, (1, ncol), 1)
        ctok = col >> HEAD_SHIFT
        kpos = base_pos + ctok
        valid = (col - (ctok << HEAD_SHIFT)) == rhead
        if causal:
            valid = valid & (kpos <= qpos) & (ctok < nq)
        s = jnp.where(valid, s - slope * (qpos - kpos).astype(F32), NEG_INF)
        m_prev = m_ref[...]
        m_new = jnp.maximum(m_prev, jnp.max(s, axis=-1, keepdims=True))
        a = jnp.exp(m_prev - m_new)
        p = jnp.exp(s - jnp.concatenate([m_new] * (ncol // LANES), axis=1)).astype(BF16)
        l_ref[...] = a * l_ref[...] + jnp.sum(p.astype(F32), axis=-1, keepdims=True)
        pv = jnp.dot(p[:, 0:PAGE_ROWS], vbs[0], preferred_element_type=F32)
        for i in range(1, n):
            pv = pv + jnp.dot(p[:, i * PAGE_ROWS:(i + 1) * PAGE_ROWS], vbs[i], preferred_element_type=F32)
        acc_ref[...] = a * acc_ref[...] + pv
        m_ref[...] = m_new

    update([r[...].astype(BF16) for r in kp_refs], [r[...].astype(BF16) for r in vp_refs],
           g * (PAGES_PER_STEP * PAGE), False)

    @pl.when(g == ngroups - 1)
    def _():
        pad_k_ref[...] = jnp.zeros_like(pad_k_ref)
        pad_v_ref[...] = jnp.zeros_like(pad_v_ref)
        pad_k_ref[0:nq * H_A, :] = kn_ref[0]
        pad_v_ref[0:nq * H_A, :] = vn_ref[0]
        update([pad_k_ref[...].astype(BF16)], [pad_v_ref[...].astype(BF16)], past_len, True)
        o = acc_ref[...] / l_ref[...]
        lam = _lam_value(lam_ref, lam_init)
        for h in range(H_A):
            o1 = o[(2 * h) * nq:(2 * h + 1) * nq, :]
            o2 = o[(2 * h + 1) * nq:(2 * h + 2) * nq, :]
            y = _sub_ln(o1 - lam * o2, g_ref[...], lam_init)
            o_ref[0, :, h * DV_A:(h + 1) * DV_A] = y.astype(o_ref.dtype)


def _attn_decode(q, k_new, v_new, cache_k, cache_v, layer, pool, page_table, lam_p, subln_g, lam_init):
    nb, nq, _ = q.shape
    n_pages = page_table.shape[1]
    assert n_pages % PAGES_PER_STEP == 0 and nq * H_A <= PAGE_ROWS and H_A == 1 << HEAD_SHIFT
    ngroups = n_pages // PAGES_PER_STEP
    past_len = n_pages * PAGE
    rows = 2 * H_A * nq

    def page_spec(i):
        return pl.BlockSpec((PAGE_ROWS, DV_A),
                            lambda b, g, pt: (layer * pool + pt[b, g * PAGES_PER_STEP + i], 0))

    per_b = lambda r, w: pl.BlockSpec((1, r, w), lambda b, g, pt: (b, 0, 0))
    grid_spec = pltpu.PrefetchScalarGridSpec(
        num_scalar_prefetch=1,
        grid=(nb, ngroups),
        in_specs=[per_b(nq, D_A), per_b(nq * H_A, DV_A), per_b(nq * H_A, DV_A),
                  pl.BlockSpec((4, DQK_A), lambda b, g, pt: (0, 0)),
                  pl.BlockSpec((1, DV_A), lambda b, g, pt: (0, 0))]
                 + [page_spec(i) for i in range(PAGES_PER_STEP)]
                 + [page_spec(i) for i in range(PAGES_PER_STEP)],
        out_specs=per_b(nq, D_A),
        scratch_shapes=[pltpu.VMEM((rows, DV_A), BF16), pltpu.VMEM((rows, LANES), F32), pltpu.VMEM((rows, LANES), F32),
                        pltpu.VMEM((rows, DV_A), F32), pltpu.VMEM((PAGE_ROWS, DV_A), F32),
                        pltpu.VMEM((PAGE_ROWS, DV_A), F32)],
    )
    return pl.pallas_call(
        functools.partial(_decode_kernel, nq=nq, past_len=past_len, lam_init=lam_init, ngroups=ngroups),
        grid_spec=grid_spec,
        out_shape=jax.ShapeDtypeStruct((nb, nq, D_A), BF16),
        compiler_params=_cparams(("arbitrary", "arbitrary")),
        name="attn_decode",
    )(page_table, q, k_new.reshape(nb, nq * H_A, DV_A), v_new.reshape(nb, nq * H_A, DV_A), lam_p,
      subln_g.reshape(1, DV_A), *([cache_k] * PAGES_PER_STEP), *([cache_v] * PAGES_PER_STEP))


def _gdn_step_kernel(x_ref, cb_ref, cw_ref, ba_ref, z_ref, s0_ref, hp_ref, ng_ref, o_ref, sn_ref, xx_ref, *, seq):
    width = 3 * D_B
    hist = SUBLANES
    xx_ref[0:hist, :] = jnp.zeros((hist, width), F32)
    xx_ref[hist - (CONV_W - 1):hist, :] = cb_ref[0]
    xx_ref[hist:hist + seq, :] = x_ref[0]
    cw = cw_ref[...]
    y = cw[CONV_W - 1:CONV_W] * xx_ref[hist:hist + seq, :]
    for i in range(CONV_W - 1):
        off = hist - (CONV_W - 1) + i
        y = y + cw[i:i + 1] * xx_ref[off:off + seq, :]
    y = _silu(y)

    def l2n(v):
        return v * lax.rsqrt(jnp.sum(v * v, axis=-1, keepdims=True) + 1e-6)

    ba = ba_ref[0]
    ng = ng_ref[...]
    for h in range(H_B):
        hc = slice(h * DK_B, (h + 1) * DK_B)
        q = l2n(y[:, h * DK_B:(h + 1) * DK_B]) * (DK_B ** -0.5)
        k = l2n(y[:, D_B + h * DK_B:D_B + (h + 1) * DK_B])
        v = y[:, 2 * D_B + h * DV_B:2 * D_B + (h + 1) * DV_B]
        beta = jax.nn.sigmoid(ba[:, h:h + 1])
        xsp = ba[:, H_B + h:H_B + h + 1] + hp_ref[h, 1:2, :]
        decay = jnp.exp(-jnp.exp(hp_ref[h, 0:1, :]) * (jnp.maximum(xsp, 0.0) + jnp.log1p(jnp.exp(-jnp.abs(xsp)))))
        kt = k.T
        qt = q.T
        bv = beta * v
        state = s0_ref[0, h]
        outs = []
        for t in range(seq):
            kcol = kt[:, t:t + 1]
            state = state * decay[t:t + 1, :]
            r = jnp.sum(kcol * state, axis=0, keepdims=True)
            state = state + kcol * (bv[t:t + 1, :] - beta[t:t + 1, :] * r)
            outs.append(jnp.sum(qt[:, t:t + 1] * state, axis=0, keepdims=True))
        o = jnp.concatenate(outs, axis=0)
        sn_ref[0, h] = state
        yo = o * lax.rsqrt(jnp.mean(o * o, axis=-1, keepdims=True) + 1e-6) * ng
        o_ref[0, :, hc] = (yo * _silu(z_ref[0, :, hc])).astype(o_ref.dtype)


def _gated_deltanet_step(qkvb, ba, z, conv_buf, s0, conv_w, head_params, norm_g):
    nb, seq, width = qkvb.shape
    row = lambda w: pl.BlockSpec((1, seq, w), lambda b: (b, 0, 0))
    st = pl.BlockSpec((1, H_B, DK_B, DV_B), lambda b: (b, 0, 0, 0))
    return pl.pallas_call(
        functools.partial(_gdn_step_kernel, seq=seq),
        grid=(nb,),
        in_specs=[row(width),
                  pl.BlockSpec((1, CONV_W - 1, width), lambda b: (b, 0, 0)),
                  pl.BlockSpec((CONV_W, width), lambda b: (0, 0)),
                  row(LANES), row(D_B), st,
                  pl.BlockSpec((H_B, 2, LANES), lambda b: (0, 0, 0)),
                  pl.BlockSpec((1, DV_B), lambda b: (0, 0))],
        out_specs=[row(D_B), st],
        out_shape=[jax.ShapeDtypeStruct((nb, seq, D_B), BF16),
                   jax.ShapeDtypeStruct((nb, H_B, DK_B, DV_B), F32)],
        scratch_shapes=[pltpu.VMEM((seq + SUBLANES, width), F32)],
        compiler_params=_cparams(("arbitrary",)),
        name="gated_deltanet_step",
    )(qkvb, conv_buf, conv_w, ba, z, s0, head_params, norm_g.reshape(1, DV_B))


GDN_TILE = 512
GDN_SUPER = 256


def _dg(a, b, dims):
    return lax.dot_general(a, b, (dims, ((), ())), preferred_element_type=F32)


_NN = ((1,), (0,))
_NT = ((1,), (1,))
_TN = ((0,), (0,))


def _mm_exact_lhs(t, b):
    b1 = b.astype(BF16)
    r1 = b - b1.astype(F32)
    b2 = r1.astype(BF16)
    b3 = (r1 - b2.astype(F32)).astype(BF16)
    return _dg(t, b1, _NN) + (_dg(t, b2, _NN) + _dg(t, b3, _NN))


def _prod(a, b):
    return _dg(a.astype(BF16), b.astype(BF16), _NN)


def _unit_lower_inverse_minus_eye(lmat, size, csz):
    r = lax.broadcasted_iota(jnp.int32, (size, size), 0)
    c = lax.broadcasted_iota(jnp.int32, (size, size), 1)
    ld = jnp.where((r // SUBLANES) == (c // SUBLANES), lmat, 0.0)
    p2 = _prod(ld, ld)
    d = (p2 - ld) - _prod(ld, p2)
    p4 = _prod(p2, p2)
    d = d + p4 + _prod(d, p4)
    e = lmat - ld
    n = e + _prod(d, e)
    out = -n
    pw = n
    k = 2
    while k < csz // SUBLANES:
        pw = _prod(pw, pw)
        out = out + pw + _prod(out, pw)
        k *= 2
    return out + d + _prod(out, d)


def _gdn_long_kernel(x_ref, cb_ref, cw_ref, ba_ref, z_ref, s0_ref, hp_ref, ng_ref, o_ref, sn_ref,
                     xx_ref, y_ref, st_ref, u_ref, w_ref, qd_ref, kd_ref, gt_ref, qk_ref, *, tl):
    t = pl.program_id(1)
    width = 3 * D_B
    hist = SUBLANES

    @pl.when(t == 0)
    def _():
        xx_ref[0:hist, :] = jnp.zeros((hist, width), F32)
        xx_ref[hist - (CONV_W - 1):hist, :] = cb_ref[0]
        st_ref[...] = s0_ref[0]

    @pl.when(t > 0)
    def _():
        xx_ref[0:hist, :] = xx_ref[tl:tl + hist, :]

    xx_ref[hist:hist + tl, :] = x_ref[0]

    cw = cw_ref[...]
    blk = 256
    for r0 in range(0, tl, blk):
        y = cw[CONV_W - 1:CONV_W] * xx_ref[hist + r0:hist + r0 + blk, :]
        for i in range(CONV_W - 1):
            off = hist - (CONV_W - 1) + i + r0
            y = y + cw[i:i + 1] * xx_ref[off:off + blk, :]
        y_ref[r0:r0 + blk, :] = _silu(y)

    sup = GDN_SUPER
    r = lax.broadcasted_iota(jnp.int32, (sup, sup), 0)
    c = lax.broadcasted_iota(jnp.int32, (sup, sup), 1)
    same_chunk = (r // CHUNK) == (c // CHUNK)
    tril = same_chunk & (r >= c)
    strict = same_chunk & (r > c)
    diag = r == c
    sum_mat = jnp.concatenate([tril.astype(BF16), same_chunk.astype(BF16)], axis=0)

    def l2n(v):
        return v * lax.rsqrt(jnp.sum(v * v, axis=-1, keepdims=True) + 1e-6)

    for sc in range(tl // sup):
        rs = slice(sc * sup, (sc + 1) * sup)
        ba = ba_ref[0, rs, :]
        for h in range(H_B):
            hc = slice(h * DK_B, (h + 1) * DK_B)
            q = l2n(y_ref[rs, h * DK_B:(h + 1) * DK_B]) * (DK_B ** -0.5)
            k = l2n(y_ref[rs, D_B + h * DK_B:D_B + (h + 1) * DK_B])
            v = y_ref[rs, 2 * D_B + h * DV_B:2 * D_B + (h + 1) * DV_B]
            beta = jax.nn.sigmoid(ba[:, h:h + 1])
            xsp = ba[:, H_B + h:H_B + h + 1] + hp_ref[h, 1:2, :]
            gl = -jnp.exp(hp_ref[h, 0:1, :]) * (jnp.maximum(xsp, 0.0) + jnp.log1p(jnp.exp(-jnp.abs(xsp))))
            sums = _mm_exact_lhs(sum_mat, gl)
            gcum, gtot = sums[0:sup], sums[sup:2 * sup]
            gi = jnp.concatenate([gcum] * (sup // LANES), axis=1)
            gj = jnp.sum(jnp.where(diag, gi, 0.0), axis=0, keepdims=True)
            decay = jnp.where(tril, jnp.exp(jnp.where(tril, gi - gj, 0.0)), 0.0)
            eg = jnp.exp(gcum)
            kb = k * beta
            kbf = k.astype(BF16)
            lmat = jnp.where(strict, _dg(kb.astype(BF16), kbf, _NT) * decay, 0.0)
            rhs = jnp.concatenate([v * beta, kb * eg], axis=1)
            uw = rhs + _prod(_unit_lower_inverse_minus_eye(lmat, sup, CHUNK), rhs)
            u_ref[rs, hc] = uw[:, 0:DV_B]
            w_ref[rs, hc] = uw[:, DV_B:DV_B + DK_B]
            qk = _dg(q.astype(BF16), kbf, _NT) * decay
            qkc = qk[:, 0:CHUNK]
            for i in range(1, sup // CHUNK):
                qkc = qkc + qk[:, i * CHUNK:(i + 1) * CHUNK]
            qk_ref[h, rs, :] = qkc
            qd_ref[rs, hc] = q * eg
            kd_ref[rs, hc] = k * jnp.exp(gtot - gcum)
            gt_ref[rs, hc] = jnp.exp(gtot)

    ng = ng_ref[...]

    def chunk(ci, carry):
        start = ci * CHUNK
        rows = pl.ds(start, CHUNK)
        for h in range(H_B):
            hc = slice(h * DK_B, (h + 1) * DK_B)
            sb = st_ref[h].astype(BF16)
            v_new = u_ref[rows, hc] - _dg(w_ref[rows, hc].astype(BF16), sb, _NN)
            vb = v_new.astype(BF16)
            o = _dg(qd_ref[rows, hc].astype(BF16), sb, _NN) + _dg(qk_ref[h, rows, :].astype(BF16), vb, _NN)
            st_ref[h] = st_ref[h] * gt_ref[pl.ds(start, 1), hc] + _dg(kd_ref[rows, hc].astype(BF16), vb, _TN)
            y = o * lax.rsqrt(jnp.mean(o * o, axis=-1, keepdims=True) + 1e-6) * ng
            o_ref[0, rows, hc] = (y * _silu(z_ref[0, rows, hc])).astype(o_ref.dtype)
        return carry

    for ci in range(tl // CHUNK):
        chunk(ci, 0)
    sn_ref[0] = st_ref[...]


def _gated_deltanet_long(qkvb, ba, z, conv_buf, s0, conv_w, head_params, norm_g):
    nb, seq, width = qkvb.shape
    tl = GDN_TILE
    assert seq % tl == 0 and tl % GDN_SUPER == 0
    row = lambda w: pl.BlockSpec((1, tl, w), lambda b, t: (b, t, 0))
    st = pl.BlockSpec((1, H_B, DK_B, DV_B), lambda b, t: (b, 0, 0, 0))
    scr = lambda n, w: pltpu.VMEM((n, w), F32)
    return pl.pallas_call(
        functools.partial(_gdn_long_kernel, tl=tl),
        grid=(nb, seq // tl),
        in_specs=[row(width),
                  pl.BlockSpec((1, CONV_W - 1, width), lambda b, t: (b, 0, 0)),
                  pl.BlockSpec((CONV_W, width), lambda b, t: (0, 0)),
                  row(LANES), row(D_B), st,
                  pl.BlockSpec((H_B, 2, LANES), lambda b, t: (0, 0, 0)),
                  pl.BlockSpec((1, DV_B), lambda b, t: (0, 0))],
        out_specs=[row(D_B), st],
        out_shape=[jax.ShapeDtypeStruct((nb, seq, D_B), BF16),
                   jax.ShapeDtypeStruct((nb, H_B, DK_B, DV_B), F32)],
        scratch_shapes=[scr(tl + SUBLANES, width), scr(tl, width), pltpu.VMEM((H_B, DK_B, DV_B), F32),
                        scr(tl, D_B), scr(tl, D_B), scr(tl, D_B), scr(tl, D_B), scr(tl, D_B),
                        pltpu.VMEM((H_B, tl, CHUNK), F32)],
        compiler_params=_cparams(("arbitrary", "arbitrary")),
        name="gated_deltanet_long",
    )(qkvb, conv_buf, conv_w, ba, z, s0, head_params, norm_g.reshape(1, DV_B))


def _outproj_kernel(oa_ref, ob_ref, x_ref, w_ref, g1_ref, lg_ref, lb_ref, sh_ref, sc_ref, *rest, tb, tl, alpha, region):
    router = region is not None
    if router:
        wr_ref, br_ref, x1_ref, h2_ref, pair_ref, cnt_ref, run_ref = rest
    else:
        x1_ref, h2_ref = rest
    rows = tb * tl
    oa = oa_ref[...].reshape(rows, D_A)
    ob = ob_ref[...].reshape(rows, D_B)
    mix = (jnp.dot(oa, w_ref[0:D_A, :], preferred_element_type=F32)
           + jnp.dot(ob, w_ref[D_A:D_A + D_B, :], preferred_element_type=F32))
    y = alpha * x_ref[...] + g1_ref[...] * mix.reshape(tb, tl, D_MODEL)
    x1 = _ln(y) * lg_ref[...] + lb_ref[...]
    x1_ref[...] = x1
    h2 = (_ln(x1) * (1.0 + sc_ref[...]) + sh_ref[...]).astype(BF16)
    h2_ref[...] = h2
    if router:
        @pl.when((pl.program_id(0) == 0) & (pl.program_id(1) == 0))
        def _():
            run_ref[...] = jnp.zeros_like(run_ref)

        logits = jnp.dot(h2.reshape(rows, D_MODEL), wr_ref[...], preferred_element_type=F32) + br_ref[...]
        lane = lax.broadcasted_iota(jnp.int32, (rows, LANES), 1)
        logits = jnp.where(lane < N_EXPERTS, logits, NEG_INF)
        v1 = jnp.max(logits, axis=-1, keepdims=True)
        i1 = jnp.min(jnp.where(logits == v1, lane, LANES), axis=-1, keepdims=True)
        rest_l = jnp.where(lane == i1, NEG_INF, logits)
        v2 = jnp.max(rest_l, axis=-1, keepdims=True)
        i2 = jnp.min(jnp.where(rest_l == v2, lane, LANES), axis=-1, keepdims=True)
        e2 = jnp.exp(v2 - v1)
        gate1 = 1.0 / (1.0 + e2)
        gate2 = e2 / (1.0 + e2)
        sel = jnp.where((lane == i1) | (lane == i2), 1.0, 0.0)
        r = lax.broadcasted_iota(jnp.int32, (rows, rows), 0)
        c = lax.broadcasted_iota(jnp.int32, (rows, rows), 1)
        earlier = jnp.where(r > c, 1.0, 0.0).astype(BF16)
        rank = jnp.dot(earlier, sel.astype(BF16), preferred_element_type=F32) + run_ref[...]
        run_ref[...] = run_ref[...] + jnp.sum(sel, axis=0, keepdims=True)
        slot = lane.astype(F32) * float(region) + rank
        first, second = jnp.minimum(i1, i2), jnp.maximum(i1, i2)
        slot_a = jnp.sum(jnp.where(lane == first, slot, 0.0), axis=-1, keepdims=True)
        slot_b = jnp.sum(jnp.where(lane == second, slot, 0.0), axis=-1, keepdims=True)
        gate_a = jnp.where(first == i1, gate1, gate2)
        gate_b = jnp.where(first == i1, gate2, gate1)
        pair = (jnp.where(lane == 0, slot_a, 0.0) + jnp.where(lane == 1, slot_b, 0.0)
                + jnp.where(lane == 2, gate_a, 0.0) + jnp.where(lane == 3, gate_b, 0.0))
        pair_ref[...] = pair.reshape(tb, tl, LANES)
        cnt_ref[...] = run_ref[...]


def _out_proj(oa, ob, x, w_out_b, g1, ln_g, ln_b, sh2, sc2, alpha, tm, router_w=None, router_b=None, region=None):
    nb, seq, _ = x.shape
    tb, tl = _row_tile(nb, seq, tm)
    grid = (nb // tb, seq // tl)
    row = lambda w: pl.BlockSpec((tb, tl, w), lambda i, j: (i, j, 0))
    mod = pl.BlockSpec((tb, 1, D_MODEL), lambda i, j: (i, 0, 0))
    vec = pl.BlockSpec((1, 1, D_MODEL), lambda i, j: (0, 0, 0))
    in_specs = [row(D_A), row(D_B), row(D_MODEL), pl.BlockSpec((D_MODEL, D_MODEL), lambda i, j: (0, 0)),
                mod, vec, vec, mod, mod]
    args = [oa, ob, x, w_out_b, g1, ln_g.reshape(1, 1, D_MODEL), ln_b.reshape(1, 1, D_MODEL), sh2, sc2]
    out_specs = [row(D_MODEL), row(D_MODEL)]
    out_shape = [jax.ShapeDtypeStruct((nb, seq, D_MODEL), F32), jax.ShapeDtypeStruct((nb, seq, D_MODEL), BF16)]
    scratch = []
    if region is not None:
        in_specs += [pl.BlockSpec((D_MODEL, LANES), lambda i, j: (0, 0)), pl.BlockSpec((1, LANES), lambda i, j: (0, 0))]
        args += [router_w, router_b]
        out_specs += [row(LANES), pl.BlockSpec((1, LANES), lambda i, j: (0, 0))]
        out_shape += [jax.ShapeDtypeStruct((nb, seq, LANES), F32), jax.ShapeDtypeStruct((1, LANES), F32)]
        scratch = [pltpu.VMEM((1, LANES), F32)]
    return pl.pallas_call(
        functools.partial(_outproj_kernel, tb=tb, tl=tl, alpha=alpha, region=region),
        grid=grid, in_specs=in_specs, out_specs=out_specs, out_shape=out_shape, scratch_shapes=scratch,
        compiler_params=_cparams(("arbitrary", "arbitrary")),
        name="out_proj",
    )(*args)


def _ffn_kernel(h_ref, x_ref, wg_ref, wu_ref, wd_ref, g2_ref, lg_ref, lb_ref, o_ref, acc_ref, *, tb, tl, alpha):
    f = pl.program_id(2)
    rows = tb * tl

    @pl.when(f == 0)
    def _():
        acc_ref[...] = jnp.zeros_like(acc_ref)

    hb = h_ref[...].reshape(rows, D_MODEL)
    gt = jnp.dot(hb, wg_ref[...], preferred_element_type=F32)
    up = jnp.dot(hb, wu_ref[...], preferred_element_type=F32)
    a = (_silu(gt) * up).astype(BF16)
    acc_ref[...] += jnp.dot(a, wd_ref[...], preferred_element_type=F32)

    @pl.when(f == pl.num_programs(2) - 1)
    def _():
        y = alpha * x_ref[...] + g2_ref[...] * acc_ref[...].reshape(tb, tl, D_MODEL)
        o_ref[...] = _ln(y) * lg_ref[...] + lb_ref[...]


def _ffn(h2, x1, w_up_b, w_down_b, g2, ln_g, ln_b, alpha, tm, tf):
    nb, seq, _ = x1.shape
    tb, tl = _row_tile(nb, seq, tm)
    nf = D_FF // tf
    grid = (nb // tb, seq // tl, nf)
    row = lambda w: pl.BlockSpec((tb, tl, w), lambda i, j, f: (i, j, 0))
    return pl.pallas_call(
        functools.partial(_ffn_kernel, tb=tb, tl=tl, alpha=alpha),
        grid=grid,
        in_specs=[row(D_MODEL), row(D_MODEL),
                  pl.BlockSpec((D_MODEL, tf), lambda i, j, f: (0, f)),
                  pl.BlockSpec((D_MODEL, tf), lambda i, j, f: (0, nf + f)),
                  pl.BlockSpec((tf, D_MODEL), lambda i, j, f: (f, 0)),
                  pl.BlockSpec((tb, 1, D_MODEL), lambda i, j, f: (i, 0, 0)),
                  pl.BlockSpec((1, 1, D_MODEL), lambda i, j, f: (0, 0, 0)),
                  pl.BlockSpec((1, 1, D_MODEL), lambda i, j, f: (0, 0, 0))],
        out_specs=row(D_MODEL),
        out_shape=jax.ShapeDtypeStruct((nb, seq, D_MODEL), F32),
        scratch_shapes=[pltpu.VMEM((tb * tl, D_MODEL), F32)],
        compiler_params=_cparams(("arbitrary", "arbitrary", "arbitrary")),
        name="ffn",
    )(h2, x1, w_up_b, w_up_b, w_down_b, g2, ln_g.reshape(1, 1, D_MODEL), ln_b.reshape(1, 1, D_MODEL))


MOE_ROWS = 512
ROW_TILE = (SUBLANES, LANES)
ISSUE_UNROLL = 8


def _region_rows(m):
    return -(-m // MOE_ROWS) * MOE_ROWS


def _dispatch_kernel(slot_ref, pad_ref, h_ref, xs_ref, src_ref, zero_ref, sem, *, rows, npad):
    @pl.when((pl.program_id(0) == 0) & (pl.program_id(1) == 0))
    def _():
        zero_ref[...] = jnp.zeros_like(zero_ref)

        def pad_copy(k):
            return pltpu.make_async_copy(zero_ref.at[0], xs_ref.at[pad_ref[0, k]], sem)

        def issue_pad(k, c):
            pad_copy(k).start()
            return c

        def drain_pad(k, c):
            pad_copy(0).wait()
            return c

        lax.fori_loop(0, npad, issue_pad, 0)
        lax.fori_loop(0, npad, drain_pad, 0)

    src_ref[...] = h_ref[...].reshape(rows, D_MODEL).astype(F32).reshape(rows, *ROW_TILE)

    def row_copy(t, k):
        return pltpu.make_async_copy(src_ref.at[t], xs_ref.at[slot_ref[k, t]], sem)

    def issue(i, c):
        for u in range(ISSUE_UNROLL):
            t = i * ISSUE_UNROLL + u
            row_copy(t, 0).start()
            row_copy(t, 1).start()
        return c

    lax.fori_loop(0, rows // ISSUE_UNROLL, issue, 0)

    def drain(i, c):
        row_copy(0, 0).wait()
        row_copy(0, 1).wait()
        return c

    lax.fori_loop(0, rows, drain, 0)


def _dispatch(h2, slots, pad_slots, total_rows):
    nb, seq, _ = h2.shape
    tb, tl = _row_tile(nb, seq, MOE_ROWS)
    rows = tb * tl
    nj = seq // tl
    npad = pad_slots.shape[1]
    return pl.pallas_call(
        functools.partial(_dispatch_kernel, rows=rows, npad=npad),
        grid=(nb // tb, nj),
        in_specs=[pl.BlockSpec((2, rows), lambda i, j: (0, i * nj + j), memory_space=pltpu.SMEM),
                  pl.BlockSpec((1, npad), lambda i, j: (0, 0), memory_space=pltpu.SMEM),
                  pl.BlockSpec((tb, tl, D_MODEL), lambda i, j: (i, j, 0))],
        out_specs=pl.BlockSpec(memory_space=pl.ANY),
        out_shape=jax.ShapeDtypeStruct((total_rows, *ROW_TILE), F32),
        scratch_shapes=[pltpu.VMEM((rows, *ROW_TILE), F32), pltpu.VMEM((1, *ROW_TILE), F32),
                        pltpu.SemaphoreType.DMA(())],
        compiler_params=_cparams(("arbitrary", "arbitrary")),
        name="moe_dispatch",
    )(slots, pad_slots, h2)


def _experts_kernel(blk_e_ref, blk_rows_ref, nused_ref, x_ref, wg_ref, wu_ref, wd_ref, o_ref, *, nsub):
    i = pl.program_id(0)
    nrows = blk_rows_ref[i]

    @pl.when(nrows > 0)
    def _():
        x = x_ref[...].reshape(MOE_ROWS, D_MODEL)
        row = lax.broadcasted_iota(jnp.int32, (MOE_ROWS, 1), 0)
        xb = jnp.where(row < nrows, x, 0.0).astype(BF16)
        sub = D_FF // nsub
        acc = None
        for c in range(nsub):
            cols = slice(c * sub, (c + 1) * sub)
            gt = jnp.dot(xb, wg_ref[0, :, cols], preferred_element_type=F32)
            up = jnp.dot(xb, wu_ref[0, :, cols], preferred_element_type=F32)
            a = (_silu(gt) * up).astype(BF16)
            part = jnp.dot(a, wd_ref[0, cols, :], preferred_element_type=F32)
            acc = part if acc is None else acc + part
        o_ref[...] = acc.reshape(MOE_ROWS, *ROW_TILE)

    @pl.when(nrows == 0)
    def _():
        o_ref[...] = jnp.zeros_like(o_ref)


def _experts(xs, blk_e, blk_rows, nused, w_up_b, w_down_b, nsub):
    nblk = xs.shape[0] // MOE_ROWS

    def xi(i, be, br, nu):
        return jnp.minimum(i, nu[0] - 1)

    def ei(i, be, br, nu):
        return be[jnp.minimum(i, nu[0] - 1)]

    grid_spec = pltpu.PrefetchScalarGridSpec(
        num_scalar_prefetch=3,
        grid=(nblk,),
        in_specs=[pl.BlockSpec((MOE_ROWS, *ROW_TILE), lambda *a: (xi(*a), 0, 0)),
                  pl.BlockSpec((1, D_MODEL, D_FF), lambda *a: (ei(*a), 0, 0)),
                  pl.BlockSpec((1, D_MODEL, D_FF), lambda *a: (ei(*a), 0, 1)),
                  pl.BlockSpec((1, D_FF, D_MODEL), lambda *a: (ei(*a), 0, 0))],
        out_specs=pl.BlockSpec((MOE_ROWS, *ROW_TILE), lambda i, be, br, nu: (i, 0, 0)),
    )
    return pl.pallas_call(
        functools.partial(_experts_kernel, nsub=nsub), grid_spec=grid_spec,
        out_shape=jax.ShapeDtypeStruct(xs.shape, F32),
        compiler_params=_cparams(("arbitrary",)),
        name="moe_experts",
    )(blk_e, blk_rows, nused, xs, w_up_b, w_up_b, w_down_b)


def _moe_layout(pair, cnt, m, region):
    raw = pair[..., 0:2].reshape(m, 2).astype(jnp.int32)
    e = raw // region
    counts = cnt[0, :N_EXPERTS].astype(jnp.int32)
    padded = (counts + MOE_ROWS - 1) // MOE_ROWS * MOE_ROWS
    off = jnp.cumsum(padded) - padded
    slots = (off[e] + (raw - e * region)).T
    nblk = (2 * m + MOE_ROWS - 1) // MOE_ROWS + N_EXPERTS
    bstart = jnp.arange(nblk, dtype=jnp.int32) * MOE_ROWS
    blk_e = jnp.minimum(jnp.sum((bstart[:, None] >= (off + padded)[None, :]).astype(jnp.int32), axis=1), N_EXPERTS - 1)
    blk_rows = jnp.clip(counts[blk_e] - (bstart - off[blk_e]), 0, MOE_ROWS)
    nused = (jnp.sum(padded) // MOE_ROWS).reshape(1)
    npad = nblk * MOE_ROWS - 2 * m
    run_start = jnp.concatenate([off + counts, jnp.sum(padded).reshape(1)])
    run_len = jnp.concatenate([padded - counts, (nblk * MOE_ROWS - jnp.sum(padded)).reshape(1)])
    run_end = jnp.cumsum(run_len)
    k = jnp.arange(npad, dtype=jnp.int32)[:, None]
    in_run = (k >= (run_end - run_len)[None, :]) & (k < run_end[None, :])
    pad_slots = jnp.sum(jnp.where(in_run, run_start[None, :] + k - (run_end - run_len)[None, :], 0), axis=1)
    return slots, pad_slots.astype(jnp.int32).reshape(1, npad), blk_e, blk_rows, nused


def _combine_kernel(slot_ref, ys_ref, pair_ref, x_ref, g2_ref, lg_ref, lb_ref, o_ref, buf0, buf1, sem, *, tb, tl, alpha):
    rows = tb * tl

    def row_copy(t, k):
        return pltpu.make_async_copy(ys_ref.at[slot_ref[k, t]], (buf0 if k == 0 else buf1).at[t], sem)

    def issue(i, c):
        for u in range(ISSUE_UNROLL):
            t = i * ISSUE_UNROLL + u
            row_copy(t, 0).start()
            row_copy(t, 1).start()
        return c

    lax.fori_loop(0, rows // ISSUE_UNROLL, issue, 0)

    def drain(i, c):
        row_copy(0, 0).wait()
        row_copy(0, 1).wait()
        return c

    lax.fori_loop(0, rows, drain, 0)
    pair = pair_ref[...].reshape(rows, LANES)
    ga, gb = pair[:, 2:3], pair[:, 3:4]
    f = ga * buf0[...].reshape(rows, D_MODEL) + gb * buf1[...].reshape(rows, D_MODEL)
    y = alpha * x_ref[...] + g2_ref[...] * f.reshape(tb, tl, D_MODEL)
    o_ref[...] = _ln(y) * lg_ref[...] + lb_ref[...]


def _combine(ys, slots, pair, x1, g2, ln_g, ln_b, alpha):
    nb, seq, _ = x1.shape
    tb, tl = _row_tile(nb, seq, MOE_ROWS)
    rows = tb * tl
    nj = seq // tl
    row = lambda w: pl.BlockSpec((tb, tl, w), lambda i, j: (i, j, 0))
    return pl.pallas_call(
        functools.partial(_combine_kernel, tb=tb, tl=tl, alpha=alpha),
        grid=(nb // tb, nj),
        in_specs=[pl.BlockSpec((2, rows), lambda i, j: (0, i * nj + j), memory_space=pltpu.SMEM),
                  pl.BlockSpec(memory_space=pl.ANY), row(LANES), row(D_MODEL),
                  pl.BlockSpec((tb, 1, D_MODEL), lambda i, j: (i, 0, 0)),
                  pl.BlockSpec((1, 1, D_MODEL), lambda i, j: (0, 0, 0)),
                  pl.BlockSpec((1, 1, D_MODEL), lambda i, j: (0, 0, 0))],
        out_specs=row(D_MODEL),
        out_shape=jax.ShapeDtypeStruct((nb, seq, D_MODEL), F32),
        scratch_shapes=[pltpu.VMEM((rows, *ROW_TILE), F32), pltpu.VMEM((rows, *ROW_TILE), F32),
                        pltpu.SemaphoreType.DMA(())],
        compiler_params=_cparams(("arbitrary", "arbitrary")),
        name="moe_combine",
    )(slots, ys, pair, x1, g2, ln_g.reshape(1, 1, D_MODEL), ln_b.reshape(1, 1, D_MODEL))


def _trunk(x, mod, conv_bufs, delta_states, past, weights, tm):
    depth = weights["w_in"].shape[0]
    alpha = (2 * depth) ** 0.25
    nb, seq, _ = x.shape
    new_k, new_v, new_conv, new_delta = [], [], [], []
    slopes = jnp.broadcast_to(
        jnp.asarray([2.0 ** (-8.0 * (h + 1) / H_A) for h in range(H_A)], F32)[:, None, None], (H_A, 1, LANES))
    for l in range(depth):
        sh1, sc1, g1, sh2, sc2, g2 = [mod[l, s] for s in range(6)]
        lam_init = 0.8 - 0.6 * math.exp(-0.3 * l)
        q, k, v, qkvb, z, ba, k_heads, v_heads = _in_proj(x, sh1, sc1, weights["w_in"][l], tm)
        new_k.append(k_heads)
        new_v.append(v_heads)
        if past is None:
            o_a = _attn_prompt(q, k, v, slopes, weights["lam_params"][l], weights["subln_a"][l], lam_init, 512)
        else:
            cache_k, cache_v, pool, page_table = past
            o_a = _attn_decode(q, k, v, cache_k, cache_v, l, pool, page_table,
                               weights["lam_params"][l], weights["subln_a"][l], lam_init)
        gdn = _gated_deltanet_long if seq % GDN_TILE == 0 else _gated_deltanet_step
        o_b, s_new = gdn(qkvb, ba, z, conv_bufs[l], delta_states[l], weights["conv_w"][l],
                         weights["head_params"][l], weights["norm_b"][l])
        new_conv.append(qkvb[:, seq - (CONV_W - 1):, :] if seq >= CONV_W - 1 else
                        jnp.concatenate([conv_bufs[l], qkvb], axis=1)[:, -(CONV_W - 1):, :])
        new_delta.append(s_new)
        j = l // 2
        if l % 2 == 0:
            x1, h2 = _out_proj(o_a, o_b, x, weights["w_out"][l], g1, weights["ln1_g"][l], weights["ln1_b"][l],
                               sh2, sc2, alpha, tm)
            x = _ffn(h2, x1, weights["dense_w_up"][j], weights["dense_w_down"][j], g2,
                     weights["ln2_g"][l], weights["ln2_b"][l], alpha, tm, D_FF // 2)
        else:
            region = _region_rows(nb * seq)
            x1, h2, pair, cnt = _out_proj(o_a, o_b, x, weights["w_out"][l], g1, weights["ln1_g"][l],
                                          weights["ln1_b"][l], sh2, sc2, alpha, MOE_ROWS,
                                          weights["router_w"][j], weights["router_b"][j], region)
            slots, pad_slots, blk_e, blk_rows, nused = _moe_layout(pair, cnt, nb * seq, region)
            xs = _dispatch(h2, slots, pad_slots, blk_e.shape[0] * MOE_ROWS)
            ys = _experts(xs, blk_e, blk_rows, nused, weights["moe_w_up"][j], weights["moe_w_down"][j], 2)
            x = _combine(ys, slots, pair, x1, g2, weights["ln2_g"][l], weights["ln2_b"][l], alpha)
    return x, jnp.stack(new_k), jnp.stack(new_v), jnp.stack(new_conv), jnp.stack(new_delta)


def kernel(x_prompt, x_sample, cache_k, cache_v, state_conv, state_delta, page_table, c_prompt, c_sample,
           w_in, w_out, lam_params, subln_a, conv_w, a_log, dt_bias, norm_b, ada_w, ada_b,
           ln1_g, ln1_b, ln2_g, ln2_b, dense_w_up, dense_w_down, moe_router, moe_router_b, moe_w_up, moe_w_down):
    depth = w_in.shape[0]
    nbp, nbs = x_prompt.shape[0], x_sample.shape[0]
    a_q, a_k, a_v, b_qkv = 2 * H_A * DQK_A, 2 * H_A * DQK_A, H_A * DV_A, 3 * D_B
    n0 = a_q + a_k + a_v + b_qkv
    w_perm = jnp.concatenate(
        [w_in[:, :, :n0], w_in[:, :, n0 + 2 * H_B:], w_in[:, :, n0:n0 + 2 * H_B],
         jnp.zeros((depth, D_MODEL, LANES - 2 * H_B), w_in.dtype)], axis=-1).astype(BF16)
    n_moe = moe_router.shape[0]
    router_w = jnp.concatenate([moe_router, jnp.zeros((n_moe, D_MODEL, LANES - N_EXPERTS), moe_router.dtype)],
                               axis=-1).astype(BF16)
    router_b = jnp.concatenate([moe_router_b, jnp.zeros((n_moe, LANES - N_EXPERTS), moe_router_b.dtype)],
                               axis=-1).reshape(n_moe, 1, LANES)
    head_params = jnp.broadcast_to(jnp.stack([a_log, dt_bias], axis=-1)[..., None], (depth, H_B, 2, LANES))
    weights = dict(w_in=w_perm, w_out=w_out.astype(BF16), lam_params=lam_params, subln_a=subln_a, conv_w=conv_w,
                   head_params=head_params, norm_b=norm_b, ln1_g=ln1_g, ln1_b=ln1_b, ln2_g=ln2_g, ln2_b=ln2_b,
                   dense_w_up=dense_w_up.astype(BF16), dense_w_down=dense_w_down.astype(BF16),
                   router_w=router_w, router_b=router_b,
                   moe_w_up=moe_w_up.astype(BF16), moe_w_down=moe_w_down.astype(BF16))

    mod = _ada_mod(jnp.concatenate([c_prompt, c_sample], axis=0), ada_w, ada_b)
    mod = mod[:, :, :, None, :]
    conv0 = jnp.zeros((depth, nbp, CONV_W - 1, 3 * D_B), x_prompt.dtype)
    delta0 = jnp.zeros((depth, nbp, H_B, DK_B, DV_B), x_prompt.dtype)
    y_p, k_p, v_p, conv_p, delta_p = _trunk(x_prompt, mod[:, :, :nbp], conv0, delta0, None, weights, 512)

    n_pool = cache_k.shape[1]
    ck = cache_k.reshape(depth * n_pool * PAGE * H_A, 2 * DQK_A)
    cv = cache_v.reshape(depth * n_pool * PAGE * H_A, DV_A)
    y_s, k_s, v_s, conv_s, delta_s = _trunk(x_sample, mod[:, :, nbp:], state_conv, state_delta,
                                            (ck, cv, n_pool, page_table), weights, 512)
    return (y_p, y_s, k_p, v_p, conv_p, delta_p, k_s, v_s, conv_s, delta_s)
```
